```python
import math
import jax, jax.numpy as jnp
from jax import lax
import numpy as np

D_MODEL = 4096
BATCH = 4
SEQ = 2048
DEPTH = 2
DEC_BATCH = 128
DEC_SEQ = 8
PAST_LEN = 16384
PAGE_SIZE = 128

W_A = D_MODEL // 4
W_B = D_MODEL // 2
W_C = D_MODEL // 4
MIX = W_A + W_B + W_C
CONV_A = 3
CONV_B = 4
LRU_HEADS = 16
LRU_HD = W_B // LRU_HEADS
LRU_C = 8.0
S5_GROUP = 16
S5_G = W_C // S5_GROUP
S5_P = 64
EPS = 1e-6
SPLITS = (W_A, W_A, W_A, W_A, W_B, W_B, W_C, W_C)
IN_COLS = 4 * W_A + 2 * W_B + 2 * W_C

kernel_name = "hybrid_conv_rglru_s5_decoder_step"


def _split_cols(z):
    out = []
    start = 0
    for w in SPLITS:
        out.append(z[..., start:start + w])
        start += w
    return out


def _rmsnorm(x, g):
    xf = x.astype(jnp.float32)
    y = xf * lax.rsqrt(jnp.mean(xf * xf, axis=-1, keepdims=True) + EPS)
    return (y * g.astype(jnp.float32)).astype(x.dtype)


def _causal_dwconv(buf, v, w, b):
    k_width = w.shape[0]
    t_len = v.shape[1]
    padded = jnp.concatenate([buf.astype(v.dtype), v], axis=1)
    y = b + sum(w[k] * padded[:, k:k + t_len] for k in range(k_width))
    return y, padded[:, t_len:]


def _real_combine(e1, e2):
    a1, b1 = e1
    a2, b2 = e2
    return a1 * a2, a2 * b1 + b2


def _complex_combine(e1, e2):
    ar1, ai1, br1, bi1 = e1
    ar2, ai2, br2, bi2 = e2
    return (ar2 * ar1 - ai2 * ai1,
            ar2 * ai1 + ai2 * ar1,
            ar2 * br1 - ai2 * bi1 + br2,
            ar2 * bi1 + ai2 * br1 + bi2)


def _layer(x, c, conv_a, conv_b, lru_h, s5_re, s5_im,
           g_norm, w_ada, b_ada, w_in, w_conv_a, b_conv_a, w_conv_b, b_conv_b,
           w_rg, b_rg, w_ig, b_ig, lru_lambda, s5_lambda_re, s5_lambda_im, s5_log_dt,
           s5_b_re, s5_b_im, s5_c_re, s5_c_im, s5_d, w_glu, b_glu, w_out):
    f32 = jnp.float32
    bn, t_len, _ = x.shape
    mod = c @ w_ada + b_ada
    shift, scale, gate = jnp.split(mod, 3, axis=-1)
    h = _rmsnorm(x, g_norm) * (1.0 + scale[:, None]) + shift[:, None]
    proj = h @ w_in
    a_b, a_c, a_x, a_g, b_x, b_g, c_u, c_g = _split_cols(proj)

    conv_in = a_c * a_x
    conv_out, conv_a_new = _causal_dwconv(conv_a, conv_in, w_conv_a, b_conv_a)
    out_a = a_b * conv_out * jax.nn.silu(a_g)

    xb, conv_b_new = _causal_dwconv(conv_b, b_x, w_conv_b, b_conv_b)
    xh = xb.reshape(bn, t_len, LRU_HEADS, LRU_HD)
    r = jax.nn.sigmoid(jnp.einsum('bthi,hij->bthj', xh, w_rg) + b_rg).reshape(bn, t_len, W_B)
    ig = jax.nn.sigmoid(jnp.einsum('bthi,hij->bthj', xh, w_ig) + b_ig).reshape(bn, t_len, W_B)
    log_a = -LRU_C * r.astype(f32) * jax.nn.softplus(-lru_lambda.astype(f32))
    a = jnp.exp(log_a)
    beta = jnp.sqrt(-jnp.expm1(2.0 * log_a))
    bt = beta * ig.astype(f32) * xb.astype(f32)
    bt = bt.at[:, 0].add(a[:, 0] * lru_h.astype(f32))
    _, hs = lax.associative_scan(_real_combine, (a, bt), axis=1)
    lru_h_new = hs[:, -1]
    out_b = hs.astype(x.dtype) * jax.nn.silu(b_g)

    dt = jnp.exp(s5_log_dt.astype(f32))[:, None]
    lre = s5_lambda_re.astype(f32)
    lim = s5_lambda_im.astype(f32)
    mag = jnp.exp(lre * dt)
    ab_re = mag * jnp.cos(lim * dt)
    ab_im = mag * jnp.sin(lim * dt)
    nr = ab_re - 1.0
    den = lre * lre + lim * lim
    fr = (nr * lre + ab_im * lim) / den
    fi = (ab_im * lre - nr * lim) / den
    bre = s5_b_re.astype(f32)
    bim = s5_b_im.astype(f32)
    bb_re = fr[..., None] * bre - fi[..., None] * bim
    bb_im = fr[..., None] * bim + fi[..., None] * bre
    u = c_u.reshape(bn, t_len, S5_G, S5_GROUP).astype(f32)
    bu_re = jnp.einsum('gpi,btgi->btgp', bb_re, u)
    bu_im = jnp.einsum('gpi,btgi->btgp', bb_im, u)
    h0r = s5_re.astype(f32)
    h0i = s5_im.astype(f32)
    bu_re = bu_re.at[:, 0].add(ab_re * h0r - ab_im * h0i)
    bu_im = bu_im.at[:, 0].add(ab_re * h0i + ab_im * h0r)
    ar = jnp.broadcast_to(ab_re, bu_re.shape)
    ai = jnp.broadcast_to(ab_im, bu_im.shape)
    _, _, hr, hi = lax.associative_scan(_complex_combine, (ar, ai, bu_re, bu_im), axis=1)
    s5_re_new = hr[:, -1]
    s5_im_new = hi[:, -1]
    y = (jnp.einsum('gip,btgp->btgi', s5_c_re.astype(f32), hr)
         - jnp.einsum('gip,btgp->btgi', s5_c_im.astype(f32), hi)
         + s5_d.astype(f32) * u).reshape(bn, t_len, W_C).astype(x.dtype)
    y = jax.nn.gelu(y)
    y = y * jax.nn.sigmoid(y @ w_glu + b_glu)
    out_c = y * jax.nn.silu(c_g)

    mixed = jnp.concatenate([out_a, out_b, out_c], axis=-1) @ w_out
    x = x + gate[:, None] * mixed
    return (x, conv_a_new, conv_b_new, lru_h_new.astype(x.dtype),
            s5_re_new.astype(x.dtype), s5_im_new.astype(x.dtype))


def setup_inputs(seed: int = 0) -> dict:
    key = jax.random.key(seed)
    ks = iter(jax.random.split(key, 40))
    f32 = jnp.float32
    nrm = lambda shape, s: jax.random.normal(next(ks), shape, f32) * s
    u_lru = jax.random.uniform(next(ks), (DEPTH, W_B), f32, 0.9, 0.999)
    log_dt = jax.random.uniform(next(ks), (DEPTH, S5_G), f32,
                                math.log(0.001), math.log(0.1))
    lam_im = jnp.broadcast_to(jnp.pi * jnp.arange(S5_P, dtype=f32), (DEPTH, S5_G, S5_P))
    return {
        "x_prompt": nrm((BATCH, SEQ, D_MODEL), 1.0),
        "x_sample": nrm((DEC_BATCH, DEC_SEQ, D_MODEL), 1.0),
        "c_prompt": nrm((BATCH, D_MODEL), 1.0),
        "c_sample": nrm((DEC_BATCH, D_MODEL), 1.0),
        "state_conv_a": nrm((DEPTH, DEC_BATCH, CONV_A - 1, W_A), 1.0),
        "state_conv_b": nrm((DEPTH, DEC_BATCH, CONV_B - 1, W_B), 1.0),
        "state_lru_h": nrm((DEPTH, DEC_BATCH, W_B), 0.5),
        "state_s5_re": nrm((DEPTH, DEC_BATCH, S5_G, S5_P), 0.5),
        "state_s5_im": nrm((DEPTH, DEC_BATCH, S5_G, S5_P), 0.5),
        "g_norm": 1.0 + nrm((DEPTH, D_MODEL), 0.02),
        "w_ada": nrm((DEPTH, D_MODEL, 3 * D_MODEL), 0.5 * D_MODEL ** -0.5),
        "b_ada": nrm((DEPTH, 3 * D_MODEL), 0.02),
        "w_in": nrm((DEPTH, D_MODEL, IN_COLS), D_MODEL ** -0.5),
        "w_conv_a": nrm((DEPTH, CONV_A, W_A), CONV_A ** -0.5),
        "b_conv_a": nrm((DEPTH, W_A), 0.02),
        "w_conv_b": nrm((DEPTH, CONV_B, W_B), CONV_B ** -0.5),
        "b_conv_b": nrm((DEPTH, W_B), 0.02),
        "w_rg": nrm((DEPTH, LRU_HEADS, LRU_HD, LRU_HD), LRU_HD ** -0.5),
        "b_rg": nrm((DEPTH, LRU_HEADS, LRU_HD), 0.02),
        "w_ig": nrm((DEPTH, LRU_HEADS, LRU_HD, LRU_HD), LRU_HD ** -0.5),
        "b_ig": nrm((DEPTH, LRU_HEADS, LRU_HD), 0.02),
        "lru_lambda": jnp.log(u_lru) - jnp.log1p(-u_lru),
        "s5_lambda_re": -0.5 + nrm((DEPTH, S5_G, S5_P), 0.01),
        "s5_lambda_im": lam_im + nrm((DEPTH, S5_G, S5_P), 0.01),
        "s5_log_dt": log_dt,
        "s5_b_re": nrm((DEPTH, S5_G, S5_P, S5_GROUP), (2.0 * S5_GROUP) ** -0.5),
        "s5_b_im": nrm((DEPTH, S5_G, S5_P, S5_GROUP), (2.0 * S5_GROUP) ** -0.5),
        "s5_c_re": nrm((DEPTH, S5_G, S5_GROUP, S5_P), (2.0 * S5_P) ** -0.5),
        "s5_c_im": nrm((DEPTH, S5_G, S5_GROUP, S5_P), (2.0 * S5_P) ** -0.5),
        "s5_d": nrm((DEPTH, S5_G, S5_GROUP), 1.0),
        "w_glu": nrm((DEPTH, W_C, W_C), W_C ** -0.5),
        "b_glu": nrm((DEPTH, W_C), 0.02),
        "w_out": nrm((DEPTH, MIX, D_MODEL), MIX ** -0.5),
        "g_final": 1.0 + nrm((D_MODEL,), 0.02),
    }


def reference(x_prompt, x_sample, c_prompt, c_sample,
              state_conv_a, state_conv_b, state_lru_h, state_s5_re, state_s5_im,
              g_norm, w_ada, b_ada, w_in, w_conv_a, b_conv_a, w_conv_b, b_conv_b,
              w_rg, b_rg, w_ig, b_ig, lru_lambda, s5_lambda_re, s5_lambda_im, s5_log_dt,
              s5_b_re, s5_b_im, s5_c_re, s5_c_im, s5_d, w_glu, b_glu, w_out, g_final):
    dt_p = x_prompt.dtype
    bp = x_prompt.shape[0]
    xp = x_prompt
    xs = x_sample
    p_ca, p_cb, p_h, p_re, p_im = [], [], [], [], []
    s_ca, s_cb, s_h, s_re, s_im = [], [], [], [], []
    for l in range(DEPTH):
        w_l = (g_norm[l], w_ada[l], b_ada[l], w_in[l], w_conv_a[l], b_conv_a[l],
               w_conv_b[l], b_conv_b[l], w_rg[l], b_rg[l], w_ig[l], b_ig[l],
               lru_lambda[l], s5_lambda_re[l], s5_lambda_im[l], s5_log_dt[l],
               s5_b_re[l], s5_b_im[l], s5_c_re[l], s5_c_im[l], s5_d[l],
               w_glu[l], b_glu[l], w_out[l])
        zp = (jnp.zeros((bp, CONV_A - 1, W_A), dt_p), jnp.zeros((bp, CONV_B - 1, W_B), dt_p),
              jnp.zeros((bp, W_B), dt_p), jnp.zeros((bp, S5_G, S5_P), dt_p),
              jnp.zeros((bp, S5_G, S5_P), dt_p))
        xp, ca, cb, hh, sr, si = _layer(xp, c_prompt, *zp, *w_l)
        p_ca.append(ca); p_cb.append(cb); p_h.append(hh); p_re.append(sr); p_im.append(si)
        xs, ca, cb, hh, sr, si = _layer(xs, c_sample, state_conv_a[l], state_conv_b[l],
                                        state_lru_h[l], state_s5_re[l], state_s5_im[l], *w_l)
        s_ca.append(ca); s_cb.append(cb); s_h.append(hh); s_re.append(sr); s_im.append(si)
    y_prompt = _rmsnorm(xp, g_final)
    y_sample = _rmsnorm(xs, g_final)
    return (y_prompt, y_sample,
            jnp.stack(p_ca), jnp.stack(p_cb), jnp.stack(p_h), jnp.stack(p_re), jnp.stack(p_im),
            jnp.stack(s_ca), jnp.stack(s_cb), jnp.stack(s_h), jnp.stack(s_re), jnp.stack(s_im))
```

```python
import functools

import jax
import jax.numpy as jnp
from jax import lax
from jax.experimental import pallas as pl
from jax.experimental.pallas import tpu as pltpu

F32 = jnp.float32
BF16 = jnp.bfloat16

D_MODEL = 4096
W_A = D_MODEL // 4
W_B = D_MODEL // 2
W_C = D_MODEL // 4
IN_COLS = 4 * W_A + 2 * W_B + 2 * W_C
CONV_A = 3
CONV_B = 4
LRU_HEADS = 16
LRU_HD = W_B // LRU_HEADS
LRU_C = 8.0
S5_GROUP = 16
S5_G = W_C // S5_GROUP
S5_P = 64
S5_STATES = S5_G * S5_P
EPS = 1e-6

SUBLANES = 8
LANES = 128
MXU_DIM = 256
VMEM_LIMIT_CAP = 60000 * 1024
CH_TILE = MXU_DIM
S5_KT = W_C // CH_TILE
S5_Q = (CH_TILE // S5_GROUP) * S5_P // LANES
HALO = SUBLANES


def _params(sem, vmem_bytes):
    return pltpu.CompilerParams(
        dimension_semantics=sem,
        vmem_limit_bytes=int(min(VMEM_LIMIT_CAP, vmem_bytes)))


def _nbytes(shape, dtype):
    n = 1
    for s in shape:
        n *= s
    return n * jnp.dtype(dtype).itemsize


def _mod_kernel(c_ref, w_ref, b_ref, o_ref):
    c = c_ref[...].astype(BF16)
    w = w_ref[0].astype(BF16)
    o_ref[0] = jnp.dot(c, w, preferred_element_type=F32) + b_ref[0]


def _modulation(c_all, w_ada, b_ada):
    depth, d, n = w_ada.shape
    rows = c_all.shape[0]
    tn = 512
    vmem = 2 * (_nbytes((rows, d), F32) + _nbytes((d, tn), F32)
                + _nbytes((rows, tn), F32)) + _nbytes((d, tn), F32)
    return pl.pallas_call(
        _mod_kernel,
        grid=(depth, n // tn),
        in_specs=[
            pl.BlockSpec((rows, d), lambda l, j: (0, 0)),
            pl.BlockSpec((1, d, tn), lambda l, j: (l, 0, j)),
            pl.BlockSpec((1, 1, tn), lambda l, j: (l, 0, j)),
        ],
        out_specs=pl.BlockSpec((1, rows, tn), lambda l, j: (l, 0, j)),
        out_shape=jax.ShapeDtypeStruct((depth, rows, n), F32),
        compiler_params=_params(("arbitrary", "arbitrary"), vmem + (8 << 20)),
        name="adaln_modulation",
    )(c_all, w_ada, b_ada.reshape(depth, 1, n))


def _norm_mod_kernel(x_ref, scale_ref, shift_ref, g_ref, o_ref):
    x = x_ref[...]
    ms = jnp.mean(x * x, axis=-1, keepdims=True)
    y = (x * lax.rsqrt(ms + EPS)) * g_ref[...]
    h = y * (1.0 + scale_ref[...]) + shift_ref[...]
    o_ref[...] = h.astype(o_ref.dtype)


def _norm_modulate(x, scale, shift, g, bb, tt):
    b, t, d = x.shape
    vmem = 2 * (_nbytes((bb, tt, d), F32) + _nbytes((bb, tt, d), BF16)) \
        + 4 * _nbytes((bb, tt, d), F32)
    return pl.pallas_call(
        _norm_mod_kernel,
        grid=(b // bb, t // tt),
        in_specs=[
            pl.BlockSpec((bb, tt, d), lambda i, j: (i, j, 0)),
            pl.BlockSpec((bb, 1, d), lambda i, j: (i, 0, 0)),
            pl.BlockSpec((bb, 1, d), lambda i, j: (i, 0, 0)),
            pl.BlockSpec((1, 1, d), lambda i, j: (0, 0, 0)),
        ],
        out_specs=pl.BlockSpec((bb, tt, d), lambda i, j: (i, j, 0)),
        out_shape=jax.ShapeDtypeStruct((b, t, d), BF16),
        compiler_params=_params(("arbitrary", "arbitrary"), vmem),
        name="norm_modulate",
    )(x, scale, shift, g.reshape(1, 1, d))


def _final_norm_kernel(x_ref, g_ref, o_ref):
    x = x_ref[...]
    ms = jnp.mean(x * x, axis=-1, keepdims=True)
    o_ref[...] = (x * lax.rsqrt(ms + EPS)) * g_ref[...]


def _final_norm(x, g, bb, tt):
    b, t, d = x.shape
    vmem = 8 * _nbytes((bb, tt, d), F32)
    return pl.pallas_call(
        _final_norm_kernel,
        grid=(b // bb, t // tt),
        in_specs=[
            pl.BlockSpec((bb, tt, d), lambda i, j: (i, j, 0)),
            pl.BlockSpec((1, 1, d), lambda i, j: (0, 0, 0)),
        ],
        out_specs=pl.BlockSpec((bb, tt, d), lambda i, j: (i, j, 0)),
        out_shape=jax.ShapeDtypeStruct((b, t, d), F32),
        compiler_params=_params(("arbitrary", "arbitrary"), vmem),
        name="final_norm",
    )(x, g.reshape(1, 1, d))


def _matmul_kernel(a_ref, b_ref, o_ref):
    o_ref[...] = jnp.dot(a_ref[...], b_ref[...], preferred_element_type=F32)


def _in_projection(h2d, w_bf16):
    m, k = h2d.shape
    n = w_bf16.shape[1]
    tm, tn = 1024, 1024
    vmem = 2 * (_nbytes((tm, k), BF16) + _nbytes((k, tn), BF16)
                + _nbytes((tm, tn), F32)) + _nbytes((tm, tn), F32)
    return pl.pallas_call(
        _matmul_kernel,
        grid=(n // tn, m // tm),
        in_specs=[
            pl.BlockSpec((tm, k), lambda j, i: (i, 0)),
            pl.BlockSpec((k, tn), lambda j, i: (0, j)),
        ],
        out_specs=pl.BlockSpec((tm, tn), lambda j, i: (i, j)),
        out_shape=jax.ShapeDtypeStruct((m, n), F32),
        compiler_params=_params(("arbitrary", "arbitrary"), vmem + (4 << 20)),
        name="in_projection",
    )(h2d, w_bf16)


def _causal_taps(scr, v, w_ref, taps, tt):
    scr[:, HALO:HALO + tt, :] = v
    acc = None
    for k in range(taps):
        src = v if k == taps - 1 else scr[:, HALO - (taps - 1) + k:HALO - (taps - 1) + k + tt, :]
        term = w_ref[k:k + 1, :] * src
        acc = term if acc is None else acc + term
    return acc


def _block_scan_real(a, b):
    row = lax.broadcasted_iota(jnp.int32, (1, SUBLANES, a.shape[-1]), 1)
    for d in (1, 2, 4):
        keep = row >= d
        a_sh = pltpu.roll(a, d, 1)
        b_sh = pltpu.roll(b, d, 1)
        b = b + a * jnp.where(keep, b_sh, 0.0)
        a = a * jnp.where(keep, a_sh, 1.0)
    return a, b


def _mixer_a_kernel(ab_ref, ac_ref, ax_ref, ag_ref, w_ref, b_ref, st_ref,
                    o_ref, st_out_ref, scr, *, tt):
    keep = CONV_A - 1

    @pl.when(pl.program_id(2) == 0)
    def _():
        scr[:, HALO - keep:HALO, :] = st_ref[...]

    conv_in = ac_ref[...] * ax_ref[...]
    y = b_ref[...] + _causal_taps(scr, conv_in, w_ref, CONV_A, tt)
    out = (ab_ref[...] * y) * jax.nn.silu(ag_ref[...])
    o_ref[...] = out.astype(o_ref.dtype)
    tail = scr[:, HALO + tt - keep:HALO + tt, :]
    st_out_ref[...] = tail
    scr[:, HALO - keep:HALO, :] = tail


def _mixer_a(proj3, w_conv, b_conv, state, bb, tt):
    b, t, _ = proj3.shape
    cw = CH_TILE
    nc = W_A // cw
    blk = (bb, tt, cw)

    def col(off):
        return pl.BlockSpec(blk, lambda i, c, j, off=off: (i, j, off // cw + c))

    vmem = 2 * (4 * _nbytes(blk, F32) + _nbytes(blk, BF16)) \
        + _nbytes((bb, HALO + tt, cw), F32) + 6 * _nbytes(blk, F32)
    return pl.pallas_call(
        functools.partial(_mixer_a_kernel, tt=tt),
        grid=(b // bb, nc, t // tt),
        in_specs=[
            col(0), col(W_A), col(2 * W_A), col(3 * W_A),
            pl.BlockSpec((CONV_A, cw), lambda i, c, j: (0, c)),
            pl.BlockSpec((1, cw), lambda i, c, j: (0, c)),
            pl.BlockSpec((bb, CONV_A - 1, cw), lambda i, c, j: (i, 0, c)),
        ],
        out_specs=[
            pl.BlockSpec(blk, lambda i, c, j: (i, j, c)),
            pl.BlockSpec((bb, CONV_A - 1, cw), lambda i, c, j: (i, 0, c)),
        ],
        out_shape=[
            jax.ShapeDtypeStruct((b, t, W_A), BF16),
            jax.ShapeDtypeStruct((b, CONV_A - 1, W_A), F32),
        ],
        scratch_shapes=[pltpu.VMEM((bb, HALO + tt, cw), F32)],
        compiler_params=_params(("arbitrary",) * 3, vmem),
        name="mixer_a",
    )(proj3, proj3, proj3, proj3, w_conv, b_conv.reshape(1, W_A), state)


def _mixer_b_kernel(bx_ref, bg_ref, w_ref, b_ref, wg_ref, brg_ref, big_ref, lam_ref,
                    cst_ref, hst_ref, o_ref, cst_out_ref, hst_out_ref,
                    scr, acum, bcum, carry, *, bb, tt):
    keep = CONV_B - 1
    cw = CH_TILE
    rows = bb * tt
    nblk = tt // SUBLANES

    @pl.when(pl.program_id(2) == 0)
    def _():
        scr[:, HALO - keep:HALO, :] = cst_ref[...]
        carry[...] = hst_ref[...]

    bx = bx_ref[...]
    xb = b_ref[...] + _causal_taps(scr, bx, w_ref, CONV_B, tt)
    tail = scr[:, HALO + tt - keep:HALO + tt, :]
    cst_out_ref[...] = tail
    scr[:, HALO - keep:HALO, :] = tail

    gates = jnp.dot(xb.reshape(rows, cw).astype(BF16), wg_ref[0],
                    preferred_element_type=F32)
    r = jax.nn.sigmoid(gates[:, :cw] + brg_ref[...]).reshape(bb, tt, cw)
    ig = jax.nn.sigmoid(gates[:, cw:] + big_ref[...]).reshape(bb, tt, cw)
    log_a = (-LRU_C * r) * jax.nn.softplus(-lam_ref[...])
    a = jnp.exp(log_a)
    beta = jnp.sqrt(-jnp.tanh(log_a) * (a * a + 1.0))
    bt = (beta * ig) * xb

    a_blk, b_blk = _block_scan_real(a.reshape(bb * nblk, SUBLANES, cw),
                                    bt.reshape(bb * nblk, SUBLANES, cw))
    if nblk == 1:
        h = b_blk + a_blk * carry[...]
        last = h[:, SUBLANES - 1:SUBLANES, :]
    else:
        acum[...] = a_blk.reshape(bb, tt, cw)
        bcum[...] = b_blk.reshape(bb, tt, cw)

        def chain(k, c):
            off = pl.multiple_of(k * SUBLANES, SUBLANES)
            hk = bcum[:, pl.ds(off, SUBLANES), :] + acum[:, pl.ds(off, SUBLANES), :] * c
            bcum[:, pl.ds(off, SUBLANES), :] = hk
            return hk[:, SUBLANES - 1:SUBLANES, :]

        last = lax.fori_loop(0, nblk, chain, carry[...])
        h = bcum[...]
    carry[...] = last
    hst_out_ref[...] = last
    o_ref[...] = (h * jax.nn.silu(bg_ref[...])).astype(o_ref.dtype)


def _mixer_b(proj3, w_conv, b_conv, w_gates, b_rg, b_ig, lam, conv_state, h_state, bb, tt):
    b, t, _ = proj3.shape
    cw = CH_TILE
    nc = W_B // cw
    blk = (bb, tt, cw)
    x_off = 4 * W_A
    g_off = 4 * W_A + W_B
    vec = pl.BlockSpec((1, cw), lambda i, c, j: (0, c))
    vmem = 2 * (2 * _nbytes(blk, F32) + _nbytes(blk, BF16)
                + _nbytes((cw, 2 * cw), BF16)) \
        + _nbytes((bb, HALO + tt, cw), F32) + 14 * _nbytes(blk, F32)
    return pl.pallas_call(
        functools.partial(_mixer_b_kernel, bb=bb, tt=tt),
        grid=(b // bb, nc, t // tt),
        in_specs=[
            pl.BlockSpec(blk, lambda i, c, j: (i, j, x_off // cw + c)),
            pl.BlockSpec(blk, lambda i, c, j: (i, j, g_off // cw + c)),
            pl.BlockSpec((CONV_B, cw), lambda i, c, j: (0, c)),
            vec,
            pl.BlockSpec((1, cw, 2 * cw), lambda i, c, j: (c, 0, 0)),
            vec, vec, vec,
            pl.BlockSpec((bb, CONV_B - 1, cw), lambda i, c, j: (i, 0, c)),
            pl.BlockSpec((bb, 1, cw), lambda i, c, j: (i, 0, c)),
        ],
        out_specs=[
            pl.BlockSpec(blk, lambda i, c, j: (i, j, c)),
            pl.BlockSpec((bb, CONV_B - 1, cw), lambda i, c, j: (i, 0, c)),
            pl.BlockSpec((bb, 1, cw), lambda i, c, j: (i, 0, c)),
        ],
        out_shape=[
            jax.ShapeDtypeStruct((b, t, W_B), BF16),
            jax.ShapeDtypeStruct((b, CONV_B - 1, W_B), F32),
            jax.ShapeDtypeStruct((b, 1, W_B), F32),
        ],
        scratch_shapes=[
            pltpu.VMEM((bb, HALO + tt, cw), F32),
            pltpu.VMEM(blk, F32),
            pltpu.VMEM(blk, F32),
            pltpu.VMEM((bb, 1, cw), F32),
        ],
        compiler_params=_params(("arbitrary",) * 3, vmem),
        name="mixer_b",
    )(proj3, proj3, w_conv, b_conv.reshape(1, W_B), w_gates,
      b_rg.reshape(1, W_B), b_ig.reshape(1, W_B), lam.reshape(1, W_B),
      conv_state, h_state)


def _s5_disc_kernel(lre_ref, lim_ref, ldt_ref, btre_ref, btim_ref,
                    abre_ref, abim_ref, bbre_ref, bbim_ref):
    lre = lre_ref[0]
    lim = lim_ref[0]
    dt = jnp.exp(ldt_ref[0])
    mag = jnp.exp(lre * dt)
    ab_re = mag * jnp.cos(lim * dt)
    ab_im = mag * jnp.sin(lim * dt)
    nr = ab_re - 1.0
    den = lre * lre + lim * lim
    fr = (nr * lre + ab_im * lim) / den
    fi = (ab_im * lre - nr * lim) / den
    bre = btre_ref[0]
    bim = btim_ref[0]
    bbre_ref[0] = fr * bre - fi * bim
    bbim_ref[0] = fr * bim + fi * bre
    abre_ref[0] = ab_re
    abim_ref[0] = ab_im


def _s5_discretize(lam_re, lam_im, log_dt, b_re, b_im):
    depth = lam_re.shape[0]
    gp = (depth, S5_G, 1, S5_P)
    ldt = jnp.broadcast_to(log_dt[:, :, None, None], gp)
    bt_shape = (depth, S5_G, S5_GROUP, S5_P)
    small = pl.BlockSpec((1, S5_G, 1, S5_P), lambda l: (l, 0, 0, 0))
    big = pl.BlockSpec((1, S5_G, S5_GROUP, S5_P), lambda l: (l, 0, 0, 0))
    ab_re, ab_im, bb_re, bb_im = pl.pallas_call(
        _s5_disc_kernel,
        grid=(depth,),
        in_specs=[small, small, small, big, big],
        out_specs=[small, small, big, big],
        out_shape=[jax.ShapeDtypeStruct(gp, F32), jax.ShapeDtypeStruct(gp, F32),
                   jax.ShapeDtypeStruct(bt_shape, F32), jax.ShapeDtypeStruct(bt_shape, F32)],
        name="s5_discretize",
    )(lam_re.reshape(gp), lam_im.reshape(gp), ldt,
      jnp.swapaxes(b_re, 2, 3), jnp.swapaxes(b_im, 2, 3))
    return (ab_re.reshape(depth, 1, S5_STATES), ab_im.reshape(depth, 1, S5_STATES),
            bb_re, bb_im)


def _s5_pow_kernel(ar_ref, ai_ref, l1r_ref, l1i_ref, l2r_ref, l2i_ref,
                   l4r_ref, l4i_ref, pr_ref, pi_ref):
    def cmul(x, y):
        return x[0] * y[0] - x[1] * y[1], x[0] * y[1] + x[1] * y[0]

    a1 = (ar_ref[0], ai_ref[0])
    a2 = cmul(a1, a1)
    a3 = cmul(a2, a1)
    a4 = cmul(a2, a2)
    a5 = cmul(a4, a1)
    a6 = cmul(a4, a2)
    a7 = cmul(a4, a3)
    a8 = cmul(a4, a4)
    shape = (SUBLANES, S5_STATES)
    row = lax.broadcasted_iota(jnp.int32, shape, 0)
    zero = jnp.zeros(shape, F32)
    for part, p_ref, l1_ref, l2_ref, l4_ref in ((0, pr_ref, l1r_ref, l2r_ref, l4r_ref),
                                                 (1, pi_ref, l1i_ref, l2i_ref, l4i_ref)):
        p = zero
        for r, v in enumerate((a1, a2, a3, a4, a5, a6, a7, a8)):
            p = jnp.where(row == r, v[part], p)
        p_ref[0] = p
        l1_ref[0] = jnp.where(row >= 1, a1[part], zero)
        l2_ref[0] = jnp.where(row >= 2, a2[part], zero)
        l4_ref[0] = jnp.where(row >= 4, a4[part], zero)


def _s5_power_tables(ab_re, ab_im):
    depth = ab_re.shape[0]
    vec = pl.BlockSpec((1, 1, S5_STATES), lambda l: (l, 0, 0))
    tab = pl.BlockSpec((1, SUBLANES, S5_STATES), lambda l: (l, 0, 0))
    shp = jax.ShapeDtypeStruct((depth, SUBLANES, S5_STATES), F32)
    return pl.pallas_call(
        _s5_pow_kernel,
        grid=(depth,),
        in_specs=[vec, vec],
        out_specs=[tab] * 8,
        out_shape=[shp] * 8,
        name="s5_power_tables",
    )(ab_re, ab_im)


def _s5_block_matrices(bb_re, bb_im, c_re, c_im):
    gpt = CH_TILE // S5_GROUP
    gps = LANES // S5_P
    eye_q = jnp.eye(S5_Q, dtype=bool)
    eye_g = jnp.eye(gps, dtype=bool)
    mask = eye_q[:, :, None, None] & eye_g[None, None, :, :]

    tb = jnp.stack([bb_re, bb_im], axis=0)
    tb = tb.reshape(2, S5_KT, S5_Q, gps, S5_GROUP, S5_P)
    tb = jnp.transpose(tb, (1, 2, 4, 0, 3, 5))
    bm = jnp.where(mask[None, :, :, :, None, None, :, None],
                   tb[:, :, None, None, :, :, :, :], 0.0)
    bm = bm.reshape(S5_KT, S5_Q, gpt * S5_GROUP, 2 * LANES).astype(BF16)

    tc = jnp.stack([c_re, -c_im], axis=0)
    tc = tc.reshape(2, S5_KT, S5_Q, gps, S5_GROUP, S5_P)
    tc = jnp.transpose(tc, (1, 2, 0, 3, 5, 4))
    mask_c = jnp.transpose(mask, (0, 3, 1, 2))
    cm = jnp.where(mask_c[None, :, None, :, None, :, :, None],
                   tc[:, :, :, :, :, None, None, :], 0.0)
    cm = cm.reshape(S5_KT, S5_Q, 2 * LANES, gpt * S5_GROUP).astype(BF16)
    return bm, cm


def _mixer_c_kernel(cu_ref, cg_ref, bm_ref, cm_ref,
                    l1r_ref, l1i_ref, l2r_ref, l2i_ref, l4r_ref, l4i_ref, pr_ref, pi_ref,
                    d_ref, wglu_ref, bglu_ref, sre_ref, sim_ref,
                    o_ref, sre_out_ref, sim_out_ref,
                    ystore, hsr, hsi, car_r, car_i, *, bb, tt):
    kt = pl.program_id(2)
    cw = CH_TILE
    rows = bb * tt
    nblk = tt // SUBLANES
    n = bb * nblk

    @pl.when(pl.program_id(1) == 0)
    def _():
        car_r[kt] = sre_ref[...]
        car_i[kt] = sim_ref[...]

    u = cu_ref[...].reshape(rows, cw)
    ub = u.astype(BF16)
    y = d_ref[...] * u
    cr_all = car_r[kt]
    ci_all = car_i[kt]
    new_r, new_i = [], []
    for q in range(S5_Q):
        sl = slice(LANES * q, LANES * (q + 1))
        bu = jnp.dot(ub, bm_ref[0, q], preferred_element_type=F32)
        br = bu[:, :LANES].reshape(n, SUBLANES, LANES)
        bi = bu[:, LANES:].reshape(n, SUBLANES, LANES)
        for d, lr_ref, li_ref in ((1, l1r_ref, l1i_ref), (2, l2r_ref, l2i_ref),
                                  (4, l4r_ref, l4i_ref)):
            lr = lr_ref[0, :, sl]
            li = li_ref[0, :, sl]
            sr = pltpu.roll(br, d, 1)
            si = pltpu.roll(bi, d, 1)
            br, bi = br + (lr * sr - li * si), bi + (lr * si + li * sr)
        p_r = pr_ref[0, :, sl]
        p_i = pi_ref[0, :, sl]
        cr = cr_all[:, :, sl]
        ci = ci_all[:, :, sl]
        if nblk == 1:
            hr = br + (p_r * cr - p_i * ci)
            hi = bi + (p_r * ci + p_i * cr)
            ncr = hr[:, SUBLANES - 1:SUBLANES, :]
            nci = hi[:, SUBLANES - 1:SUBLANES, :]
        else:
            hsr[...] = br.reshape(bb, tt, LANES)
            hsi[...] = bi.reshape(bb, tt, LANES)

            def chain(k, c, p_r=p_r, p_i=p_i):
                c_r, c_i = c
                off = pl.multiple_of(k * SUBLANES, SUBLANES)
                hkr = hsr[:, pl.ds(off, SUBLANES), :] + (p_r * c_r - p_i * c_i)
                hki = hsi[:, pl.ds(off, SUBLANES), :] + (p_r * c_i + p_i * c_r)
                hsr[:, pl.ds(off, SUBLANES), :] = hkr
                hsi[:, pl.ds(off, SUBLANES), :] = hki
                return (hkr[:, SUBLANES - 1:SUBLANES, :], hki[:, SUBLANES - 1:SUBLANES, :])

            ncr, nci = lax.fori_loop(0, nblk, chain, (cr, ci))
            hr = hsr[...]
            hi = hsi[...]
        new_r.append(ncr)
        new_i.append(nci)
        hcat = jnp.concatenate([hr.reshape(rows, LANES), hi.reshape(rows, LANES)], axis=-1)
        y = y + jnp.dot(hcat.astype(BF16), cm_ref[0, q], preferred_element_type=F32)

    ncr_all = jnp.concatenate(new_r, axis=-1)
    nci_all = jnp.concatenate(new_i, axis=-1)
    car_r[kt] = ncr_all
    car_i[kt] = nci_all
    sre_out_ref[:, kt] = ncr_all
    sim_out_ref[:, kt] = nci_all
    ystore[kt] = y

    @pl.when(kt == S5_KT - 1)
    def _():
        y_all = jnp.concatenate([ystore[k] for k in range(S5_KT)], axis=-1)
        yg = jax.nn.gelu(y_all)
        z = jnp.dot(yg.astype(BF16), wglu_ref[...], preferred_element_type=F32) + bglu_ref[...]
        yy = yg * jax.nn.sigmoid(z)
        out = yy.reshape(bb, tt, W_C) * jax.nn.silu(cg_ref[...])
        o_ref[...] = out.astype(o_ref.dtype)


def _mixer_c(proj3, bmat, cmat, tables, d_skip, w_glu, b_glu, s_re, s_im, bb, tt):
    b, t, _ = proj3.shape
    cw = CH_TILE
    u_off = 4 * W_A + 2 * W_B
    g_off = u_off + W_C
    kslab = S5_Q * LANES
    tab = pl.BlockSpec((1, SUBLANES, kslab), lambda i, j, k: (0, 0, k))
    mat = pl.BlockSpec((1, S5_Q, cw, cw), lambda i, j, k: (k, 0, 0, 0))
    st = pl.BlockSpec((bb, 1, kslab), lambda i, j, k: (i, 0, k))
    st_out = pl.BlockSpec((bb, S5_KT, 1, kslab), lambda i, j, k: (i, 0, 0, 0))
    rows = bb * tt
    vmem = 2 * (_nbytes((bb, tt, cw), F32) + _nbytes((bb, tt, W_C), F32)
                + 2 * _nbytes((S5_Q, cw, cw), BF16) + 8 * _nbytes((SUBLANES, kslab), F32)
                + _nbytes((W_C, W_C), BF16) + _nbytes((bb, tt, W_C), BF16)
                + 4 * _nbytes((bb, SUBLANES, kslab), F32)) \
        + _nbytes((S5_KT, rows, cw), F32) + 2 * _nbytes((bb, tt, LANES), F32) \
        + 2 * _nbytes((S5_KT, bb, SUBLANES, kslab), F32) + 10 * _nbytes((rows, W_C), F32)
    return pl.pallas_call(
        functools.partial(_mixer_c_kernel, bb=bb, tt=tt),
        grid=(b // bb, t // tt, S5_KT),
        in_specs=[
            pl.BlockSpec((bb, tt, cw), lambda i, j, k: (i, j, u_off // cw + k)),
            pl.BlockSpec((bb, tt, W_C), lambda i, j, k: (i, j, g_off // W_C)),
            mat, mat,
            tab, tab, tab, tab, tab, tab, tab, tab,
            pl.BlockSpec((1, cw), lambda i, j, k: (0, k)),
            pl.BlockSpec((W_C, W_C), lambda i, j, k: (0, 0)),
            pl.BlockSpec((1, W_C), lambda i, j, k: (0, 0)),
            st, st,
        ],
        out_specs=[
            pl.BlockSpec((bb, tt, W_C), lambda i, j, k: (i, j, 0)),
            st_out, st_out,
        ],
        out_shape=[
            jax.ShapeDtypeStruct((b, t, W_C), BF16),
            jax.ShapeDtypeStruct((b, S5_KT, 1, kslab), F32),
            jax.ShapeDtypeStruct((b, S5_KT, 1, kslab), F32),
        ],
        scratch_shapes=[
            pltpu.VMEM((S5_KT, rows, cw), F32),
            pltpu.VMEM((bb, tt, LANES), F32),
            pltpu.VMEM((bb, tt, LANES), F32),
            pltpu.VMEM((S5_KT, bb, 1, kslab), F32),
            pltpu.VMEM((S5_KT, bb, 1, kslab), F32),
        ],
        compiler_params=_params(("arbitrary",) * 3, vmem),
        name="mixer_c",
    )(proj3, proj3, bmat, cmat, *tables, d_skip.reshape(1, W_C), w_glu,
      b_glu.reshape(1, W_C), s_re, s_im)


def _out_proj_kernel(ma_ref, mb_ref, mc_ref, w_ref, x_ref, gate_ref, o_ref, *, bb, tt):
    acc = jnp.dot(ma_ref[...], w_ref[0:W_A, :], preferred_element_type=F32)
    acc = acc + jnp.dot(mb_ref[...], w_ref[W_A:W_A + W_B, :], preferred_element_type=F32)
    acc = acc + jnp.dot(mc_ref[...], w_ref[W_A + W_B:, :], preferred_element_type=F32)
    o_ref[...] = x_ref[...] + gate_ref[...] * acc.reshape(bb, tt, acc.shape[-1])


def _out_projection(out_a, out_b, out_c, w_bf16, x, gate, bb, tt):
    b, t, d = x.shape
    tm = bb * tt
    tn = 512
    nt = t // tt
    m = b * t

    def rows(width):
        return pl.BlockSpec((tm, width), lambda i, j: (i, 0))

    vmem = 2 * (_nbytes((tm, d), BF16) + _nbytes((d, tn), BF16)
                + 2 * _nbytes((tm, tn), F32)) + 3 * _nbytes((tm, tn), F32)
    return pl.pallas_call(
        functools.partial(_out_proj_kernel, bb=bb, tt=tt),
        grid=(m // tm, d // tn),
        in_specs=[
            rows(W_A), rows(W_B), rows(W_C),
            pl.BlockSpec((d, tn), lambda i, j: (0, j)),
            pl.BlockSpec((bb, tt, tn), lambda i, j: (i // nt, i % nt, j)),
            pl.BlockSpec((bb, 1, tn), lambda i, j: (i // nt, 0, j)),
        ],
        out_specs=pl.BlockSpec((bb, tt, tn), lambda i, j: (i // nt, i % nt, j)),
        out_shape=jax.ShapeDtypeStruct((b, t, d), F32),
        compiler_params=_params(("arbitrary", "arbitrary"), vmem + (4 << 20)),
        name="out_projection",
    )(out_a.reshape(m, W_A), out_b.reshape(m, W_B), out_c.reshape(m, W_C), w_bf16, x, gate)


def _gate_weights(w_rg, w_ig):
    hpt = CH_TILE // LRU_HD
    pairs = LRU_HEADS // hpt
    w = jnp.stack([w_rg, w_ig], axis=2)
    w = w.reshape(pairs, hpt, LRU_HD, 2, LRU_HD)
    eye = jnp.eye(hpt, dtype=bool)
    out = jnp.where(eye[None, :, None, None, :, None],
                    w[:, :, :, :, None, :], 0.0)
    return out.reshape(pairs, CH_TILE, 2 * CH_TILE).astype(BF16)


def _layer(x, mod, states, wts, tiles):
    b, t, d = x.shape
    conv_a, conv_b, lru_h, s5_re, s5_im = states
    shift, scale, gate = (mod[:, None, k * d:(k + 1) * d] for k in range(3))

    h = _norm_modulate(x, scale, shift, wts["g_norm"], *tiles["norm"])
    proj = _in_projection(h.reshape(b * t, d), wts["w_in"]).reshape(b, t, IN_COLS)

    out_a, conv_a_new = _mixer_a(proj, wts["w_conv_a"], wts["b_conv_a"], conv_a, *tiles["a"])
    out_b, conv_b_new, lru_new = _mixer_b(
        proj, wts["w_conv_b"], wts["b_conv_b"], wts["w_gates"], wts["b_rg"], wts["b_ig"],
        wts["lru_lambda"], conv_b, lru_h.reshape(b, 1, W_B), *tiles["b"])
    out_c, re_new, im_new = _mixer_c(
        proj, wts["bmat"], wts["cmat"], wts["tables"], wts["s5_d"], wts["w_glu"],
        wts["b_glu"], s5_re.reshape(b, 1, S5_STATES), s5_im.reshape(b, 1, S5_STATES),
        *tiles["c"])

    x_new = _out_projection(out_a, out_b, out_c, wts["w_out"], x, gate, *tiles["out"])
    return (x_new, conv_a_new, conv_b_new, lru_new.reshape(b, W_B),
            re_new.reshape(b, S5_G, S5_P), im_new.reshape(b, S5_G, S5_P))


PROMPT_TILES = {"norm": (1, 256), "a": (1, 1024), "b": (1, 512), "c": (1, 256),
                "out": (1, 1024), "final": (1, 256)}
SAMPLE_TILES = {"norm": (32, 8), "a": (128, 8), "b": (64, 8), "c": (32, 8),
                "out": (128, 8), "final": (32, 8)}


def kernel(x_prompt, x_sample, c_prompt, c_sample, state_conv_a, state_conv_b, state_lru_h,
           state_s5_re, state_s5_im, g_norm, w_ada, b_ada, w_in, w_conv_a, b_conv_a,
           w_conv_b, b_conv_b, w_rg, b_rg, w_ig, b_ig, lru_lambda, s5_lambda_re,
           s5_lambda_im, s5_log_dt, s5_b_re, s5_b_im, s5_c_re, s5_c_im, s5_d, w_glu,
           b_glu, w_out, g_final):
    depth = w_in.shape[0]
    bp = x_prompt.shape[0]
    bs = x_sample.shape[0]

    c_all = jnp.concatenate([c_prompt, c_sample], axis=0)
    pad = (-c_all.shape[0]) % SUBLANES
    c_all = jnp.pad(c_all, ((0, pad), (0, 0)))
    mod = _modulation(c_all, w_ada, b_ada)

    ab_re, ab_im, bb_re, bb_im = _s5_discretize(s5_lambda_re, s5_lambda_im, s5_log_dt,
                                               s5_b_re, s5_b_im)
    tables = _s5_power_tables(ab_re, ab_im)

    xp, xs = x_prompt, x_sample
    zeros_p = (jnp.zeros((bp, CONV_A - 1, W_A), F32), jnp.zeros((bp, CONV_B - 1, W_B), F32),
               jnp.zeros((bp, W_B), F32), jnp.zeros((bp, S5_G, S5_P), F32),
               jnp.zeros((bp, S5_G, S5_P), F32))
    outs_p, outs_s = [], []
    for l in range(depth):
        bmat, cmat = _s5_block_matrices(bb_re[l], bb_im[l], s5_c_re[l], s5_c_im[l])
        wts = {
            "g_norm": g_norm[l], "w_in": w_in[l].astype(BF16),
            "w_conv_a": w_conv_a[l], "b_conv_a": b_conv_a[l],
            "w_conv_b": w_conv_b[l], "b_conv_b": b_conv_b[l],
            "w_gates": _gate_weights(w_rg[l], w_ig[l]), "b_rg": b_rg[l], "b_ig": b_ig[l],
            "lru_lambda": lru_lambda[l], "bmat": bmat, "cmat": cmat,
            "tables": [tb[l:l + 1] for tb in tables], "s5_d": s5_d[l],
            "w_glu": w_glu[l].astype(BF16), "b_glu": b_glu[l], "w_out": w_out[l].astype(BF16),
        }
        res_p = _layer(xp, mod[l, :bp], zeros_p, wts, PROMPT_TILES)
        xp = res_p[0]
        outs_p.append(res_p[1:])
        st_s = (state_conv_a[l], state_conv_b[l], state_lru_h[l], state_s5_re[l],
                state_s5_im[l])
        res_s = _layer(xs, mod[l, bp:bp + bs], st_s, wts, SAMPLE_TILES)
        xs = res_s[0]
        outs_s.append(res_s[1:])

    y_prompt = _final_norm(xp, g_final, *PROMPT_TILES["final"])
    y_sample = _final_norm(xs, g_final, *SAMPLE_TILES["final"])
    stack = lambda outs, k: jnp.stack([o[k] for o in outs])
    return (y_prompt, y_sample,
            *(stack(outs_p, k) for k in range(5)),
            *(stack(outs_s, k) for k in range(5)))
```

```python
import functools

import jax
import jax.numpy as jnp
from jax import lax
from jax.experimental import pallas as pl
from jax.experimental.pallas import tpu as pltpu

F32 = jnp.float32
BF16 = jnp.bfloat16

D_MODEL = 4096
W_A = D_MODEL // 4
W_B = D_MODEL // 2
W_C = D_MODEL // 4
IN_COLS = 4 * W_A + 2 * W_B + 2 * W_C
CONV_A = 3
CONV_B = 4
LRU_HEADS = 16
LRU_HD = W_B // LRU_HEADS
LRU_C = 8.0
S5_GROUP = 16
S5_G = W_C // S5_GROUP
S5_P = 64
S5_STATES = S5_G * S5_P
EPS = 1e-6

SUBLANES = 8
LANES = 128
MXU_DIM = 256
VMEM_LIMIT_CAP = 60000 * 1024
CH_TILE = MXU_DIM
S5_KT = W_C // CH_TILE
S5_Q = (CH_TILE // S5_GROUP) * S5_P // LANES
HALO = SUBLANES
N_TABLES = 16


def _params(sem, vmem_bytes):
    return pltpu.CompilerParams(
        dimension_semantics=sem,
        vmem_limit_bytes=int(min(VMEM_LIMIT_CAP, vmem_bytes)))


def _nbytes(shape, dtype):
    n = 1
    for s in shape:
        n *= s
    return n * jnp.dtype(dtype).itemsize


def _mod_kernel(c_ref, w_ref, b_ref, o_ref):
    c = c_ref[...].astype(BF16)
    w = w_ref[0].astype(BF16)
    o_ref[0] = jnp.dot(c, w, preferred_element_type=F32) + b_ref[0]


def _modulation(c_all, w_ada, b_ada):
    depth, d, n = w_ada.shape
    rows = c_all.shape[0]
    tn = 512
    vmem = 2 * (_nbytes((rows, d), F32) + _nbytes((d, tn), F32)
                + _nbytes((rows, tn), F32)) + _nbytes((d, tn), F32)
    return pl.pallas_call(
        _mod_kernel,
        grid=(depth, n // tn),
        in_specs=[
            pl.BlockSpec((rows, d), lambda l, j: (0, 0)),
            pl.BlockSpec((1, d, tn), lambda l, j: (l, 0, j)),
            pl.BlockSpec((1, 1, tn), lambda l, j: (l, 0, j)),
        ],
        out_specs=pl.BlockSpec((1, rows, tn), lambda l, j: (l, 0, j)),
        out_shape=jax.ShapeDtypeStruct((depth, rows, n), F32),
        compiler_params=_params(("arbitrary", "arbitrary"), vmem + (8 << 20)),
        name="adaln_modulation",
    )(c_all, w_ada, b_ada.reshape(depth, 1, n))


def _norm_mod_kernel(x_ref, scale_ref, shift_ref, g_ref, o_ref):
    x = x_ref[...]
    ms = jnp.mean(x * x, axis=-1, keepdims=True)
    y = (x * lax.rsqrt(ms + EPS)) * g_ref[...]
    h = y * (1.0 + scale_ref[...]) + shift_ref[...]
    o_ref[...] = h.reshape(o_ref.shape).astype(o_ref.dtype)


def _norm_modulate(x, scale, shift, g, bb, tt):
    b, t, d = x.shape
    nt = t // tt
    vmem = 2 * (_nbytes((bb, tt, d), F32) + _nbytes((bb, tt, d), BF16)) \
        + 4 * _nbytes((bb, tt, d), F32)
    return pl.pallas_call(
        _norm_mod_kernel,
        grid=(b // bb, nt),
        in_specs=[
            pl.BlockSpec((bb, tt, d), lambda i, j: (i, j, 0)),
            pl.BlockSpec((bb, 1, d), lambda i, j: (i, 0, 0)),
            pl.BlockSpec((bb, 1, d), lambda i, j: (i, 0, 0)),
            pl.BlockSpec((1, 1, d), lambda i, j: (0, 0, 0)),
        ],
        out_specs=pl.BlockSpec((bb * tt, d), lambda i, j: (i * nt + j, 0)),
        out_shape=jax.ShapeDtypeStruct((b * t, d), BF16),
        compiler_params=_params(("arbitrary", "arbitrary"), vmem),
        name="norm_modulate",
    )(x, scale, shift, g.reshape(1, 1, d))


def _final_norm_kernel(x_ref, g_ref, o_ref):
    x = x_ref[...]
    ms = jnp.mean(x * x, axis=-1, keepdims=True)
    o_ref[...] = (x * lax.rsqrt(ms + EPS)) * g_ref[...]


def _final_norm(x, g, bb, tt):
    b, t, d = x.shape
    vmem = 8 * _nbytes((bb, tt, d), F32)
    return pl.pallas_call(
        _final_norm_kernel,
        grid=(b // bb, t // tt),
        in_specs=[
            pl.BlockSpec((bb, tt, d), lambda i, j: (i, j, 0)),
            pl.BlockSpec((1, 1, d), lambda i, j: (0, 0, 0)),
        ],
        out_specs=pl.BlockSpec((bb, tt, d), lambda i, j: (i, j, 0)),
        out_shape=jax.ShapeDtypeStruct((b, t, d), F32),
        compiler_params=_params(("arbitrary", "arbitrary"), vmem),
        name="final_norm",
    )(x, g.reshape(1, 1, d))


def _in_proj_kernel(a_ref, w_ref, o_ref, wb_ref):
    @pl.when(pl.program_id(1) == 0)
    def _():
        wb_ref[...] = w_ref[0].astype(BF16)

    o_ref[...] = jnp.dot(a_ref[...], wb_ref[...], preferred_element_type=F32)


def _in_projection(h2d, w_in, layer):
    m, k = h2d.shape
    n = w_in.shape[2]
    tm, tn = 1024, 512
    vmem = 2 * (_nbytes((tm, k), BF16) + _nbytes((k, tn), F32)
                + _nbytes((tm, tn), F32)) + _nbytes((k, tn), BF16) + _nbytes((tm, tn), F32)
    return pl.pallas_call(
        _in_proj_kernel,
        grid=(n // tn, m // tm),
        in_specs=[
            pl.BlockSpec((tm, k), lambda j, i: (i, 0)),
            pl.BlockSpec((1, k, tn), lambda j, i: (layer, 0, j)),
        ],
        out_specs=pl.BlockSpec((tm, tn), lambda j, i: (i, j)),
        out_shape=jax.ShapeDtypeStruct((m, n), F32),
        scratch_shapes=[pltpu.VMEM((k, tn), BF16)],
        compiler_params=_params(("arbitrary", "arbitrary"), vmem + (4 << 20)),
        name="in_projection",
    )(h2d, w_in)


def _causal_taps(scr, v, w_ref, taps, tt):
    scr[:, HALO:HALO + tt, :] = v
    acc = None
    for k in range(taps):
        src = v if k == taps - 1 else scr[:, HALO - (taps - 1) + k:HALO - (taps - 1) + k + tt, :]
        term = w_ref[k:k + 1, :] * src
        acc = term if acc is None else acc + term
    return acc


def _block_scan_real(a, b):
    row = lax.broadcasted_iota(jnp.int32, (1, SUBLANES, a.shape[-1]), 1)
    for d in (1, 2, 4):
        keep = row >= d
        a_sh = pltpu.roll(a, d, 1)
        b_sh = pltpu.roll(b, d, 1)
        b = b + a * jnp.where(keep, b_sh, 0.0)
        a = a * jnp.where(keep, a_sh, 1.0)
    return a, b


def _block_scan_cplx(br, bi, tab_ref, base, sl):
    for lvl, d in enumerate((1, 2, 4)):
        lr = tab_ref[0, base + 2 * lvl, :, sl]
        li = tab_ref[0, base + 2 * lvl + 1, :, sl]
        sr = pltpu.roll(br, d, 1)
        si = pltpu.roll(bi, d, 1)
        br, bi = br + (lr * sr - li * si), bi + (lr * si + li * sr)
    return br, bi


def _spread_rows(scr, nblk, bb):
    return jnp.concatenate(
        [jnp.broadcast_to(scr[:, k:k + 1, :], (bb, SUBLANES, LANES)) for k in range(nblk)],
        axis=1)


def _mixer_a_kernel(ab_ref, ac_ref, ax_ref, ag_ref, w_ref, b_ref, st_ref,
                    o_ref, st_out_ref, scr, *, tt):
    keep = CONV_A - 1

    @pl.when(pl.program_id(2) == 0)
    def _():
        scr[:, HALO - keep:HALO, :] = st_ref[...]

    conv_in = ac_ref[...] * ax_ref[...]
    y = b_ref[...] + _causal_taps(scr, conv_in, w_ref, CONV_A, tt)
    out = (ab_ref[...] * y) * jax.nn.silu(ag_ref[...])
    o_ref[...] = out.reshape(o_ref.shape).astype(o_ref.dtype)
    tail = scr[:, HALO + tt - keep:HALO + tt, :]
    st_out_ref[...] = tail
    scr[:, HALO - keep:HALO, :] = tail


def _mixer_a(proj3, w_conv, b_conv, state, bb, tt):
    b, t, _ = proj3.shape
    cw = CH_TILE
    nc = W_A // cw
    nt = t // tt
    blk = (bb, tt, cw)

    def col(off):
        return pl.BlockSpec(blk, lambda i, c, j, off=off: (i, j, off // cw + c))

    vmem = 2 * (4 * _nbytes(blk, F32) + _nbytes(blk, BF16)) \
        + _nbytes((bb, HALO + tt, cw), F32) + 6 * _nbytes(blk, F32)
    return pl.pallas_call(
        functools.partial(_mixer_a_kernel, tt=tt),
        grid=(b // bb, nc, nt),
        in_specs=[
            col(0), col(W_A), col(2 * W_A), col(3 * W_A),
            pl.BlockSpec((CONV_A, cw), lambda i, c, j: (0, c)),
            pl.BlockSpec((1, cw), lambda i, c, j: (0, c)),
            pl.BlockSpec((bb, CONV_A - 1, cw), lambda i, c, j: (i, 0, c)),
        ],
        out_specs=[
            pl.BlockSpec((bb * tt, cw), lambda i, c, j: (i * nt + j, c)),
            pl.BlockSpec((bb, CONV_A - 1, cw), lambda i, c, j: (i, 0, c)),
        ],
        out_shape=[
            jax.ShapeDtypeStruct((b * t, W_A), BF16),
            jax.ShapeDtypeStruct((b, CONV_A - 1, W_A), F32),
        ],
        scratch_shapes=[pltpu.VMEM((bb, HALO + tt, cw), F32)],
        compiler_params=_params(("arbitrary",) * 3, vmem),
        name="mixer_a",
    )(proj3, proj3, proj3, proj3, w_conv, b_conv.reshape(1, W_A), state)


def _mixer_b_kernel(bx_ref, bg_ref, w_ref, b_ref, wg_ref, brg_ref, big_ref, lam_ref,
                    cst_ref, hst_ref, o_ref, cst_out_ref, hst_out_ref,
                    scr, carry, *scan_scr, bb, tt):
    keep = CONV_B - 1
    cw = CH_TILE
    rows = bb * tt
    nblk = tt // SUBLANES
    nv = nblk // SUBLANES

    @pl.when(pl.program_id(2) == 0)
    def _():
        scr[:, HALO - keep:HALO, :] = cst_ref[...]
        carry[...] = hst_ref[...]

    bx = bx_ref[...]
    xb = b_ref[...] + _causal_taps(scr, bx, w_ref, CONV_B, tt)
    tail = scr[:, HALO + tt - keep:HALO + tt, :]
    cst_out_ref[...] = tail
    scr[:, HALO - keep:HALO, :] = tail

    gates = jnp.dot(xb.reshape(rows, cw).astype(BF16), wg_ref[0],
                    preferred_element_type=F32)
    r = jax.nn.sigmoid(gates[:, :cw] + brg_ref[...]).reshape(bb, tt, cw)
    ig = jax.nn.sigmoid(gates[:, cw:] + big_ref[...]).reshape(bb, tt, cw)
    log_a = (-LRU_C * r) * jax.nn.softplus(-lam_ref[...])
    a = jnp.exp(log_a)
    beta = jnp.sqrt(-jnp.tanh(log_a) * (a * a + 1.0))
    bt = (beta * ig) * xb

    cin_all = carry[...]
    h_parts, last_parts = [], []
    for s in range(cw // LANES):
        sl = slice(s * LANES, (s + 1) * LANES)
        cin = cin_all[:, :, sl]
        a_blk, b_blk = _block_scan_real(a[:, :, sl].reshape(bb * nblk, SUBLANES, LANES),
                                        bt[:, :, sl].reshape(bb * nblk, SUBLANES, LANES))
        if nblk == 1:
            h = b_blk + a_blk * cin
            last = h[:, SUBLANES - 1:SUBLANES, :]
        else:
            acum, bcum, cscr = scan_scr
            acum[s] = a_blk.reshape(bb, tt, LANES)
            bcum[s] = b_blk.reshape(bb, tt, LANES)
            a2 = acum[s, :, pl.ds(SUBLANES - 1, nblk, stride=SUBLANES), :]
            b2 = bcum[s, :, pl.ds(SUBLANES - 1, nblk, stride=SUBLANES), :]
            a2, b2 = _block_scan_real(a2.reshape(bb * nv, SUBLANES, LANES),
                                      b2.reshape(bb * nv, SUBLANES, LANES))
            a2 = a2.reshape(bb, nblk, LANES)
            b2 = b2.reshape(bb, nblk, LANES)
            prev = cin
            ends = []
            for j in range(nv):
                seg = slice(j * SUBLANES, (j + 1) * SUBLANES)
                e = b2[:, seg, :] + a2[:, seg, :] * prev
                ends.append(e)
                prev = e[:, SUBLANES - 1:SUBLANES, :]
            last = prev
            cscr[s, :, 0:1, :] = cin
            cscr[s, :, 1:1 + nblk, :] = jnp.concatenate(ends, axis=1)
            spread = _spread_rows(cscr.at[s], nblk, bb)
            h = b_blk + a_blk * spread.reshape(bb * nblk, SUBLANES, LANES)
        h_parts.append(h.reshape(bb, tt, LANES))
        last_parts.append(last)
    h = jnp.concatenate(h_parts, axis=-1)
    last = jnp.concatenate(last_parts, axis=-1)
    carry[...] = last
    hst_out_ref[...] = last
    out = h * jax.nn.silu(bg_ref[...])
    o_ref[...] = out.reshape(o_ref.shape).astype(o_ref.dtype)


def _mixer_b(proj3, w_conv, b_conv, w_gates, b_rg, b_ig, lam, conv_state, h_state, bb, tt):
    b, t, _ = proj3.shape
    cw = CH_TILE
    nc = W_B // cw
    nt = t // tt
    nblk = tt // SUBLANES
    blk = (bb, tt, cw)
    x_off = 4 * W_A
    g_off = 4 * W_A + W_B
    vec = pl.BlockSpec((1, cw), lambda i, c, j: (0, c))
    scratch = [pltpu.VMEM((bb, HALO + tt, cw), F32), pltpu.VMEM((bb, 1, cw), F32)]
    if nblk > 1:
        assert nblk % SUBLANES == 0 and nblk // SUBLANES <= SUBLANES
        slabs = cw // LANES
        scratch += [pltpu.VMEM((slabs, bb, tt, LANES), F32),
                    pltpu.VMEM((slabs, bb, tt, LANES), F32),
                    pltpu.VMEM((slabs, bb, HALO + nblk, LANES), F32)]
    vmem = 2 * (2 * _nbytes(blk, F32) + _nbytes(blk, BF16)
                + _nbytes((cw, 2 * cw), BF16)) \
        + _nbytes((bb, HALO + tt, cw), F32) + 16 * _nbytes(blk, F32)
    return pl.pallas_call(
        functools.partial(_mixer_b_kernel, bb=bb, tt=tt),
        grid=(b // bb, nc, nt),
        in_specs=[
            pl.BlockSpec(blk, lambda i, c, j: (i, j, x_off // cw + c)),
            pl.BlockSpec(blk, lambda i, c, j: (i, j, g_off // cw + c)),
            pl.BlockSpec((CONV_B, cw), lambda i, c, j: (0, c)),
            vec,
            pl.BlockSpec((1, cw, 2 * cw), lambda i, c, j: (c, 0, 0)),
            vec, vec, vec,
            pl.BlockSpec((bb, CONV_B - 1, cw), lambda i, c, j: (i, 0, c)),
            pl.BlockSpec((bb, 1, cw), lambda i, c, j: (i, 0, c)),
        ],
        out_specs=[
            pl.BlockSpec((bb * tt, cw), lambda i, c, j: (i * nt + j, c)),
            pl.BlockSpec((bb, CONV_B - 1, cw), lambda i, c, j: (i, 0, c)),
            pl.BlockSpec((bb, 1, cw), lambda i, c, j: (i, 0, c)),
        ],
        out_shape=[
            jax.ShapeDtypeStruct((b * t, W_B), BF16),
            jax.ShapeDtypeStruct((b, CONV_B - 1, W_B), F32),
            jax.ShapeDtypeStruct((b, 1, W_B), F32),
        ],
        scratch_shapes=scratch,
        compiler_params=_params(("arbitrary",) * 3, vmem),
        name="mixer_b",
    )(proj3, proj3, w_conv, b_conv.reshape(1, W_B), w_gates,
      b_rg.reshape(1, W_B), b_ig.reshape(1, W_B), lam.reshape(1, W_B),
      conv_state, h_state)


def _s5_disc_kernel(lre_ref, lim_ref, ldt_ref, btre_ref, btim_ref,
                    abre_ref, abim_ref, bbre_ref, bbim_ref):
    lre = lre_ref[0]
    lim = lim_ref[0]
    dt = jnp.exp(ldt_ref[0])
    mag = jnp.exp(lre * dt)
    ab_re = mag * jnp.cos(lim * dt)
    ab_im = mag * jnp.sin(lim * dt)
    nr = ab_re - 1.0
    den = lre * lre + lim * lim
    fr = (nr * lre + ab_im * lim) / den
    fi = (ab_im * lre - nr * lim) / den
    bre = btre_ref[0]
    bim = btim_ref[0]
    bbre_ref[0] = fr * bre - fi * bim
    bbim_ref[0] = fr * bim + fi * bre
    abre_ref[0] = ab_re
    abim_ref[0] = ab_im


def _s5_discretize(lam_re, lam_im, log_dt, b_re, b_im):
    depth = lam_re.shape[0]
    gp = (depth, S5_G, 1, S5_P)
    ldt = jnp.broadcast_to(log_dt[:, :, None, None], gp)
    bt_shape = (depth, S5_G, S5_GROUP, S5_P)
    small = pl.BlockSpec((1, S5_G, 1, S5_P), lambda l: (l, 0, 0, 0))
    big = pl.BlockSpec((1, S5_G, S5_GROUP, S5_P), lambda l: (l, 0, 0, 0))
    ab_re, ab_im, bb_re, bb_im = pl.pallas_call(
        _s5_disc_kernel,
        grid=(depth,),
        in_specs=[small, small, small, big, big],
        out_specs=[small, small, big, big],
        out_shape=[jax.ShapeDtypeStruct(gp, F32), jax.ShapeDtypeStruct(gp, F32),
                   jax.ShapeDtypeStruct(bt_shape, F32), jax.ShapeDtypeStruct(bt_shape, F32)],
        name="s5_discretize",
    )(lam_re.reshape(gp), lam_im.reshape(gp), ldt,
      jnp.swapaxes(b_re, 2, 3), jnp.swapaxes(b_im, 2, 3))
    return (ab_re.reshape(depth, 1, S5_STATES), ab_im.reshape(depth, 1, S5_STATES),
            bb_re, bb_im)


def _s5_pow_kernel(ar_ref, ai_ref, tab_ref):
    def cmul(x, y):
        return x[0] * y[0] - x[1] * y[1], x[0] * y[1] + x[1] * y[0]

    def powers(b1):
        b2 = cmul(b1, b1)
        b3 = cmul(b2, b1)
        b4 = cmul(b2, b2)
        return (b1, b2, b3, b4, cmul(b4, b1), cmul(b4, b2), cmul(b4, b3), cmul(b4, b4))

    shape = (SUBLANES, S5_STATES)
    row = lax.broadcasted_iota(jnp.int32, shape, 0)
    zero = jnp.zeros(shape, F32)
    pw_a = powers((ar_ref[0], ai_ref[0]))
    pw_a8 = powers(pw_a[SUBLANES - 1])
    for base, pw in ((0, pw_a), (N_TABLES // 2, pw_a8)):
        for part in (0, 1):
            tab_ref[0, base + part] = jnp.where(row >= 1, pw[0][part], zero)
            tab_ref[0, base + 2 + part] = jnp.where(row >= 2, pw[1][part], zero)
            tab_ref[0, base + 4 + part] = jnp.where(row >= 4, pw[3][part], zero)
            p = zero
            for r, v in enumerate(pw):
                p = jnp.where(row == r, v[part], p)
            tab_ref[0, base + 6 + part] = p


def _s5_power_tables(ab_re, ab_im):
    depth = ab_re.shape[0]
    vec = pl.BlockSpec((1, 1, S5_STATES), lambda l: (l, 0, 0))
    return pl.pallas_call(
        _s5_pow_kernel,
        grid=(depth,),
        in_specs=[vec, vec],
        out_specs=pl.BlockSpec((1, N_TABLES, SUBLANES, S5_STATES), lambda l: (l, 0, 0, 0)),
        out_shape=jax.ShapeDtypeStruct((depth, N_TABLES, SUBLANES, S5_STATES), F32),
        name="s5_power_tables",
    )(ab_re, ab_im)


def _s5_block_matrices(bb_re, bb_im, c_re, c_im):
    gpt = CH_TILE // S5_GROUP
    gps = LANES // S5_P
    eye_q = jnp.eye(S5_Q, dtype=bool)
    eye_g = jnp.eye(gps, dtype=bool)
    mask = eye_q[:, :, None, None] & eye_g[None, None, :, :]

    tb = jnp.stack([bb_re, bb_im], axis=0)
    tb = tb.reshape(2, S5_KT, S5_Q, gps, S5_GROUP, S5_P)
    tb = jnp.transpose(tb, (1, 2, 4, 0, 3, 5))
    bm = jnp.where(mask[None, :, :, :, None, None, :, None],
                   tb[:, :, None, None, :, :, :, :], 0.0)
    bm = bm.reshape(S5_KT, S5_Q, gpt * S5_GROUP, 2 * LANES).astype(BF16)

    tc = jnp.stack([c_re, -c_im], axis=0)
    tc = tc.reshape(2, S5_KT, S5_Q, gps, S5_GROUP, S5_P)
    tc = jnp.transpose(tc, (1, 2, 0, 3, 5, 4))
    mask_c = jnp.transpose(mask, (0, 3, 1, 2))
    cm = jnp.where(mask_c[None, :, None, :, None, :, :, None],
                   tc[:, :, :, :, :, None, None, :], 0.0)
    cm = cm.reshape(S5_KT, S5_Q, 2 * LANES, gpt * S5_GROUP).astype(BF16)
    return bm, cm


def _mixer_c_kernel(cu_ref, cg_ref, bm_ref, cm_ref, tab_ref,
                    d_ref, wglu_ref, bglu_ref, sre_ref, sim_ref,
                    o_ref, sre_out_ref, sim_out_ref,
                    ystore, car_r, car_i, *scan_scr, bb, tt):
    kt = pl.program_id(2)
    cw = CH_TILE
    rows = bb * tt
    nblk = tt // SUBLANES
    nv = nblk // SUBLANES
    n = bb * nblk
    base8 = N_TABLES // 2

    @pl.when(pl.program_id(1) == 0)
    def _():
        car_r[kt] = sre_ref[...]
        car_i[kt] = sim_ref[...]

    u = cu_ref[...].reshape(rows, cw)
    ub = u.astype(BF16)
    y = d_ref[...] * u
    cr_all = car_r[kt]
    ci_all = car_i[kt]
    new_r, new_i = [], []
    for q in range(S5_Q):
        sl = slice(LANES * q, LANES * (q + 1))
        bu = jnp.dot(ub, bm_ref[0, q], preferred_element_type=F32)
        br, bi = _block_scan_cplx(bu[:, :LANES].reshape(n, SUBLANES, LANES),
                                  bu[:, LANES:].reshape(n, SUBLANES, LANES), tab_ref, 0, sl)
        p_r = tab_ref[0, 6, :, sl]
        p_i = tab_ref[0, 7, :, sl]
        cr = cr_all[:, :, sl]
        ci = ci_all[:, :, sl]
        if nblk == 1:
            kr, ki = cr, ci
        else:
            hsr, hsi, csr, csi = scan_scr
            hsr[...] = br.reshape(bb, tt, LANES)
            hsi[...] = bi.reshape(bb, tt, LANES)
            xr = hsr[:, pl.ds(SUBLANES - 1, nblk, stride=SUBLANES), :]
            xi = hsi[:, pl.ds(SUBLANES - 1, nblk, stride=SUBLANES), :]
            xr, xi = _block_scan_cplx(xr.reshape(bb * nv, SUBLANES, LANES),
                                      xi.reshape(bb * nv, SUBLANES, LANES), tab_ref, base8, sl)
            xr = xr.reshape(bb, nblk, LANES)
            xi = xi.reshape(bb, nblk, LANES)
            q_r = tab_ref[0, base8 + 6, :, sl]
            q_i = tab_ref[0, base8 + 7, :, sl]
            prev_r, prev_i = cr, ci
            ends_r, ends_i = [], []
            for j in range(nv):
                seg = slice(j * SUBLANES, (j + 1) * SUBLANES)
                er = xr[:, seg, :] + (q_r * prev_r - q_i * prev_i)
                ei = xi[:, seg, :] + (q_r * prev_i + q_i * prev_r)
                ends_r.append(er)
                ends_i.append(ei)
                prev_r = er[:, SUBLANES - 1:SUBLANES, :]
                prev_i = ei[:, SUBLANES - 1:SUBLANES, :]
            csr[:, 0:1, :] = cr
            csi[:, 0:1, :] = ci
            csr[:, 1:1 + nblk, :] = jnp.concatenate(ends_r, axis=1)
            csi[:, 1:1 + nblk, :] = jnp.concatenate(ends_i, axis=1)
            kr = _spread_rows(csr, nblk, bb).reshape(n, SUBLANES, LANES)
            ki = _spread_rows(csi, nblk, bb).reshape(n, SUBLANES, LANES)
        hr = br + (p_r * kr - p_i * ki)
        hi = bi + (p_r * ki + p_i * kr)
        hr = hr.reshape(bb, tt, LANES)
        hi = hi.reshape(bb, tt, LANES)
        new_r.append(hr[:, tt - 1:tt, :])
        new_i.append(hi[:, tt - 1:tt, :])
        hcat = jnp.concatenate([hr.reshape(rows, LANES), hi.reshape(rows, LANES)], axis=-1)
        y = y + jnp.dot(hcat.astype(BF16), cm_ref[0, q], preferred_element_type=F32)

    ncr_all = jnp.concatenate(new_r, axis=-1)
    nci_all = jnp.concatenate(new_i, axis=-1)
    car_r[kt] = ncr_all
    car_i[kt] = nci_all
    sre_out_ref[:, kt] = ncr_all
    sim_out_ref[:, kt] = nci_all
    ystore[kt] = y

    @pl.when(kt == S5_KT - 1)
    def _():
        y_all = jnp.concatenate([ystore[k] for k in range(S5_KT)], axis=-1)
        yg = jax.nn.gelu(y_all)
        z = jnp.dot(yg.astype(BF16), wglu_ref[0].astype(BF16),
                    preferred_element_type=F32) + bglu_ref[...]
        yy = yg * jax.nn.sigmoid(z)
        out = yy * jax.nn.silu(cg_ref[...].reshape(rows, W_C))
        o_ref[...] = out.astype(o_ref.dtype)


def _mixer_c(proj3, bmat, cmat, tables, d_skip, w_glu, b_glu, s_re, s_im, layer, bb, tt):
    b, t, _ = proj3.shape
    cw = CH_TILE
    nt = t // tt
    nblk = tt // SUBLANES
    u_off = 4 * W_A + 2 * W_B
    g_off = u_off + W_C
    kslab = S5_Q * LANES
    mat = pl.BlockSpec((1, S5_Q, cw, cw), lambda i, j, k: (k, 0, 0, 0))
    st = pl.BlockSpec((bb, 1, kslab), lambda i, j, k: (i, 0, k))
    st_out = pl.BlockSpec((bb, S5_KT, 1, kslab), lambda i, j, k: (i, 0, 0, 0))
    rows = bb * tt
    scratch = [pltpu.VMEM((S5_KT, rows, cw), F32),
               pltpu.VMEM((S5_KT, bb, 1, kslab), F32),
               pltpu.VMEM((S5_KT, bb, 1, kslab), F32)]
    if nblk > 1:
        assert nblk % SUBLANES == 0 and nblk // SUBLANES <= SUBLANES
        scratch += [pltpu.VMEM((bb, tt, LANES), F32), pltpu.VMEM((bb, tt, LANES), F32),
                    pltpu.VMEM((bb, HALO + nblk, LANES), F32),
                    pltpu.VMEM((bb, HALO + nblk, LANES), F32)]
    vmem = 2 * (_nbytes((bb, tt, cw), F32) + _nbytes((bb, tt, W_C), F32)
                + 2 * _nbytes((S5_Q, cw, cw), BF16)
                + _nbytes((N_TABLES, SUBLANES, kslab), F32)
                + _nbytes((W_C, W_C), F32) + _nbytes((bb, tt, W_C), BF16)
                + 4 * _nbytes((bb, SUBLANES, kslab), F32)) \
        + _nbytes((S5_KT, rows, cw), F32) + 2 * _nbytes((bb, tt, LANES), F32) \
        + 2 * _nbytes((S5_KT, bb, SUBLANES, kslab), F32) + 10 * _nbytes((rows, W_C), F32)
    return pl.pallas_call(
        functools.partial(_mixer_c_kernel, bb=bb, tt=tt),
        grid=(b // bb, nt, S5_KT),
        in_specs=[
            pl.BlockSpec((bb, tt, cw), lambda i, j, k: (i, j, u_off // cw + k)),
            pl.BlockSpec((bb, tt, W_C), lambda i, j, k: (i, j, g_off // W_C)),
            mat, mat,
            pl.BlockSpec((1, N_TABLES, SUBLANES, kslab), lambda i, j, k: (layer, 0, 0, k)),
            pl.BlockSpec((1, cw), lambda i, j, k: (0, k)),
            pl.BlockSpec((1, W_C, W_C), lambda i, j, k: (layer, 0, 0)),
            pl.BlockSpec((1, W_C), lambda i, j, k: (0, 0)),
            st, st,
        ],
        out_specs=[
            pl.BlockSpec((rows, W_C), lambda i, j, k: (i * nt + j, 0)),
            st_out, st_out,
        ],
        out_shape=[
            jax.ShapeDtypeStruct((b * t, W_C), BF16),
            jax.ShapeDtypeStruct((b, S5_KT, 1, kslab), F32),
            jax.ShapeDtypeStruct((b, S5_KT, 1, kslab), F32),
        ],
        scratch_shapes=scratch,
        compiler_params=_params(("arbitrary",) * 3, vmem),
        name="mixer_c",
    )(proj3, proj3, bmat, cmat, tables, d_skip.reshape(1, W_C), w_glu,
      b_glu.reshape(1, W_C), s_re, s_im)


def _out_proj_kernel(ma_ref, mb_ref, mc_ref, w_ref, x_ref, gate_ref, o_ref, *, bb, tt):
    w = w_ref[0].astype(BF16)
    acc = jnp.dot(ma_ref[...], w[0:W_A, :], preferred_element_type=F32)
    acc = acc + jnp.dot(mb_ref[...], w[W_A:W_A + W_B, :], preferred_element_type=F32)
    acc = acc + jnp.dot(mc_ref[...], w[W_A + W_B:, :], preferred_element_type=F32)
    o_ref[...] = x_ref[...] + gate_ref[...] * acc.reshape(bb, tt, acc.shape[-1])


def _out_projection(out_a, out_b, out_c, w_out, x, gate, layer, bb, tt):
    b, t, d = x.shape
    tm = bb * tt
    tn = 512
    nt = t // tt
    m = b * t

    def rows(width):
        return pl.BlockSpec((tm, width), lambda i, j: (i, 0))

    vmem = 2 * (_nbytes((tm, d), BF16) + _nbytes((d, tn), F32)
                + 2 * _nbytes((tm, tn), F32)) + _nbytes((d, tn), BF16) \
        + 3 * _nbytes((tm, tn), F32)
    return pl.pallas_call(
        functools.partial(_out_proj_kernel, bb=bb, tt=tt),
        grid=(m // tm, d // tn),
        in_specs=[
            rows(W_A), rows(W_B), rows(W_C),
            pl.BlockSpec((1, d, tn), lambda i, j: (layer, 0, j)),
            pl.BlockSpec((bb, tt, tn), lambda i, j: (i // nt, i % nt, j)),
            pl.BlockSpec((bb, 1, tn), lambda i, j: (i // nt, 0, j)),
        ],
        out_specs=pl.BlockSpec((bb, tt, tn), lambda i, j: (i // nt, i % nt, j)),
        out_shape=jax.ShapeDtypeStruct((b, t, d), F32),
        compiler_params=_params(("arbitrary", "arbitrary"), vmem + (4 << 20)),
        name="out_projection",
    )(out_a, out_b, out_c, w_out, x, gate)


def _gate_weights(w_rg, w_ig):
    hpt = CH_TILE // LRU_HD
    pairs = LRU_HEADS // hpt
    w = jnp.stack([w_rg, w_ig], axis=2)
    w = w.reshape(pairs, hpt, LRU_HD, 2, LRU_HD)
    eye = jnp.eye(hpt, dtype=bool)
    out = jnp.where(eye[None, :, None, None, :, None],
                    w[:, :, :, :, None, :], 0.0)
    return out.reshape(pairs, CH_TILE, 2 * CH_TILE).astype(BF16)


def _layer(x, mod, states, wts, layer, tiles):
    b, t, d = x.shape
    conv_a, conv_b, lru_h, s5_re, s5_im = states
    shift, scale, gate = (mod[:, None, k * d:(k + 1) * d] for k in range(3))

    h = _norm_modulate(x, scale, shift, wts["g_norm"], *tiles["norm"])
    proj = _in_projection(h, wts["w_in"], layer).reshape(b, t, IN_COLS)

    out_a, conv_a_new = _mixer_a(proj, wts["w_conv_a"], wts["b_conv_a"], conv_a, *tiles["a"])
    out_b, conv_b_new, lru_new = _mixer_b(
        proj, wts["w_conv_b"], wts["b_conv_b"], wts["w_gates"], wts["b_rg"], wts["b_ig"],
        wts["lru_lambda"], conv_b, lru_h.reshape(b, 1, W_B), *tiles["b"])
    out_c, re_new, im_new = _mixer_c(
        proj, wts["bmat"], wts["cmat"], wts["tables"], wts["s5_d"], wts["w_glu"],
        wts["b_glu"], s5_re.reshape(b, 1, S5_STATES), s5_im.reshape(b, 1, S5_STATES),
        layer, *tiles["c"])

    x_new = _out_projection(out_a, out_b, out_c, wts["w_out"], x, gate, layer, *tiles["out"])
    return (x_new, conv_a_new, conv_b_new, lru_new.reshape(b, W_B),
            re_new.reshape(b, S5_G, S5_P), im_new.reshape(b, S5_G, S5_P))


PROMPT_TILES = {"norm": (1, 256), "a": (1, 1024), "b": (1, 512), "c": (1, 256),
                "out": (1, 1024), "final": (1, 256)}
SAMPLE_TILES = {"norm": (32, 8), "a": (128, 8), "b": (64, 8), "c": (32, 8),
                "out": (128, 8), "final": (32, 8)}


def kernel(x_prompt, x_sample, c_prompt, c_sample, state_conv_a, state_conv_b, state_lru_h,
           state_s5_re, state_s5_im, g_norm, w_ada, b_ada, w_in, w_conv_a, b_conv_a,
           w_conv_b, b_conv_b, w_rg, b_rg, w_ig, b_ig, lru_lambda, s5_lambda_re,
           s5_lambda_im, s5_log_dt, s5_b_re, s5_b_im, s5_c_re, s5_c_im, s5_d, w_glu,
           b_glu, w_out, g_final):
    depth = w_in.shape[0]
    bp = x_prompt.shape[0]
    bs = x_sample.shape[0]

    c_all = jnp.concatenate([c_prompt, c_sample], axis=0)
    pad = (-c_all.shape[0]) % SUBLANES
    c_all = jnp.pad(c_all, ((0, pad), (0, 0)))
    mod = _modulation(c_all, w_ada, b_ada)

    ab_re, ab_im, bb_re, bb_im = _s5_discretize(s5_lambda_re, s5_lambda_im, s5_log_dt,
                                               s5_b_re, s5_b_im)
    tables = _s5_power_tables(ab_re, ab_im)

    xp, xs = x_prompt, x_sample
    zeros_p = (jnp.zeros((bp, CONV_A - 1, W_A), F32), jnp.zeros((bp, CONV_B - 1, W_B), F32),
               jnp.zeros((bp, W_B), F32), jnp.zeros((bp, S5_G, S5_P), F32),
               jnp.zeros((bp, S5_G, S5_P), F32))
    outs_p, outs_s = [], []
    for l in range(depth):
        bmat, cmat = _s5_block_matrices(bb_re[l], bb_im[l], s5_c_re[l], s5_c_im[l])
        wts = {
            "g_norm": g_norm[l], "w_in": w_in,
            "w_conv_a": w_conv_a[l], "b_conv_a": b_conv_a[l],
            "w_conv_b": w_conv_b[l], "b_conv_b": b_conv_b[l],
            "w_gates": _gate_weights(w_rg[l], w_ig[l]), "b_rg": b_rg[l], "b_ig": b_ig[l],
            "lru_lambda": lru_lambda[l], "bmat": bmat, "cmat": cmat,
            "tables": tables, "s5_d": s5_d[l],
            "w_glu": w_glu, "b_glu": b_glu[l], "w_out": w_out,
        }
        res_p = _layer(xp, mod[l, :bp], zeros_p, wts, l, PROMPT_TILES)
        xp = res_p[0]
        outs_p.append(res_p[1:])
        st_s = (state_conv_a[l], state_conv_b[l], state_lru_h[l], state_s5_re[l],
                state_s5_im[l])
        res_s = _layer(xs, mod[l, bp:bp + bs], st_s, wts, l, SAMPLE_TILES)
        xs = res_s[0]
        outs_s.append(res_s[1:])

    y_prompt = _final_norm(xp, g_final, *PROMPT_TILES["final"])
    y_sample = _final_norm(xs, g_final, *SAMPLE_TILES["final"])
    stack = lambda outs, k: jnp.stack([o[k] for o in outs])
    return (y_prompt, y_sample,
            *(stack(outs_p, k) for k in range(5)),
            *(stack(outs_s, k) for k in range(5)))
```

```python
import functools

import jax
import jax.numpy as jnp
from jax import lax
from jax.experimental import pallas as pl
from jax.experimental.pallas import tpu as pltpu

F32 = jnp.float32
BF16 = jnp.bfloat16

D_MODEL = 4096
W_A = D_MODEL // 4
W_B = D_MODEL // 2
W_C = D_MODEL // 4
IN_COLS = 4 * W_A + 2 * W_B + 2 * W_C
CONV_A = 3
CONV_B = 4
LRU_HEADS = 16
LRU_HD = W_B // LRU_HEADS
LRU_C = 8.0
S5_GROUP = 16
S5_G = W_C // S5_GROUP
S5_P = 64
S5_STATES = S5_G * S5_P
EPS = 1e-6

SUBLANES = 8
LANES = 128
MXU_DIM = 256
VMEM_LIMIT_CAP = 60000 * 1024
CH_TILE = MXU_DIM
S5_KT = W_C // CH_TILE
S5_Q = (CH_TILE // S5_GROUP) * S5_P // LANES
HALO = SUBLANES
N_TABLES = 16


def _params(sem, vmem_bytes):
    return pltpu.CompilerParams(
        dimension_semantics=sem,
        vmem_limit_bytes=int(min(VMEM_LIMIT_CAP, vmem_bytes)))


def _nbytes(shape, dtype):
    n = 1
    for s in shape:
        n *= s
    return n * jnp.dtype(dtype).itemsize


def _mod_kernel(c_ref, w_ref, b_ref, o_ref):
    c = c_ref[...].astype(BF16)
    w = w_ref[0].astype(BF16)
    o_ref[0] = jnp.dot(c, w, preferred_element_type=F32) + b_ref[0]


def _modulation(c_all, w_ada, b_ada):
    depth, d, n = w_ada.shape
    rows = c_all.shape[0]
    tn = 512
    vmem = 2 * (_nbytes((rows, d), F32) + _nbytes((d, tn), F32)
                + _nbytes((rows, tn), F32)) + _nbytes((d, tn), F32)
    return pl.pallas_call(
        _mod_kernel,
        grid=(depth, n // tn),
        in_specs=[
            pl.BlockSpec((rows, d), lambda l, j: (0, 0)),
            pl.BlockSpec((1, d, tn), lambda l, j: (l, 0, j)),
            pl.BlockSpec((1, 1, tn), lambda l, j: (l, 0, j)),
        ],
        out_specs=pl.BlockSpec((1, rows, tn), lambda l, j: (l, 0, j)),
        out_shape=jax.ShapeDtypeStruct((depth, rows, n), F32),
        compiler_params=_params(("arbitrary", "arbitrary"), vmem + (8 << 20)),
        name="adaln_modulation",
    )(c_all, w_ada, b_ada.reshape(depth, 1, n))


def _norm_mod_kernel(x_ref, scale_ref, shift_ref, g_ref, o_ref):
    x = x_ref[...]
    ms = jnp.mean(x * x, axis=-1, keepdims=True)
    y = (x * lax.rsqrt(ms + EPS)) * g_ref[...]
    h = y * (1.0 + scale_ref[...]) + shift_ref[...]
    o_ref[...] = h.reshape(o_ref.shape).astype(o_ref.dtype)


def _norm_modulate(x, scale, shift, g, bb, tt):
    b, t, d = x.shape
    nt = t // tt
    vmem = 2 * (_nbytes((bb, tt, d), F32) + _nbytes((bb, tt, d), BF16)) \
        + 4 * _nbytes((bb, tt, d), F32)
    return pl.pallas_call(
        _norm_mod_kernel,
        grid=(b // bb, nt),
        in_specs=[
            pl.BlockSpec((bb, tt, d), lambda i, j: (i, j, 0)),
            pl.BlockSpec((bb, 1, d), lambda i, j: (i, 0, 0)),
            pl.BlockSpec((bb, 1, d), lambda i, j: (i, 0, 0)),
            pl.BlockSpec((1, 1, d), lambda i, j: (0, 0, 0)),
        ],
        out_specs=pl.BlockSpec((bb * tt, d), lambda i, j: (i * nt + j, 0)),
        out_shape=jax.ShapeDtypeStruct((b * t, d), BF16),
        compiler_params=_params(("arbitrary", "arbitrary"), vmem),
        name="norm_modulate",
    )(x, scale, shift, g.reshape(1, 1, d))


def _final_norm_kernel(x_ref, g_ref, o_ref):
    x = x_ref[...]
    ms = jnp.mean(x * x, axis=-1, keepdims=True)
    o_ref[...] = (x * lax.rsqrt(ms + EPS)) * g_ref[...]


def _final_norm(x, g, bb, tt):
    b, t, d = x.shape
    vmem = 8 * _nbytes((bb, tt, d), F32)
    return pl.pallas_call(
        _final_norm_kernel,
        grid=(b // bb, t // tt),
        in_specs=[
            pl.BlockSpec((bb, tt, d), lambda i, j: (i, j, 0)),
            pl.BlockSpec((1, 1, d), lambda i, j: (0, 0, 0)),
        ],
        out_specs=pl.BlockSpec((bb, tt, d), lambda i, j: (i, j, 0)),
        out_shape=jax.ShapeDtypeStruct((b, t, d), F32),
        compiler_params=_params(("arbitrary", "arbitrary"), vmem),
        name="final_norm",
    )(x, g.reshape(1, 1, d))


PROJ_TN = 512


def _cast_weight_tile(w_ref, wb_ref, first_row_tile):
    @pl.when(first_row_tile)
    def _():
        wb_ref[...] = w_ref[0].astype(BF16)


def _in_proj_kernel(a_ref, w_ref, o_ref, wb_ref):
    _cast_weight_tile(w_ref, wb_ref, pl.program_id(1) == 0)
    o_ref[...] = jnp.dot(a_ref[...], wb_ref[...], preferred_element_type=F32)


def _in_proj_operands(h2d, w_in, layer, col0, ncols, tm, idx):
    m, k = h2d.shape
    tn = PROJ_TN
    return dict(
        args=[h2d, w_in],
        in_specs=[
            pl.BlockSpec((tm, k), lambda *g: (idx(*g)[1], 0)),
            pl.BlockSpec((1, k, tn), lambda *g: (layer, 0, col0 // tn + idx(*g)[0])),
        ],
        out_specs=[pl.BlockSpec((tm, tn), lambda *g: (idx(*g)[1], idx(*g)[0]))],
        out_shape=[jax.ShapeDtypeStruct((m, ncols), F32)],
        scratch=[pltpu.VMEM((k, tn), BF16)],
        vmem=2 * (_nbytes((tm, k), BF16) + _nbytes((k, tn), F32) + _nbytes((tm, tn), F32))
        + _nbytes((k, tn), BF16) + _nbytes((tm, tn), F32),
    )


def _in_projection(h2d, w_in, layer, col0, ncols):
    m = h2d.shape[0]
    tm = 1024
    ops = _in_proj_operands(h2d, w_in, layer, col0, ncols, tm, lambda j, i: (j, i))
    return pl.pallas_call(
        _in_proj_kernel,
        grid=(ncols // PROJ_TN, m // tm),
        in_specs=ops["in_specs"],
        out_specs=ops["out_specs"][0],
        out_shape=ops["out_shape"][0],
        scratch_shapes=ops["scratch"],
        compiler_params=_params(("arbitrary", "arbitrary"), ops["vmem"] + (4 << 20)),
        name="in_projection",
    )(*ops["args"])


def _causal_taps(scr, v, w_ref, taps, tt):
    scr[:, HALO:HALO + tt, :] = v
    acc = None
    for k in range(taps):
        src = v if k == taps - 1 else scr[:, HALO - (taps - 1) + k:HALO - (taps - 1) + k + tt, :]
        term = w_ref[k:k + 1, :] * src
        acc = term if acc is None else acc + term
    return acc


def _block_scan_real(a, b):
    row = lax.broadcasted_iota(jnp.int32, (1, SUBLANES, a.shape[-1]), 1)
    for d in (1, 2, 4):
        keep = row >= d
        a_sh = pltpu.roll(a, d, 1)
        b_sh = pltpu.roll(b, d, 1)
        b = b + a * jnp.where(keep, b_sh, 0.0)
        a = a * jnp.where(keep, a_sh, 1.0)
    return a, b


def _block_scan_cplx(br, bi, tab_ref, base, sl):
    for lvl, d in enumerate((1, 2, 4)):
        lr = tab_ref[0, base + 2 * lvl, :, sl]
        li = tab_ref[0, base + 2 * lvl + 1, :, sl]
        sr = pltpu.roll(br, d, 1)
        si = pltpu.roll(bi, d, 1)
        br, bi = br + (lr * sr - li * si), bi + (lr * si + li * sr)
    return br, bi


def _spread_rows(scr, nblk, bb):
    return jnp.concatenate(
        [jnp.broadcast_to(scr[:, k:k + 1, :], (bb, SUBLANES, LANES)) for k in range(nblk)],
        axis=1)


def _mixer_a_kernel(ab_ref, ac_ref, ax_ref, ag_ref, w_ref, b_ref, st_ref,
                    o_ref, st_out_ref, scr, *, tt):
    keep = CONV_A - 1

    @pl.when(pl.program_id(2) == 0)
    def _():
        scr[:, HALO - keep:HALO, :] = st_ref[...]

    conv_in = ac_ref[...] * ax_ref[...]
    y = b_ref[...] + _causal_taps(scr, conv_in, w_ref, CONV_A, tt)
    out = (ab_ref[...] * y) * jax.nn.silu(ag_ref[...])
    o_ref[...] = out.reshape(o_ref.shape).astype(o_ref.dtype)
    tail = scr[:, HALO + tt - keep:HALO + tt, :]
    st_out_ref[...] = tail
    scr[:, HALO - keep:HALO, :] = tail


def _mixer_a(proj3, w_conv, b_conv, state, bb, tt):
    b, t, _ = proj3.shape
    cw = CH_TILE
    nc = W_A // cw
    nt = t // tt
    blk = (bb, tt, cw)

    def col(off):
        return pl.BlockSpec(blk, lambda i, c, j, off=off: (i, j, off // cw + c))

    vmem = 2 * (4 * _nbytes(blk, F32) + _nbytes(blk, BF16)) \
        + _nbytes((bb, HALO + tt, cw), F32) + 6 * _nbytes(blk, F32)
    return pl.pallas_call(
        functools.partial(_mixer_a_kernel, tt=tt),
        grid=(b // bb, nc, nt),
        in_specs=[
            col(0), col(W_A), col(2 * W_A), col(3 * W_A),
            pl.BlockSpec((CONV_A, cw), lambda i, c, j: (0, c)),
            pl.BlockSpec((1, cw), lambda i, c, j: (0, c)),
            pl.BlockSpec((bb, CONV_A - 1, cw), lambda i, c, j: (i, 0, c)),
        ],
        out_specs=[
            pl.BlockSpec((bb * tt, cw), lambda i, c, j: (i * nt + j, c)),
            pl.BlockSpec((bb, CONV_A - 1, cw), lambda i, c, j: (i, 0, c)),
        ],
        out_shape=[
            jax.ShapeDtypeStruct((b * t, W_A), BF16),
            jax.ShapeDtypeStruct((b, CONV_A - 1, W_A), F32),
        ],
        scratch_shapes=[pltpu.VMEM((bb, HALO + tt, cw), F32)],
        compiler_params=_params(("arbitrary",) * 3, vmem),
        name="mixer_a",
    )(proj3, proj3, proj3, proj3, w_conv, b_conv.reshape(1, W_A), state)


MIXER_B_INS = 10
MIXER_B_OUTS = 3


def _mixer_b_init(cst_ref, hst_ref, scr, carry, first_time_tile):
    @pl.when(first_time_tile)
    def _():
        scr[:, HALO - (CONV_B - 1):HALO, :] = cst_ref[...]
        carry[...] = hst_ref[...]


def _mixer_b_kernel(*refs, bb, tt):
    ins = refs[:MIXER_B_INS]
    outs = refs[MIXER_B_INS:MIXER_B_INS + MIXER_B_OUTS]
    scr, carry, *scan_scr = refs[MIXER_B_INS + MIXER_B_OUTS:]
    _mixer_b_init(ins[8], ins[9], scr, carry, pl.program_id(2) == 0)
    _mixer_b_step(*ins[:8], *outs, scr, carry, scan_scr, bb=bb, tt=tt)


def _mixer_b_step(bx_ref, bg_ref, w_ref, b_ref, wg_ref, brg_ref, big_ref, lam_ref,
                  o_ref, cst_out_ref, hst_out_ref, scr, carry, scan_scr, *, bb, tt,
                  side_matmul=None):
    keep = CONV_B - 1
    cw = CH_TILE
    rows = bb * tt
    nblk = tt // SUBLANES
    nv = nblk // SUBLANES

    bx = bx_ref[...]
    xb = b_ref[...] + _causal_taps(scr, bx, w_ref, CONV_B, tt)
    tail = scr[:, HALO + tt - keep:HALO + tt, :]
    cst_out_ref[...] = tail
    scr[:, HALO - keep:HALO, :] = tail

    xb_bf16 = xb.reshape(rows, cw).astype(BF16)
    softplus_neg_lam = jax.nn.softplus(-lam_ref[...])
    cin_all = carry[...]
    h_parts, last_parts = [], []
    assert LRU_HD == LANES
    n_heads = cw // LRU_HD
    ticks_per_head = 4
    ticks = []

    def tick():
        if side_matmul is not None:
            side_matmul(len(ticks), n_heads * ticks_per_head)
            ticks.append(None)

    for s in range(n_heads):
        sl = slice(s * LANES, (s + 1) * LANES)
        w_head = jnp.concatenate([wg_ref[0, sl, sl], wg_ref[0, sl, cw + s * LANES:cw + (s + 1) * LANES]],
                                 axis=1)
        gates = jnp.dot(xb_bf16[:, sl], w_head, preferred_element_type=F32)
        tick()
        xs = xb[:, :, sl]
        r = jax.nn.sigmoid(gates[:, :LANES] + brg_ref[:, sl]).reshape(bb, tt, LANES)
        ig = jax.nn.sigmoid(gates[:, LANES:] + big_ref[:, sl]).reshape(bb, tt, LANES)
        tick()
        log_a = (-LRU_C * r) * softplus_neg_lam[:, sl]
        a = jnp.exp(log_a)
        beta = jnp.sqrt(-jnp.tanh(log_a) * (a * a + 1.0))
        bt = (beta * ig) * xs
        tick()
        cin = cin_all[:, :, sl]
        a_blk, b_blk = _block_scan_real(a.reshape(bb * nblk, SUBLANES, LANES),
                                        bt.reshape(bb * nblk, SUBLANES, LANES))
        tick()
        if nblk == 1:
            h = b_blk + a_blk * cin
            last = h[:, SUBLANES - 1:SUBLANES, :]
        else:
            acum, bcum, cscr = scan_scr
            acum[s] = a_blk.reshape(bb, tt, LANES)
            bcum[s] = b_blk.reshape(bb, tt, LANES)
            a2 = acum[s, :, pl.ds(SUBLANES - 1, nblk, stride=SUBLANES), :]
            b2 = bcum[s, :, pl.ds(SUBLANES - 1, nblk, stride=SUBLANES), :]
            a2, b2 = _block_scan_real(a2.reshape(bb * nv, SUBLANES, LANES),
                                      b2.reshape(bb * nv, SUBLANES, LANES))
            a2 = a2.reshape(bb, nblk, LANES)
            b2 = b2.reshape(bb, nblk, LANES)
            prev = cin
            ends = []
            for j in range(nv):
                seg = slice(j * SUBLANES, (j + 1) * SUBLANES)
                e = b2[:, seg, :] + a2[:, seg, :] * prev
                ends.append(e)
                prev = e[:, SUBLANES - 1:SUBLANES, :]
            last = prev
            cscr[s, :, 0:1, :] = cin
            cscr[s, :, 1:1 + nblk, :] = jnp.concatenate(ends, axis=1)
            spread = _spread_rows(cscr.at[s], nblk, bb)
            h = b_blk + a_blk * spread.reshape(bb * nblk, SUBLANES, LANES)
        h_parts.append(h.reshape(bb, tt, LANES))
        last_parts.append(last)
    h = jnp.concatenate(h_parts, axis=-1)
    last = jnp.concatenate(last_parts, axis=-1)
    carry[...] = last
    hst_out_ref[...] = last
    out = h * jax.nn.silu(bg_ref[...])
    o_ref[...] = out.reshape(o_ref.shape).astype(o_ref.dtype)


def _mixer_b_operands(proj3, w_conv, b_conv, w_gates, b_rg, b_ig, lam, conv_state, h_state,
                      bb, tt, idx):
    b, t, _ = proj3.shape
    cw = CH_TILE
    nt = t // tt
    nblk = tt // SUBLANES
    blk = (bb, tt, cw)

    def spec(shape, fn):
        return pl.BlockSpec(shape, lambda *g: fn(*idx(*g)))

    vec = spec((1, cw), lambda i, c, j: (0, c))
    scratch = [pltpu.VMEM((bb, HALO + tt, cw), F32), pltpu.VMEM((bb, 1, cw), F32)]
    if nblk > 1:
        assert nblk % SUBLANES == 0 and nblk // SUBLANES <= SUBLANES
        slabs = cw // LANES
        scratch += [pltpu.VMEM((slabs, bb, tt, LANES), F32),
                    pltpu.VMEM((slabs, bb, tt, LANES), F32),
                    pltpu.VMEM((slabs, bb, HALO + nblk, LANES), F32)]
    return dict(
        args=[proj3, proj3, w_conv, b_conv.reshape(1, W_B), w_gates,
              b_rg.reshape(1, W_B), b_ig.reshape(1, W_B), lam.reshape(1, W_B),
              conv_state, h_state],
        in_specs=[
            spec(blk, lambda i, c, j: (i, j, c)),
            spec(blk, lambda i, c, j: (i, j, W_B // cw + c)),
            spec((CONV_B, cw), lambda i, c, j: (0, c)),
            vec,
            spec((1, cw, 2 * cw), lambda i, c, j: (c, 0, 0)),
            vec, vec, vec,
            spec((bb, CONV_B - 1, cw), lambda i, c, j: (i, 0, c)),
            spec((bb, 1, cw), lambda i, c, j: (i, 0, c)),
        ],
        out_specs=[
            spec((bb * tt, cw), lambda i, c, j: (i * nt + j, c)),
            spec((bb, CONV_B - 1, cw), lambda i, c, j: (i, 0, c)),
            spec((bb, 1, cw), lambda i, c, j: (i, 0, c)),
        ],
        out_shape=[
            jax.ShapeDtypeStruct((b * t, W_B), BF16),
            jax.ShapeDtypeStruct((b, CONV_B - 1, W_B), F32),
            jax.ShapeDtypeStruct((b, 1, W_B), F32),
        ],
        scratch=scratch,
        vmem=2 * (2 * _nbytes(blk, F32) + _nbytes(blk, BF16) + _nbytes((cw, 2 * cw), BF16))
        + _nbytes((bb, HALO + tt, cw), F32) + 16 * _nbytes(blk, F32),
        grid=(b // bb, W_B // cw, nt),
    )


def _mixer_b(proj3, *params, bb, tt):
    ops = _mixer_b_operands(proj3, *params, bb, tt, lambda i, c, j: (i, c, j))
    return pl.pallas_call(
        functools.partial(_mixer_b_kernel, bb=bb, tt=tt),
        grid=ops["grid"],
        in_specs=ops["in_specs"],
        out_specs=ops["out_specs"],
        out_shape=ops["out_shape"],
        scratch_shapes=ops["scratch"],
        compiler_params=_params(("arbitrary",) * 3, ops["vmem"]),
        name="mixer_b",
    )(*ops["args"])


def _fused_proj_kernel(*refs, n_ins, n_outs, init, step, ni, first_time_tile, bb, tt):
    a_ref, w_ref = refs[:2]
    ins = refs[2:2 + n_ins]
    proj_ref = refs[2 + n_ins]
    outs = refs[3 + n_ins:3 + n_ins + n_outs]
    wb_ref, *scr = refs[3 + n_ins + n_outs:]
    s = pl.program_id(0)
    _cast_weight_tile(w_ref, wb_ref, lax.rem(s, ni) == 0)
    init(ins, scr, first_time_tile(s))
    k_total = a_ref.shape[1]
    acc = []

    def projection_part(part, n_parts):
        kc = MXU_DIM
        per_part = k_total // (kc * n_parts)
        for sub in range(part * per_part, (part + 1) * per_part):
            ks = slice(sub * kc, (sub + 1) * kc)
            term = jnp.dot(a_ref[:, ks], wb_ref[ks, :], preferred_element_type=F32)
            acc[:] = [term if not acc else acc[0] + term]
        if part == n_parts - 1:
            proj_ref[...] = acc[0]

    step(ins, outs, scr, s, projection_part, bb=bb, tt=tt)


def _fused_proj_call(name, h2d, w_in, layer, col0, ncols, mix_ops, kernel_kwargs):
    steps = 1
    for g in mix_ops["grid"]:
        steps *= g
    nj = ncols // PROJ_TN
    ni = steps // nj
    tm = h2d.shape[0] // ni
    assert nj * ni == steps and tm * ni == h2d.shape[0] and tm % SUBLANES == 0
    mm = _in_proj_operands(h2d, w_in, layer, col0, ncols, tm, lambda s: (s // ni, s % ni))
    outs = pl.pallas_call(
        functools.partial(_fused_proj_kernel, n_ins=len(mix_ops["args"]),
                          n_outs=len(mix_ops["out_shape"]), ni=ni, **kernel_kwargs),
        grid=(steps,),
        in_specs=mm["in_specs"] + mix_ops["in_specs"],
        out_specs=mm["out_specs"] + mix_ops["out_specs"],
        out_shape=mm["out_shape"] + mix_ops["out_shape"],
        scratch_shapes=mm["scratch"] + mix_ops["scratch"],
        compiler_params=_params(("arbitrary",), mm["vmem"] + mix_ops["vmem"]),
        name=name,
    )(*mm["args"], *mix_ops["args"])
    return outs


def _flat_index(dims):
    def idx(s):
        out = []
        for k, d in enumerate(dims):
            stride = 1
            for e in dims[k + 1:]:
                stride *= e
            out.append((s // stride) % d if k else s // stride)
        return tuple(out)
    return idx


def _fused_proj_mixer_b(h2d, w_in, layer, col0, ncols, proj3, *params, bb, tt):
    b, t, _ = proj3.shape
    dims = (b // bb, W_B // CH_TILE, t // tt)
    ops = _mixer_b_operands(proj3, *params, bb, tt, _flat_index(dims))
    ops["grid"] = dims
    nt = dims[2]

    def init(ins, scr, first):
        _mixer_b_init(ins[8], ins[9], scr[0], scr[1], first)

    def step(ins, outs, scr, s, side_matmul, *, bb, tt):
        _mixer_b_step(*ins[:8], *outs, scr[0], scr[1], scr[2:], bb=bb, tt=tt,
                      side_matmul=side_matmul)

    return _fused_proj_call(
        "proj_a_mixer_b", h2d, w_in, layer, col0, ncols, ops,
        dict(init=init, step=step, first_time_tile=lambda s: lax.rem(s, nt) == 0,
             bb=bb, tt=tt))


def _s5_disc_kernel(lre_ref, lim_ref, ldt_ref, btre_ref, btim_ref,
                    abre_ref, abim_ref, bbre_ref, bbim_ref):
    lre = lre_ref[0]
    lim = lim_ref[0]
    dt = jnp.exp(ldt_ref[0])
    mag = jnp.exp(lre * dt)
    ab_re = mag * jnp.cos(lim * dt)
    ab_im = mag * jnp.sin(lim * dt)
    nr = ab_re - 1.0
    den = lre * lre + lim * lim
    fr = (nr * lre + ab_im * lim) / den
    fi = (ab_im * lre - nr * lim) / den
    bre = btre_ref[0]
    bim = btim_ref[0]
    bbre_ref[0] = fr * bre - fi * bim
    bbim_ref[0] = fr * bim + fi * bre
    abre_ref[0] = ab_re
    abim_ref[0] = ab_im


def _s5_discretize(lam_re, lam_im, log_dt, b_re, b_im):
    depth = lam_re.shape[0]
    gp = (depth, S5_G, 1, S5_P)
    ldt = jnp.broadcast_to(log_dt[:, :, None, None], gp)
    bt_shape = (depth, S5_G, S5_GROUP, S5_P)
    small = pl.BlockSpec((1, S5_G, 1, S5_P), lambda l: (l, 0, 0, 0))
    big = pl.BlockSpec((1, S5_G, S5_GROUP, S5_P), lambda l: (l, 0, 0, 0))
    ab_re, ab_im, bb_re, bb_im = pl.pallas_call(
        _s5_disc_kernel,
        grid=(depth,),
        in_specs=[small, small, small, big, big],
        out_specs=[small, small, big, big],
        out_shape=[jax.ShapeDtypeStruct(gp, F32), jax.ShapeDtypeStruct(gp, F32),
                   jax.ShapeDtypeStruct(bt_shape, F32), jax.ShapeDtypeStruct(bt_shape, F32)],
        name="s5_discretize",
    )(lam_re.reshape(gp), lam_im.reshape(gp), ldt,
      jnp.swapaxes(b_re, 2, 3), jnp.swapaxes(b_im, 2, 3))
    return (ab_re.reshape(depth, 1, S5_STATES), ab_im.reshape(depth, 1, S5_STATES),
            bb_re, bb_im)


def _s5_pow_kernel(ar_ref, ai_ref, tab_ref):
    def cmul(x, y):
        return x[0] * y[0] - x[1] * y[1], x[0] * y[1] + x[1] * y[0]

    def powers(b1):
        b2 = cmul(b1, b1)
        b3 = cmul(b2, b1)
        b4 = cmul(b2, b2)
        return (b1, b2, b3, b4, cmul(b4, b1), cmul(b4, b2), cmul(b4, b3), cmul(b4, b4))

    shape = (SUBLANES, S5_STATES)
    row = lax.broadcasted_iota(jnp.int32, shape, 0)
    zero = jnp.zeros(shape, F32)
    pw_a = powers((ar_ref[0], ai_ref[0]))
    pw_a8 = powers(pw_a[SUBLANES - 1])
    for base, pw in ((0, pw_a), (N_TABLES // 2, pw_a8)):
        for part in (0, 1):
            tab_ref[0, base + part] = jnp.where(row >= 1, pw[0][part], zero)
            tab_ref[0, base + 2 + part] = jnp.where(row >= 2, pw[1][part], zero)
            tab_ref[0, base + 4 + part] = jnp.where(row >= 4, pw[3][part], zero)
            p = zero
            for r, v in enumerate(pw):
                p = jnp.where(row == r, v[part], p)
            tab_ref[0, base + 6 + part] = p


def _s5_power_tables(ab_re, ab_im):
    depth = ab_re.shape[0]
    vec = pl.BlockSpec((1, 1, S5_STATES), lambda l: (l, 0, 0))
    return pl.pallas_call(
        _s5_pow_kernel,
        grid=(depth,),
        in_specs=[vec, vec],
        out_specs=pl.BlockSpec((1, N_TABLES, SUBLANES, S5_STATES), lambda l: (l, 0, 0, 0)),
        out_shape=jax.ShapeDtypeStruct((depth, N_TABLES, SUBLANES, S5_STATES), F32),
        name="s5_power_tables",
    )(ab_re, ab_im)


def _s5_block_matrices(bb_re, bb_im, c_re, c_im):
    depth = bb_re.shape[0]
    gpt = CH_TILE // S5_GROUP
    chan = lax.broadcasted_iota(jnp.int32, (S5_Q, CH_TILE, CH_TILE), 1)
    state = lax.broadcasted_iota(jnp.int32, (S5_Q, CH_TILE, CH_TILE), 2)
    slab = lax.broadcasted_iota(jnp.int32, (S5_Q, CH_TILE, CH_TILE), 0)
    gps = LANES // S5_P
    keep = (chan // S5_GROUP) == gps * slab + (state % LANES) // S5_P

    def dense_b(x):
        return x.reshape(depth * S5_KT, gpt * S5_GROUP, S5_P)

    db = jnp.concatenate([dense_b(bb_re)] * gps + [dense_b(bb_im)] * gps, axis=-1)
    bm = jnp.where(keep[None], db[:, None], 0.0).astype(BF16)

    def dense_c(x):
        x = x.reshape(depth * S5_KT, gpt, S5_GROUP, S5_P)
        return jnp.transpose(x, (0, 3, 1, 2)).reshape(depth * S5_KT, S5_P, gpt * S5_GROUP)

    dc = jnp.concatenate([dense_c(c_re)] * gps + [dense_c(-c_im)] * gps, axis=1)
    keep_c = jnp.swapaxes(keep, 1, 2)
    cm = jnp.where(keep_c[None], dc[:, None], 0.0).astype(BF16)
    return bm, cm


MIXER_C_INS = 10
MIXER_C_OUTS = 3


def _mixer_c_init(sre_ref, sim_ref, car_r, car_i, kt, first_time_tile):
    @pl.when(first_time_tile)
    def _():
        car_r[kt] = sre_ref[...]
        car_i[kt] = sim_ref[...]


def _mixer_c_kernel(*refs, bb, tt):
    ins = refs[:MIXER_C_INS]
    outs = refs[MIXER_C_INS:MIXER_C_INS + MIXER_C_OUTS]
    ystore, car_r, car_i, *scan_scr = refs[MIXER_C_INS + MIXER_C_OUTS:]
    kt = pl.program_id(2)
    _mixer_c_init(ins[8], ins[9], car_r, car_i, kt, pl.program_id(1) == 0)
    _mixer_c_step(*ins[:8], *outs, ystore, car_r, car_i, scan_scr, kt, bb=bb, tt=tt)


def _mixer_c_step(cu_ref, cg_ref, bm_ref, cm_ref, tab_ref, d_ref, wglu_ref, bglu_ref,
                  o_ref, sre_out_ref, sim_out_ref,
                  ystore, car_r, car_i, scan_scr, kt, *, bb, tt, side_matmul=None):
    cw = CH_TILE
    rows = bb * tt
    nblk = tt // SUBLANES
    nv = nblk // SUBLANES
    n = bb * nblk
    base8 = N_TABLES // 2

    u = cu_ref[...].reshape(rows, cw)
    ub = u.astype(BF16)
    y = d_ref[...] * u
    cr_all = car_r[kt]
    ci_all = car_i[kt]
    new_r, new_i = [], []
    for q in range(S5_Q):
        sl = slice(LANES * q, LANES * (q + 1))
        bu = jnp.dot(ub, bm_ref[0, q], preferred_element_type=F32)
        if side_matmul is not None:
            side_matmul(q, S5_Q)
        br, bi = _block_scan_cplx(bu[:, :LANES].reshape(n, SUBLANES, LANES),
                                  bu[:, LANES:].reshape(n, SUBLANES, LANES), tab_ref, 0, sl)
        p_r = tab_ref[0, 6, :, sl]
        p_i = tab_ref[0, 7, :, sl]
        cr = cr_all[:, :, sl]
        ci = ci_all[:, :, sl]
        if nblk == 1:
            kr, ki = cr, ci
        else:
            hsr, hsi, csr, csi = scan_scr
            hsr[...] = br.reshape(bb, tt, LANES)
            hsi[...] = bi.reshape(bb, tt, LANES)
            xr = hsr[:, pl.ds(SUBLANES - 1, nblk, stride=SUBLANES), :]
            xi = hsi[:, pl.ds(SUBLANES - 1, nblk, stride=SUBLANES), :]
            xr, xi = _block_scan_cplx(xr.reshape(bb * nv, SUBLANES, LANES),
                                      xi.reshape(bb * nv, SUBLANES, LANES), tab_ref, base8, sl)
            xr = xr.reshape(bb, nblk, LANES)
            xi = xi.reshape(bb, nblk, LANES)
            q_r = tab_ref[0, base8 + 6, :, sl]
            q_i = tab_ref[0, base8 + 7, :, sl]
            prev_r, prev_i = cr, ci
            ends_r, ends_i = [], []
            for j in range(nv):
                seg = slice(j * SUBLANES, (j + 1) * SUBLANES)
                er = xr[:, seg, :] + (q_r * prev_r - q_i * prev_i)
                ei = xi[:, seg, :] + (q_r * prev_i + q_i * prev_r)
                ends_r.append(er)
                ends_i.append(ei)
                prev_r = er[:, SUBLANES - 1:SUBLANES, :]
                prev_i = ei[:, SUBLANES - 1:SUBLANES, :]
            csr[:, 0:1, :] = cr
            csi[:, 0:1, :] = ci
            csr[:, 1:1 + nblk, :] = jnp.concatenate(ends_r, axis=1)
            csi[:, 1:1 + nblk, :] = jnp.concatenate(ends_i, axis=1)
            kr = _spread_rows(csr, nblk, bb).reshape(n, SUBLANES, LANES)
            ki = _spread_rows(csi, nblk, bb).reshape(n, SUBLANES, LANES)
        hr = br + (p_r * kr - p_i * ki)
        hi = bi + (p_r * ki + p_i * kr)
        hr = hr.reshape(bb, tt, LANES)
        hi = hi.reshape(bb, tt, LANES)
        new_r.append(hr[:, tt - 1:tt, :])
        new_i.append(hi[:, tt - 1:tt, :])
        hcat = jnp.concatenate([hr.reshape(rows, LANES), hi.reshape(rows, LANES)], axis=-1)
        y = y + jnp.dot(hcat.astype(BF16), cm_ref[0, q], preferred_element_type=F32)

    ncr_all = jnp.concatenate(new_r, axis=-1)
    nci_all = jnp.concatenate(new_i, axis=-1)
    car_r[kt] = ncr_all
    car_i[kt] = nci_all
    sre_out_ref[:, kt] = ncr_all
    sim_out_ref[:, kt] = nci_all
    ystore[kt] = y

    @pl.when(kt == S5_KT - 1)
    def _():
        y_all = jnp.concatenate([ystore[k] for k in range(S5_KT)], axis=-1)
        yg = jax.nn.gelu(y_all)
        z = jnp.dot(yg.astype(BF16), wglu_ref[0].astype(BF16),
                    preferred_element_type=F32) + bglu_ref[...]
        yy = yg * jax.nn.sigmoid(z)
        out = yy * jax.nn.silu(cg_ref[...].reshape(rows, W_C))
        o_ref[...] = out.astype(o_ref.dtype)


def _mixer_c_operands(proj3, bmat, cmat, tables, d_skip, w_glu, b_glu, s_re, s_im, layer,
                      bb, tt, idx):
    b, t, _ = proj3.shape
    cw = CH_TILE
    nt = t // tt
    nblk = tt // SUBLANES
    kslab = S5_Q * LANES

    def spec(shape, fn):
        return pl.BlockSpec(shape, lambda *g: fn(*idx(*g)))

    mat = spec((1, S5_Q, cw, cw), lambda i, j, k: (layer * S5_KT + k, 0, 0, 0))
    st = spec((bb, 1, kslab), lambda i, j, k: (i, 0, k))
    st_out = spec((bb, S5_KT, 1, kslab), lambda i, j, k: (i, 0, 0, 0))
    rows = bb * tt
    scratch = [pltpu.VMEM((S5_KT, rows, cw), F32),
               pltpu.VMEM((S5_KT, bb, 1, kslab), F32),
               pltpu.VMEM((S5_KT, bb, 1, kslab), F32)]
    if nblk > 1:
        assert nblk % SUBLANES == 0 and nblk // SUBLANES <= SUBLANES
        scratch += [pltpu.VMEM((bb, tt, LANES), F32), pltpu.VMEM((bb, tt, LANES), F32),
                    pltpu.VMEM((bb, HALO + nblk, LANES), F32),
                    pltpu.VMEM((bb, HALO + nblk, LANES), F32)]
    vmem = 2 * (_nbytes((bb, tt, cw), F32) + _nbytes((bb, tt, W_C), F32)
                + 2 * _nbytes((S5_Q, cw, cw), BF16)
                + _nbytes((N_TABLES, SUBLANES, kslab), F32)
                + _nbytes((W_C, W_C), F32) + _nbytes((bb, tt, W_C), BF16)
                + 4 * _nbytes((bb, SUBLANES, kslab), F32)) \
        + _nbytes((S5_KT, rows, cw), F32) + 2 * _nbytes((bb, tt, LANES), F32) \
        + 2 * _nbytes((S5_KT, bb, SUBLANES, kslab), F32) + 10 * _nbytes((rows, W_C), F32)
    return dict(
        args=[proj3, proj3, bmat, cmat, tables, d_skip.reshape(1, W_C), w_glu,
              b_glu.reshape(1, W_C), s_re, s_im],
        in_specs=[
            spec((bb, tt, cw), lambda i, j, k: (i, j, k)),
            spec((bb, tt, W_C), lambda i, j, k: (i, j, 1)),
            mat, mat,
            spec((1, N_TABLES, SUBLANES, kslab), lambda i, j, k: (layer, 0, 0, k)),
            spec((1, cw), lambda i, j, k: (0, k)),
            spec((1, W_C, W_C), lambda i, j, k: (layer, 0, 0)),
            spec((1, W_C), lambda i, j, k: (0, 0)),
            st, st,
        ],
        out_specs=[
            spec((rows, W_C), lambda i, j, k: (i * nt + j, 0)),
            st_out, st_out,
        ],
        out_shape=[
            jax.ShapeDtypeStruct((b * t, W_C), BF16),
            jax.ShapeDtypeStruct((b, S5_KT, 1, kslab), F32),
            jax.ShapeDtypeStruct((b, S5_KT, 1, kslab), F32),
        ],
        scratch=scratch,
        vmem=vmem,
        grid=(b // bb, nt, S5_KT),
    )


def _mixer_c(proj3, *params, bb, tt):
    ops = _mixer_c_operands(proj3, *params, bb, tt, lambda i, j, k: (i, j, k))
    return pl.pallas_call(
        functools.partial(_mixer_c_kernel, bb=bb, tt=tt),
        grid=ops["grid"],
        in_specs=ops["in_specs"],
        out_specs=ops["out_specs"],
        out_shape=ops["out_shape"],
        scratch_shapes=ops["scratch"],
        compiler_params=_params(("arbitrary",) * 3, ops["vmem"]),
        name="mixer_c",
    )(*ops["args"])


def _fused_proj_mixer_c(h2d, w_in, layer, col0, ncols, proj3, *params, bb, tt):
    b, t, _ = proj3.shape
    dims = (b // bb, t // tt, S5_KT)
    ops = _mixer_c_operands(proj3, *params, bb, tt, _flat_index(dims))
    ops["grid"] = dims
    nt = dims[1]

    def init(ins, scr, first):
        _mixer_c_init(ins[8], ins[9], scr[1], scr[2], lax.rem(pl.program_id(0), S5_KT), first)

    def step(ins, outs, scr, s, side_matmul, *, bb, tt):
        _mixer_c_step(*ins[:8], *outs, scr[0], scr[1], scr[2], scr[3:],
                      lax.rem(s, S5_KT), bb=bb, tt=tt, side_matmul=side_matmul)

    return _fused_proj_call(
        "proj_b_mixer_c", h2d, w_in, layer, col0, ncols, ops,
        dict(init=init, step=step,
             first_time_tile=lambda s: lax.rem(s // S5_KT, nt) == 0, bb=bb, tt=tt))


def _out_proj_kernel(ma_ref, mb_ref, mc_ref, w_ref, x_ref, gate_ref, o_ref, *, bb, tt):
    w = w_ref[0].astype(BF16)
    acc = jnp.dot(ma_ref[...], w[0:W_A, :], preferred_element_type=F32)
    acc = acc + jnp.dot(mb_ref[...], w[W_A:W_A + W_B, :], preferred_element_type=F32)
    acc = acc + jnp.dot(mc_ref[...], w[W_A + W_B:, :], preferred_element_type=F32)
    o_ref[...] = x_ref[...] + gate_ref[...] * acc.reshape(bb, tt, acc.shape[-1])


def _out_projection(out_a, out_b, out_c, w_out, x, gate, layer, bb, tt):
    b, t, d = x.shape
    tm = bb * tt
    tn = 512
    nt = t // tt
    m = b * t

    def rows(width):
        return pl.BlockSpec((tm, width), lambda i, j: (i, 0))

    vmem = 2 * (_nbytes((tm, d), BF16) + _nbytes((d, tn), F32)
                + 2 * _nbytes((tm, tn), F32)) + _nbytes((d, tn), BF16) \
        + 3 * _nbytes((tm, tn), F32)
    return pl.pallas_call(
        functools.partial(_out_proj_kernel, bb=bb, tt=tt),
        grid=(m // tm, d // tn),
        in_specs=[
            rows(W_A), rows(W_B), rows(W_C),
            pl.BlockSpec((1, d, tn), lambda i, j: (layer, 0, j)),
            pl.BlockSpec((bb, tt, tn), lambda i, j: (i // nt, i % nt, j)),
            pl.BlockSpec((bb, 1, tn), lambda i, j: (i // nt, 0, j)),
        ],
        out_specs=pl.BlockSpec((bb, tt, tn), lambda i, j: (i // nt, i % nt, j)),
        out_shape=jax.ShapeDtypeStruct((b, t, d), F32),
        compiler_params=_params(("arbitrary", "arbitrary"), vmem + (4 << 20)),
        name="out_projection",
    )(out_a, out_b, out_c, w_out, x, gate)


def _gate_weights(w_rg, w_ig):
    hpt = CH_TILE // LRU_HD
    pairs = LRU_HEADS // hpt
    w = jnp.stack([w_rg, w_ig], axis=2)
    w = w.reshape(pairs, hpt, LRU_HD, 2, LRU_HD)
    eye = jnp.eye(hpt, dtype=bool)
    out = jnp.where(eye[None, :, None, None, :, None],
                    w[:, :, :, :, None, :], 0.0)
    return out.reshape(pairs, CH_TILE, 2 * CH_TILE).astype(BF16)


def _layer(x, mod, states, wts, layer, tiles):
    b, t, d = x.shape
    conv_a, conv_b, lru_h, s5_re, s5_im = states
    shift, scale, gate = (mod[:, None, k * d:(k + 1) * d] for k in range(3))

    h = _norm_modulate(x, scale, shift, wts["g_norm"], *tiles["norm"])
    w_in = wts["w_in"]
    col_a, col_b, col_c = 0, 4 * W_A, 4 * W_A + 2 * W_B
    b_params = (wts["w_conv_b"], wts["b_conv_b"], wts["w_gates"], wts["b_rg"], wts["b_ig"],
                wts["lru_lambda"], conv_b, lru_h.reshape(b, 1, W_B))
    c_params = (wts["bmat"], wts["cmat"], wts["tables"], wts["s5_d"], wts["w_glu"],
                wts["b_glu"], s5_re.reshape(b, 1, S5_STATES), s5_im.reshape(b, 1, S5_STATES),
                layer)
    bb_b, tt_b = tiles["b"]
    bb_c, tt_c = tiles["c"]

    proj_c = _in_projection(h, w_in, layer, col_c, 2 * W_C).reshape(b, t, 2 * W_C)
    if tiles["fuse"]:
        proj_b, out_c, re_new, im_new = _fused_proj_mixer_c(
            h, w_in, layer, col_b, 2 * W_B, proj_c, *c_params, bb=bb_c, tt=tt_c)
        proj_b = proj_b.reshape(b, t, 2 * W_B)
        proj_a, out_b, conv_b_new, lru_new = _fused_proj_mixer_b(
            h, w_in, layer, col_a, 4 * W_A, proj_b, *b_params, bb=bb_b, tt=tt_b)
        proj_a = proj_a.reshape(b, t, 4 * W_A)
    else:
        out_c, re_new, im_new = _mixer_c(proj_c, *c_params, bb=bb_c, tt=tt_c)
        proj_b = _in_projection(h, w_in, layer, col_b, 2 * W_B).reshape(b, t, 2 * W_B)
        out_b, conv_b_new, lru_new = _mixer_b(proj_b, *b_params, bb=bb_b, tt=tt_b)
        proj_a = _in_projection(h, w_in, layer, col_a, 4 * W_A).reshape(b, t, 4 * W_A)
    out_a, conv_a_new = _mixer_a(proj_a, wts["w_conv_a"], wts["b_conv_a"], conv_a,
                                 *tiles["a"])

    x_new = _out_projection(out_a, out_b, out_c, wts["w_out"], x, gate, layer, *tiles["out"])
    return (x_new, conv_a_new, conv_b_new, lru_new.reshape(b, W_B),
            re_new.reshape(b, S5_G, S5_P), im_new.reshape(b, S5_G, S5_P))


PROMPT_TILES = {"norm": (1, 256), "a": (1, 1024), "b": (1, 512), "c": (1, 256),
                "out": (1, 1024), "final": (1, 256), "fuse": True}
SAMPLE_TILES = {"norm": (32, 8), "a": (128, 8), "b": (64, 8), "c": (32, 8),
                "out": (128, 8), "final": (32, 8), "fuse": False}


def kernel(x_prompt, x_sample, c_prompt, c_sample, state_conv_a, state_conv_b, state_lru_h,
           state_s5_re, state_s5_im, g_norm, w_ada, b_ada, w_in, w_conv_a, b_conv_a,
           w_conv_b, b_conv_b, w_rg, b_rg, w_ig, b_ig, lru_lambda, s5_lambda_re,
           s5_lambda_im, s5_log_dt, s5_b_re, s5_b_im, s5_c_re, s5_c_im, s5_d, w_glu,
           b_glu, w_out, g_final):
    depth = w_in.shape[0]
    bp = x_prompt.shape[0]
    bs = x_sample.shape[0]

    c_all = jnp.concatenate([c_prompt, c_sample], axis=0)
    pad = (-c_all.shape[0]) % SUBLANES
    c_all = jnp.pad(c_all, ((0, pad), (0, 0)))
    mod = _modulation(c_all, w_ada, b_ada)

    ab_re, ab_im, bb_re, bb_im = _s5_discretize(s5_lambda_re, s5_lambda_im, s5_log_dt,
                                               s5_b_re, s5_b_im)
    tables = _s5_power_tables(ab_re, ab_im)
    bmat, cmat = _s5_block_matrices(bb_re, bb_im, s5_c_re, s5_c_im)

    xp, xs = x_prompt, x_sample
    zeros_p = (jnp.zeros((bp, CONV_A - 1, W_A), F32), jnp.zeros((bp, CONV_B - 1, W_B), F32),
               jnp.zeros((bp, W_B), F32), jnp.zeros((bp, S5_G, S5_P), F32),
               jnp.zeros((bp, S5_G, S5_P), F32))
    outs_p, outs_s = [], []
    for l in range(depth):
        wts = {
            "g_norm": g_norm[l], "w_in": w_in,
            "w_conv_a": w_conv_a[l], "b_conv_a": b_conv_a[l],
            "w_conv_b": w_conv_b[l], "b_conv_b": b_conv_b[l],
            "w_gates": _gate_weights(w_rg[l], w_ig[l]), "b_rg": b_rg[l], "b_ig": b_ig[l],
            "lru_lambda": lru_lambda[l], "bmat": bmat, "cmat": cmat,
            "tables": tables, "s5_d": s5_d[l],
            "w_glu": w_glu, "b_glu": b_glu[l], "w_out": w_out,
        }
        res_p = _layer(xp, mod[l, :bp], zeros_p, wts, l, PROMPT_TILES)
        xp = res_p[0]
        outs_p.append(res_p[1:])
        st_s = (state_conv_a[l], state_conv_b[l], state_lru_h[l], state_s5_re[l],
                state_s5_im[l])
        res_s = _layer(xs, mod[l, bp:bp + bs], st_s, wts, l, SAMPLE_TILES)
        xs = res_s[0]
        outs_s.append(res_s[1:])

    y_prompt = _final_norm(xp, g_final, *PROMPT_TILES["final"])
    y_sample = _final_norm(xs, g_final, *SAMPLE_TILES["final"])
    stack = lambda outs, k: jnp.stack([o[k] for o in outs])
    return (y_prompt, y_sample,
            *(stack(outs_p, k) for k in range(5)),
            *(stack(outs_s, k) for k in range(5)))
```

```python
import functools

import jax
import jax.numpy as jnp
from jax import lax
from jax.experimental import pallas as pl
from jax.experimental.pallas import tpu as pltpu

F32 = jnp.float32
BF16 = jnp.bfloat16

D_MODEL = 4096
W_A = D_MODEL // 4
W_B = D_MODEL // 2
W_C = D_MODEL // 4
IN_COLS = 4 * W_A + 2 * W_B + 2 * W_C
CONV_A = 3
CONV_B = 4
LRU_HEADS = 16
LRU_HD = W_B // LRU_HEADS
LRU_C = 8.0
S5_GROUP = 16
S5_G = W_C // S5_GROUP
S5_P = 64
S5_STATES = S5_G * S5_P
EPS = 1e-6

SUBLANES = 8
LANES = 128
MXU_DIM = 256
VMEM_LIMIT_CAP = 60000 * 1024
CH_TILE = MXU_DIM
S5_KT = W_C // CH_TILE
S5_Q = (CH_TILE // S5_GROUP) * S5_P // LANES
HALO = SUBLANES
N_TABLES = 16


def _params(sem, vmem_bytes):
    return pltpu.CompilerParams(
        dimension_semantics=sem,
        vmem_limit_bytes=int(min(VMEM_LIMIT_CAP, vmem_bytes)))


def _nbytes(shape, dtype):
    n = 1
    for s in shape:
        n *= s
    return n * jnp.dtype(dtype).itemsize


def _mod_kernel(c_ref, w_ref, b_ref, o_ref):
    c = c_ref[...].astype(BF16)
    w = w_ref[0].astype(BF16)
    o_ref[0] = jnp.dot(c, w, preferred_element_type=F32) + b_ref[0]


def _modulation(c_all, w_ada, b_ada):
    depth, d, n = w_ada.shape
    rows = c_all.shape[0]
    tn = 512
    vmem = 2 * (_nbytes((rows, d), F32) + _nbytes((d, tn), F32)
                + _nbytes((rows, tn), F32)) + _nbytes((d, tn), F32)
    return pl.pallas_call(
        _mod_kernel,
        grid=(depth, n // tn),
        in_specs=[
            pl.BlockSpec((rows, d), lambda l, j: (0, 0)),
            pl.BlockSpec((1, d, tn), lambda l, j: (l, 0, j)),
            pl.BlockSpec((1, 1, tn), lambda l, j: (l, 0, j)),
        ],
        out_specs=pl.BlockSpec((1, rows, tn), lambda l, j: (l, 0, j)),
        out_shape=jax.ShapeDtypeStruct((depth, rows, n), F32),
        compiler_params=_params(("arbitrary", "arbitrary"), vmem + (8 << 20)),
        name="adaln_modulation",
    )(c_all, w_ada, b_ada.reshape(depth, 1, n))


def _norm_mod_kernel(x_ref, scale_ref, shift_ref, g_ref, o_ref):
    x = x_ref[...]
    ms = jnp.mean(x * x, axis=-1, keepdims=True)
    y = (x * lax.rsqrt(ms + EPS)) * g_ref[...]
    h = y * (1.0 + scale_ref[...]) + shift_ref[...]
    o_ref[...] = h.reshape(o_ref.shape).astype(o_ref.dtype)


def _norm_modulate(x, scale, shift, g, bb, tt):
    b, t, d = x.shape
    nt = t // tt
    vmem = 2 * (_nbytes((bb, tt, d), F32) + _nbytes((bb, tt, d), BF16)) \
        + 4 * _nbytes((bb, tt, d), F32)
    return pl.pallas_call(
        _norm_mod_kernel,
        grid=(b // bb, nt),
        in_specs=[
            pl.BlockSpec((bb, tt, d), lambda i, j: (i, j, 0)),
            pl.BlockSpec((bb, 1, d), lambda i, j: (i, 0, 0)),
            pl.BlockSpec((bb, 1, d), lambda i, j: (i, 0, 0)),
            pl.BlockSpec((1, 1, d), lambda i, j: (0, 0, 0)),
        ],
        out_specs=pl.BlockSpec((bb * tt, d), lambda i, j: (i * nt + j, 0)),
        out_shape=jax.ShapeDtypeStruct((b * t, d), BF16),
        compiler_params=_params(("arbitrary", "arbitrary"), vmem),
        name="norm_modulate",
    )(x, scale, shift, g.reshape(1, 1, d))


def _final_norm_kernel(x_ref, g_ref, o_ref):
    x = x_ref[...]
    ms = jnp.mean(x * x, axis=-1, keepdims=True)
    o_ref[...] = (x * lax.rsqrt(ms + EPS)) * g_ref[...]


def _final_norm(x, g, bb, tt):
    b, t, d = x.shape
    vmem = 8 * _nbytes((bb, tt, d), F32)
    return pl.pallas_call(
        _final_norm_kernel,
        grid=(b // bb, t // tt),
        in_specs=[
            pl.BlockSpec((bb, tt, d), lambda i, j: (i, j, 0)),
            pl.BlockSpec((1, 1, d), lambda i, j: (0, 0, 0)),
        ],
        out_specs=pl.BlockSpec((bb, tt, d), lambda i, j: (i, j, 0)),
        out_shape=jax.ShapeDtypeStruct((b, t, d), F32),
        compiler_params=_params(("arbitrary", "arbitrary"), vmem),
        name="final_norm",
    )(x, g.reshape(1, 1, d))


PROJ_TN = 512


def _cast_weight_tile(w_ref, wb_ref, first_row_tile):
    @pl.when(first_row_tile)
    def _():
        wb_ref[...] = w_ref[0].astype(BF16)


def _in_proj_kernel(a_ref, w_ref, o_ref, wb_ref):
    _cast_weight_tile(w_ref, wb_ref, pl.program_id(1) == 0)
    o_ref[...] = jnp.dot(a_ref[...], wb_ref[...], preferred_element_type=F32)


def _in_proj_operands(h2d, w_in, layer, col0, ncols, tm, idx):
    m, k = h2d.shape
    tn = PROJ_TN
    return dict(
        args=[h2d, w_in],
        in_specs=[
            pl.BlockSpec((tm, k), lambda *g: (idx(*g)[1], 0)),
            pl.BlockSpec((1, k, tn), lambda *g: (layer, 0, col0 // tn + idx(*g)[0])),
        ],
        out_specs=[pl.BlockSpec((tm, tn), lambda *g: (idx(*g)[1], idx(*g)[0]))],
        out_shape=[jax.ShapeDtypeStruct((m, ncols), F32)],
        scratch=[pltpu.VMEM((k, tn), BF16)],
        vmem=2 * (_nbytes((tm, k), BF16) + _nbytes((k, tn), F32) + _nbytes((tm, tn), F32))
        + _nbytes((k, tn), BF16) + _nbytes((tm, tn), F32),
    )


def _in_projection(h2d, w_in, layer, col0, ncols):
    m = h2d.shape[0]
    tm = 1024
    ops = _in_proj_operands(h2d, w_in, layer, col0, ncols, tm, lambda j, i: (j, i))
    return pl.pallas_call(
        _in_proj_kernel,
        grid=(ncols // PROJ_TN, m // tm),
        in_specs=ops["in_specs"],
        out_specs=ops["out_specs"][0],
        out_shape=ops["out_shape"][0],
        scratch_shapes=ops["scratch"],
        compiler_params=_params(("arbitrary", "arbitrary"), ops["vmem"] + (4 << 20)),
        name="in_projection",
    )(*ops["args"])


def _causal_taps(scr, v, w_ref, taps, tt):
    scr[:, HALO:HALO + tt, :] = v
    acc = None
    for k in range(taps):
        src = v if k == taps - 1 else scr[:, HALO - (taps - 1) + k:HALO - (taps - 1) + k + tt, :]
        term = w_ref[k:k + 1, :] * src
        acc = term if acc is None else acc + term
    return acc


def _block_scan_real(a, b):
    row = lax.broadcasted_iota(jnp.int32, (1, SUBLANES, a.shape[-1]), 1)
    for d in (1, 2, 4):
        keep = row >= d
        a_sh = pltpu.roll(a, d, 1)
        b_sh = pltpu.roll(b, d, 1)
        b = b + a * jnp.where(keep, b_sh, 0.0)
        a = a * jnp.where(keep, a_sh, 1.0)
    return a, b


def _block_scan_cplx(br, bi, tab_ref, base, sl):
    for lvl, d in enumerate((1, 2, 4)):
        lr = tab_ref[0, base + 2 * lvl, :, sl]
        li = tab_ref[0, base + 2 * lvl + 1, :, sl]
        sr = pltpu.roll(br, d, 1)
        si = pltpu.roll(bi, d, 1)
        br, bi = br + (lr * sr - li * si), bi + (lr * si + li * sr)
    return br, bi


def _spread_rows(scr, nblk, bb):
    return jnp.concatenate(
        [jnp.broadcast_to(scr[:, k:k + 1, :], (bb, SUBLANES, LANES)) for k in range(nblk)],
        axis=1)


def _mixer_a_kernel(ab_ref, ac_ref, ax_ref, ag_ref, w_ref, b_ref, st_ref,
                    o_ref, st_out_ref, scr, *, tt):
    keep = CONV_A - 1

    @pl.when(pl.program_id(2) == 0)
    def _():
        scr[:, HALO - keep:HALO, :] = st_ref[...]

    conv_in = ac_ref[...] * ax_ref[...]
    y = b_ref[...] + _causal_taps(scr, conv_in, w_ref, CONV_A, tt)
    out = (ab_ref[...] * y) * jax.nn.silu(ag_ref[...])
    o_ref[...] = out.reshape(o_ref.shape).astype(o_ref.dtype)
    tail = scr[:, HALO + tt - keep:HALO + tt, :]
    st_out_ref[...] = tail
    scr[:, HALO - keep:HALO, :] = tail


def _mixer_a(proj3, w_conv, b_conv, state, bb, tt):
    b, t, _ = proj3.shape
    cw = CH_TILE
    nc = W_A // cw
    nt = t // tt
    blk = (bb, tt, cw)

    def col(off):
        return pl.BlockSpec(blk, lambda i, c, j, off=off: (i, j, off // cw + c))

    vmem = 2 * (4 * _nbytes(blk, F32) + _nbytes(blk, BF16)) \
        + _nbytes((bb, HALO + tt, cw), F32) + 6 * _nbytes(blk, F32)
    return pl.pallas_call(
        functools.partial(_mixer_a_kernel, tt=tt),
        grid=(b // bb, nc, nt),
        in_specs=[
            col(0), col(W_A), col(2 * W_A), col(3 * W_A),
            pl.BlockSpec((CONV_A, cw), lambda i, c, j: (0, c)),
            pl.BlockSpec((1, cw), lambda i, c, j: (0, c)),
            pl.BlockSpec((bb, CONV_A - 1, cw), lambda i, c, j: (i, 0, c)),
        ],
        out_specs=[
            pl.BlockSpec((bb * tt, cw), lambda i, c, j: (i * nt + j, c)),
            pl.BlockSpec((bb, CONV_A - 1, cw), lambda i, c, j: (i, 0, c)),
        ],
        out_shape=[
            jax.ShapeDtypeStruct((b * t, W_A), BF16),
            jax.ShapeDtypeStruct((b, CONV_A - 1, W_A), F32),
        ],
        scratch_shapes=[pltpu.VMEM((bb, HALO + tt, cw), F32)],
        compiler_params=_params(("arbitrary",) * 3, vmem),
        name="mixer_a",
    )(proj3, proj3, proj3, proj3, w_conv, b_conv.reshape(1, W_A), state)


MIXER_B_INS = 10
MIXER_B_OUTS = 3


def _mixer_b_init(cst_ref, hst_ref, scr, carry, first_time_tile):
    @pl.when(first_time_tile)
    def _():
        scr[:, HALO - (CONV_B - 1):HALO, :] = cst_ref[...]
        carry[...] = hst_ref[...]


def _mixer_b_kernel(*refs, bb, tt):
    ins = refs[:MIXER_B_INS]
    outs = refs[MIXER_B_INS:MIXER_B_INS + MIXER_B_OUTS]
    scr, carry, *scan_scr = refs[MIXER_B_INS + MIXER_B_OUTS:]
    _mixer_b_init(ins[8], ins[9], scr, carry, pl.program_id(2) == 0)
    _mixer_b_step(*ins[:8], *outs, scr, carry, scan_scr, bb=bb, tt=tt)


def _mixer_b_step(bx_ref, bg_ref, w_ref, b_ref, wg_ref, brg_ref, big_ref, lam_ref,
                  o_ref, cst_out_ref, hst_out_ref, scr, carry, scan_scr, *, bb, tt,
                  side_matmul=None):
    keep = CONV_B - 1
    cw = CH_TILE
    rows = bb * tt
    nblk = tt // SUBLANES
    nv = nblk // SUBLANES

    bx = bx_ref[...]
    xb = b_ref[...] + _causal_taps(scr, bx, w_ref, CONV_B, tt)
    tail = scr[:, HALO + tt - keep:HALO + tt, :]
    cst_out_ref[...] = tail
    scr[:, HALO - keep:HALO, :] = tail

    xb_bf16 = xb.reshape(rows, cw).astype(BF16)
    softplus_neg_lam = jax.nn.softplus(-lam_ref[...])
    cin_all = carry[...]
    h_parts, last_parts = [], []
    assert LRU_HD == LANES
    n_heads = cw // LRU_HD
    for s in range(n_heads):
        sl = slice(s * LANES, (s + 1) * LANES)
        w_head = jnp.concatenate([wg_ref[0, sl, sl], wg_ref[0, sl, cw + s * LANES:cw + (s + 1) * LANES]],
                                 axis=1)
        gates = jnp.dot(xb_bf16[:, sl], w_head, preferred_element_type=F32)
        if side_matmul is not None:
            side_matmul(s, n_heads)
        xs = xb[:, :, sl]
        r = jax.nn.sigmoid(gates[:, :LANES] + brg_ref[:, sl]).reshape(bb, tt, LANES)
        ig = jax.nn.sigmoid(gates[:, LANES:] + big_ref[:, sl]).reshape(bb, tt, LANES)
        log_a = (-LRU_C * r) * softplus_neg_lam[:, sl]
        a = jnp.exp(log_a)
        beta = jnp.sqrt(-jnp.tanh(log_a) * (a * a + 1.0))
        bt = (beta * ig) * xs
        cin = cin_all[:, :, sl]
        a_blk, b_blk = _block_scan_real(a.reshape(bb * nblk, SUBLANES, LANES),
                                        bt.reshape(bb * nblk, SUBLANES, LANES))
        if nblk == 1:
            h = b_blk + a_blk * cin
            last = h[:, SUBLANES - 1:SUBLANES, :]
        else:
            acum, bcum, cscr = scan_scr
            acum[s] = a_blk.reshape(bb, tt, LANES)
            bcum[s] = b_blk.reshape(bb, tt, LANES)
            a2 = acum[s, :, pl.ds(SUBLANES - 1, nblk, stride=SUBLANES), :]
            b2 = bcum[s, :, pl.ds(SUBLANES - 1, nblk, stride=SUBLANES), :]
            a2, b2 = _block_scan_real(a2.reshape(bb * nv, SUBLANES, LANES),
                                      b2.reshape(bb * nv, SUBLANES, LANES))
            a2 = a2.reshape(bb, nblk, LANES)
            b2 = b2.reshape(bb, nblk, LANES)
            prev = cin
            ends = []
            for j in range(nv):
                seg = slice(j * SUBLANES, (j + 1) * SUBLANES)
                e = b2[:, seg, :] + a2[:, seg, :] * prev
                ends.append(e)
                prev = e[:, SUBLANES - 1:SUBLANES, :]
            last = prev
            cscr[s, :, 0:1, :] = cin
            cscr[s, :, 1:1 + nblk, :] = jnp.concatenate(ends, axis=1)
            spread = _spread_rows(cscr.at[s], nblk, bb)
            h = b_blk + a_blk * spread.reshape(bb * nblk, SUBLANES, LANES)
        h_parts.append(h.reshape(bb, tt, LANES))
        last_parts.append(last)
    h = jnp.concatenate(h_parts, axis=-1)
    last = jnp.concatenate(last_parts, axis=-1)
    carry[...] = last
    hst_out_ref[...] = last
    out = h * jax.nn.silu(bg_ref[...])
    o_ref[...] = out.reshape(o_ref.shape).astype(o_ref.dtype)


def _mixer_b_operands(proj3, w_conv, b_conv, w_gates, b_rg, b_ig, lam, conv_state, h_state,
                      bb, tt, idx):
    b, t, _ = proj3.shape
    cw = CH_TILE
    nt = t // tt
    nblk = tt // SUBLANES
    blk = (bb, tt, cw)

    def spec(shape, fn):
        return pl.BlockSpec(shape, lambda *g: fn(*idx(*g)))

    vec = spec((1, cw), lambda i, c, j: (0, c))
    scratch = [pltpu.VMEM((bb, HALO + tt, cw), F32), pltpu.VMEM((bb, 1, cw), F32)]
    if nblk > 1:
        assert nblk % SUBLANES == 0
        slabs = cw // LANES
        scratch += [pltpu.VMEM((slabs, bb, tt, LANES), F32),
                    pltpu.VMEM((slabs, bb, tt, LANES), F32),
                    pltpu.VMEM((slabs, bb, HALO + nblk, LANES), F32)]
    return dict(
        args=[proj3, proj3, w_conv, b_conv.reshape(1, W_B), w_gates,
              b_rg.reshape(1, W_B), b_ig.reshape(1, W_B), lam.reshape(1, W_B),
              conv_state, h_state],
        in_specs=[
            spec(blk, lambda i, c, j: (i, j, c)),
            spec(blk, lambda i, c, j: (i, j, W_B // cw + c)),
            spec((CONV_B, cw), lambda i, c, j: (0, c)),
            vec,
            spec((1, cw, 2 * cw), lambda i, c, j: (c, 0, 0)),
            vec, vec, vec,
            spec((bb, CONV_B - 1, cw), lambda i, c, j: (i, 0, c)),
            spec((bb, 1, cw), lambda i, c, j: (i, 0, c)),
        ],
        out_specs=[
            spec((bb * tt, cw), lambda i, c, j: (i * nt + j, c)),
            spec((bb, CONV_B - 1, cw), lambda i, c, j: (i, 0, c)),
            spec((bb, 1, cw), lambda i, c, j: (i, 0, c)),
        ],
        out_shape=[
            jax.ShapeDtypeStruct((b * t, W_B), BF16),
            jax.ShapeDtypeStruct((b, CONV_B - 1, W_B), F32),
            jax.ShapeDtypeStruct((b, 1, W_B), F32),
        ],
        scratch=scratch,
        vmem=2 * (2 * _nbytes(blk, F32) + _nbytes(blk, BF16) + _nbytes((cw, 2 * cw), BF16))
        + _nbytes((bb, HALO + tt, cw), F32) + 16 * _nbytes(blk, F32),
        grid=(b // bb, W_B // cw, nt),
    )


def _mixer_b(proj3, *params, bb, tt):
    ops = _mixer_b_operands(proj3, *params, bb, tt, lambda i, c, j: (i, c, j))
    return pl.pallas_call(
        functools.partial(_mixer_b_kernel, bb=bb, tt=tt),
        grid=ops["grid"],
        in_specs=ops["in_specs"],
        out_specs=ops["out_specs"],
        out_shape=ops["out_shape"],
        scratch_shapes=ops["scratch"],
        compiler_params=_params(("arbitrary",) * 3, ops["vmem"]),
        name="mixer_b",
    )(*ops["args"])


def _fused_proj_kernel(*refs, mixers, ni):
    a_ref, w_ref = refs[:2]
    mixers = [dict(mx) for mx in mixers]
    pos = 2
    for mx in mixers:
        mx["ins"] = refs[pos:pos + mx["n_ins"]]
        pos += mx["n_ins"]
    proj_ref = refs[pos]
    pos += 1
    for mx in mixers:
        mx["outs"] = refs[pos:pos + mx["n_outs"]]
        pos += mx["n_outs"]
    wb_ref = refs[pos]
    pos += 1
    for mx in mixers:
        mx["scr"] = refs[pos:pos + mx["n_scr"]]
        pos += mx["n_scr"]
    s = pl.program_id(0)
    _cast_weight_tile(w_ref, wb_ref, lax.rem(s, ni) == 0)
    for mx in mixers:
        mx["init"](mx["ins"], mx["scr"], s)
    k_total = a_ref.shape[1]
    acc = []

    def projection_part(part, n_parts):
        kc = MXU_DIM
        per_part = k_total // (kc * n_parts)
        for sub in range(part * per_part, (part + 1) * per_part):
            ks = slice(sub * kc, (sub + 1) * kc)
            term = jnp.dot(a_ref[:, ks], wb_ref[ks, :], preferred_element_type=F32)
            acc[:] = [term if not acc else acc[0] + term]
        if part == n_parts - 1:
            proj_ref[...] = acc[0]

    for mx in mixers[:-1]:
        mx["step"](mx["ins"], mx["outs"], mx["scr"], s, None)
    mx = mixers[-1]
    mx["step"](mx["ins"], mx["outs"], mx["scr"], s, projection_part)


def _fused_proj_call(name, h2d, w_in, layer, col0, ncols, mixers):
    steps = None
    for ops, _, _ in mixers:
        n = 1
        for g in ops["grid"]:
            n *= g
        assert steps in (None, n)
        steps = n
    nj = ncols // PROJ_TN
    ni = steps // nj
    tm = h2d.shape[0] // ni
    assert nj * ni == steps and tm * ni == h2d.shape[0] and tm % SUBLANES == 0
    mm = _in_proj_operands(h2d, w_in, layer, col0, ncols, tm, lambda s: (s // ni, s % ni))
    specs = [dict(n_ins=len(ops["args"]), n_outs=len(ops["out_shape"]),
                  n_scr=len(ops["scratch"]), init=init, step=step)
             for ops, init, step in mixers]
    cat = lambda key: [x for ops, _, _ in mixers for x in ops[key]]
    return pl.pallas_call(
        functools.partial(_fused_proj_kernel, mixers=specs, ni=ni),
        grid=(steps,),
        in_specs=mm["in_specs"] + cat("in_specs"),
        out_specs=mm["out_specs"] + cat("out_specs"),
        out_shape=mm["out_shape"] + cat("out_shape"),
        scratch_shapes=mm["scratch"] + cat("scratch"),
        compiler_params=_params(("arbitrary",),
                                mm["vmem"] + sum(ops["vmem"] for ops, _, _ in mixers)),
        name=name,
    )(*mm["args"], *cat("args"))


def _flat_index(dims):
    def idx(s):
        out = []
        for k, d in enumerate(dims):
            stride = 1
            for e in dims[k + 1:]:
                stride *= e
            out.append((s // stride) % d if k else s // stride)
        return tuple(out)
    return idx


def _flat_mixer_b(proj3, *params, bb, tt):
    b, t, _ = proj3.shape
    dims = (b // bb, W_B // CH_TILE, t // tt)
    ops = _mixer_b_operands(proj3, *params, bb, tt, _flat_index(dims))
    nt = dims[2]

    def init(ins, scr, s):
        _mixer_b_init(ins[8], ins[9], scr[0], scr[1], lax.rem(s, nt) == 0)

    def step(ins, outs, scr, s, side_matmul):
        _mixer_b_step(*ins[:8], *outs, scr[0], scr[1], scr[2:], bb=bb, tt=tt,
                      side_matmul=side_matmul)

    return ops, init, step


def _s5_disc_kernel(lre_ref, lim_ref, ldt_ref, btre_ref, btim_ref,
                    abre_ref, abim_ref, bbre_ref, bbim_ref):
    lre = lre_ref[0]
    lim = lim_ref[0]
    dt = jnp.exp(ldt_ref[0])
    mag = jnp.exp(lre * dt)
    ab_re = mag * jnp.cos(lim * dt)
    ab_im = mag * jnp.sin(lim * dt)
    nr = ab_re - 1.0
    den = lre * lre + lim * lim
    fr = (nr * lre + ab_im * lim) / den
    fi = (ab_im * lre - nr * lim) / den
    bre = btre_ref[0]
    bim = btim_ref[0]
    bbre_ref[0] = fr * bre - fi * bim
    bbim_ref[0] = fr * bim + fi * bre
    abre_ref[0] = ab_re
    abim_ref[0] = ab_im


def _s5_discretize(lam_re, lam_im, log_dt, b_re, b_im):
    depth = lam_re.shape[0]
    gp = (depth, S5_G, 1, S5_P)
    ldt = jnp.broadcast_to(log_dt[:, :, None, None], gp)
    bt_shape = (depth, S5_G, S5_GROUP, S5_P)
    small = pl.BlockSpec((1, S5_G, 1, S5_P), lambda l: (l, 0, 0, 0))
    big = pl.BlockSpec((1, S5_G, S5_GROUP, S5_P), lambda l: (l, 0, 0, 0))
    ab_re, ab_im, bb_re, bb_im = pl.pallas_call(
        _s5_disc_kernel,
        grid=(depth,),
        in_specs=[small, small, small, big, big],
        out_specs=[small, small, big, big],
        out_shape=[jax.ShapeDtypeStruct(gp, F32), jax.ShapeDtypeStruct(gp, F32),
                   jax.ShapeDtypeStruct(bt_shape, F32), jax.ShapeDtypeStruct(bt_shape, F32)],
        name="s5_discretize",
    )(lam_re.reshape(gp), lam_im.reshape(gp), ldt,
      jnp.swapaxes(b_re, 2, 3), jnp.swapaxes(b_im, 2, 3))
    return (ab_re.reshape(depth, 1, S5_STATES), ab_im.reshape(depth, 1, S5_STATES),
            bb_re, bb_im)


def _s5_pow_kernel(ar_ref, ai_ref, tab_ref):
    def cmul(x, y):
        return x[0] * y[0] - x[1] * y[1], x[0] * y[1] + x[1] * y[0]

    def powers(b1):
        b2 = cmul(b1, b1)
        b3 = cmul(b2, b1)
        b4 = cmul(b2, b2)
        return (b1, b2, b3, b4, cmul(b4, b1), cmul(b4, b2), cmul(b4, b3), cmul(b4, b4))

    shape = (SUBLANES, S5_STATES)
    row = lax.broadcasted_iota(jnp.int32, shape, 0)
    zero = jnp.zeros(shape, F32)
    pw_a = powers((ar_ref[0], ai_ref[0]))
    pw_a8 = powers(pw_a[SUBLANES - 1])
    for base, pw in ((0, pw_a), (N_TABLES // 2, pw_a8)):
        for part in (0, 1):
            tab_ref[0, base + part] = jnp.where(row >= 1, pw[0][part], zero)
            tab_ref[0, base + 2 + part] = jnp.where(row >= 2, pw[1][part], zero)
            tab_ref[0, base + 4 + part] = jnp.where(row >= 4, pw[3][part], zero)
            p = zero
            for r, v in enumerate(pw):
                p = jnp.where(row == r, v[part], p)
            tab_ref[0, base + 6 + part] = p


def _s5_power_tables(ab_re, ab_im):
    depth = ab_re.shape[0]
    vec = pl.BlockSpec((1, 1, S5_STATES), lambda l: (l, 0, 0))
    return pl.pallas_call(
        _s5_pow_kernel,
        grid=(depth,),
        in_specs=[vec, vec],
        out_specs=pl.BlockSpec((1, N_TABLES, SUBLANES, S5_STATES), lambda l: (l, 0, 0, 0)),
        out_shape=jax.ShapeDtypeStruct((depth, N_TABLES, SUBLANES, S5_STATES), F32),
        name="s5_power_tables",
    )(ab_re, ab_im)


def _s5_block_matrices(bb_re, bb_im, c_re, c_im):
    depth = bb_re.shape[0]
    gpt = CH_TILE // S5_GROUP
    chan = lax.broadcasted_iota(jnp.int32, (S5_Q, CH_TILE, CH_TILE), 1)
    state = lax.broadcasted_iota(jnp.int32, (S5_Q, CH_TILE, CH_TILE), 2)
    slab = lax.broadcasted_iota(jnp.int32, (S5_Q, CH_TILE, CH_TILE), 0)
    gps = LANES // S5_P
    keep = (chan // S5_GROUP) == gps * slab + (state % LANES) // S5_P

    def dense_b(x):
        return x.reshape(depth * S5_KT, gpt * S5_GROUP, S5_P)

    db = jnp.concatenate([dense_b(bb_re)] * gps + [dense_b(bb_im)] * gps, axis=-1)
    bm = jnp.where(keep[None], db[:, None], 0.0).astype(BF16)

    def dense_c(x):
        x = x.reshape(depth * S5_KT, gpt, S5_GROUP, S5_P)
        return jnp.transpose(x, (0, 3, 1, 2)).reshape(depth * S5_KT, S5_P, gpt * S5_GROUP)

    dc = jnp.concatenate([dense_c(c_re)] * gps + [dense_c(-c_im)] * gps, axis=1)
    keep_c = jnp.swapaxes(keep, 1, 2)
    cm = jnp.where(keep_c[None], dc[:, None], 0.0).astype(BF16)
    return bm, cm


MIXER_C_INS = 10
MIXER_C_OUTS = 3


def _mixer_c_init(sre_ref, sim_ref, car_r, car_i, kt, first_time_tile):
    @pl.when(first_time_tile)
    def _():
        car_r[kt] = sre_ref[...]
        car_i[kt] = sim_ref[...]


def _mixer_c_kernel(*refs, bb, tt):
    ins = refs[:MIXER_C_INS]
    outs = refs[MIXER_C_INS:MIXER_C_INS + MIXER_C_OUTS]
    ystore, car_r, car_i, *scan_scr = refs[MIXER_C_INS + MIXER_C_OUTS:]
    kt = pl.program_id(2)
    _mixer_c_init(ins[8], ins[9], car_r, car_i, kt, pl.program_id(1) == 0)
    _mixer_c_step(*ins[:8], *outs, ystore, car_r, car_i, scan_scr, kt, bb=bb, tt=tt)


def _mixer_c_step(cu_ref, cg_ref, bm_ref, cm_ref, tab_ref, d_ref, wglu_ref, bglu_ref,
                  o_ref, sre_out_ref, sim_out_ref,
                  ystore, car_r, car_i, scan_scr, kt, *, bb, tt, side_matmul=None):
    cw = CH_TILE
    rows = bb * tt
    nblk = tt // SUBLANES
    nv = nblk // SUBLANES
    n = bb * nblk
    base8 = N_TABLES // 2

    u = cu_ref[...].reshape(rows, cw)
    ub = u.astype(BF16)
    y = d_ref[...] * u
    cr_all = car_r[kt]
    ci_all = car_i[kt]
    new_r, new_i = [], []
    for q in range(S5_Q):
        sl = slice(LANES * q, LANES * (q + 1))
        bu = jnp.dot(ub, bm_ref[0, q], preferred_element_type=F32)
        if side_matmul is not None:
            side_matmul(q, S5_Q)
        br, bi = _block_scan_cplx(bu[:, :LANES].reshape(n, SUBLANES, LANES),
                                  bu[:, LANES:].reshape(n, SUBLANES, LANES), tab_ref, 0, sl)
        p_r = tab_ref[0, 6, :, sl]
        p_i = tab_ref[0, 7, :, sl]
        cr = cr_all[:, :, sl]
        ci = ci_all[:, :, sl]
        if nblk == 1:
            kr, ki = cr, ci
        else:
            hsr, hsi, csr, csi = scan_scr
            hsr[...] = br.reshape(bb, tt, LANES)
            hsi[...] = bi.reshape(bb, tt, LANES)
            xr = hsr[:, pl.ds(SUBLANES - 1, nblk, stride=SUBLANES), :]
            xi = hsi[:, pl.ds(SUBLANES - 1, nblk, stride=SUBLANES), :]
            xr, xi = _block_scan_cplx(xr.reshape(bb * nv, SUBLANES, LANES),
                                      xi.reshape(bb * nv, SUBLANES, LANES), tab_ref, base8, sl)
            xr = xr.reshape(bb, nblk, LANES)
            xi = xi.reshape(bb, nblk, LANES)
            q_r = tab_ref[0, base8 + 6, :, sl]
            q_i = tab_ref[0, base8 + 7, :, sl]
            prev_r, prev_i = cr, ci
            ends_r, ends_i = [], []
            for j in range(nv):
                seg = slice(j * SUBLANES, (j + 1) * SUBLANES)
                er = xr[:, seg, :] + (q_r * prev_r - q_i * prev_i)
                ei = xi[:, seg, :] + (q_r * prev_i + q_i * prev_r)
                ends_r.append(er)
                ends_i.append(ei)
                prev_r = er[:, SUBLANES - 1:SUBLANES, :]
                prev_i = ei[:, SUBLANES - 1:SUBLANES, :]
            csr[:, 0:1, :] = cr
            csi[:, 0:1, :] = ci
            csr[:, 1:1 + nblk, :] = jnp.concatenate(ends_r, axis=1)
            csi[:, 1:1 + nblk, :] = jnp.concatenate(ends_i, axis=1)
            kr = _spread_rows(csr, nblk, bb).reshape(n, SUBLANES, LANES)
            ki = _spread_rows(csi, nblk, bb).reshape(n, SUBLANES, LANES)
        hr = br + (p_r * kr - p_i * ki)
        hi = bi + (p_r * ki + p_i * kr)
        hr = hr.reshape(bb, tt, LANES)
        hi = hi.reshape(bb, tt, LANES)
        new_r.append(hr[:, tt - 1:tt, :])
        new_i.append(hi[:, tt - 1:tt, :])
        hcat = jnp.concatenate([hr.reshape(rows, LANES), hi.reshape(rows, LANES)], axis=-1)
        y = y + jnp.dot(hcat.astype(BF16), cm_ref[0, q], preferred_element_type=F32)

    ncr_all = jnp.concatenate(new_r, axis=-1)
    nci_all = jnp.concatenate(new_i, axis=-1)
    car_r[kt] = ncr_all
    car_i[kt] = nci_all
    sre_out_ref[:, kt] = ncr_all
    sim_out_ref[:, kt] = nci_all
    ystore[kt] = y

    @pl.when(kt == S5_KT - 1)
    def _():
        y_all = jnp.concatenate([ystore[k] for k in range(S5_KT)], axis=-1)
        yg = jax.nn.gelu(y_all)
        z = jnp.dot(yg.astype(BF16), wglu_ref[0].astype(BF16),
                    preferred_element_type=F32) + bglu_ref[...]
        yy = yg * jax.nn.sigmoid(z)
        out = yy * jax.nn.silu(cg_ref[...].reshape(rows, W_C))
        o_ref[...] = out.astype(o_ref.dtype)


def _mixer_c_operands(proj3, bmat, cmat, tables, d_skip, w_glu, b_glu, s_re, s_im, layer,
                      bb, tt, idx):
    b, t, _ = proj3.shape
    u_col = 0
    cw = CH_TILE
    nt = t // tt
    nblk = tt // SUBLANES
    kslab = S5_Q * LANES

    def spec(shape, fn):
        return pl.BlockSpec(shape, lambda *g: fn(*idx(*g)))

    mat = spec((1, S5_Q, cw, cw), lambda i, j, k: (layer * S5_KT + k, 0, 0, 0))
    st = spec((bb, 1, kslab), lambda i, j, k: (i, 0, k))
    st_out = spec((bb, S5_KT, 1, kslab), lambda i, j, k: (i, 0, 0, 0))
    rows = bb * tt
    scratch = [pltpu.VMEM((S5_KT, rows, cw), F32),
               pltpu.VMEM((S5_KT, bb, 1, kslab), F32),
               pltpu.VMEM((S5_KT, bb, 1, kslab), F32)]
    if nblk > 1:
        assert nblk % SUBLANES == 0 and nblk // SUBLANES <= SUBLANES
        scratch += [pltpu.VMEM((bb, tt, LANES), F32), pltpu.VMEM((bb, tt, LANES), F32),
                    pltpu.VMEM((bb, HALO + nblk, LANES), F32),
                    pltpu.VMEM((bb, HALO + nblk, LANES), F32)]
    vmem = 2 * (_nbytes((bb, tt, cw), F32) + _nbytes((bb, tt, W_C), F32)
                + 2 * _nbytes((S5_Q, cw, cw), BF16)
                + _nbytes((N_TABLES, SUBLANES, kslab), F32)
                + _nbytes((W_C, W_C), F32) + _nbytes((bb, tt, W_C), BF16)
                + 4 * _nbytes((bb, SUBLANES, kslab), F32)) \
        + _nbytes((S5_KT, rows, cw), F32) + 2 * _nbytes((bb, tt, LANES), F32) \
        + 2 * _nbytes((S5_KT, bb, SUBLANES, kslab), F32) + 10 * _nbytes((rows, W_C), F32)
    return dict(
        args=[proj3, proj3, bmat, cmat, tables, d_skip.reshape(1, W_C), w_glu,
              b_glu.reshape(1, W_C), s_re, s_im],
        in_specs=[
            spec((bb, tt, cw), lambda i, j, k: (i, j, u_col // cw + k)),
            spec((bb, tt, W_C), lambda i, j, k: (i, j, u_col // W_C + 1)),
            mat, mat,
            spec((1, N_TABLES, SUBLANES, kslab), lambda i, j, k: (layer, 0, 0, k)),
            spec((1, cw), lambda i, j, k: (0, k)),
            spec((1, W_C, W_C), lambda i, j, k: (layer, 0, 0)),
            spec((1, W_C), lambda i, j, k: (0, 0)),
            st, st,
        ],
        out_specs=[
            spec((rows, W_C), lambda i, j, k: (i * nt + j, 0)),
            st_out, st_out,
        ],
        out_shape=[
            jax.ShapeDtypeStruct((b * t, W_C), BF16),
            jax.ShapeDtypeStruct((b, S5_KT, 1, kslab), F32),
            jax.ShapeDtypeStruct((b, S5_KT, 1, kslab), F32),
        ],
        scratch=scratch,
        vmem=vmem,
        grid=(b // bb, nt, S5_KT),
    )


def _mixer_c(proj3, *params, bb, tt):
    ops = _mixer_c_operands(proj3, *params, bb, tt, lambda i, j, k: (i, j, k))
    return pl.pallas_call(
        functools.partial(_mixer_c_kernel, bb=bb, tt=tt),
        grid=ops["grid"],
        in_specs=ops["in_specs"],
        out_specs=ops["out_specs"],
        out_shape=ops["out_shape"],
        scratch_shapes=ops["scratch"],
        compiler_params=_params(("arbitrary",) * 3, ops["vmem"]),
        name="mixer_c",
    )(*ops["args"])


def _flat_mixer_c(proj3, *params, bb, tt):
    b, t, _ = proj3.shape
    dims = (b // bb, t // tt, S5_KT)
    ops = _mixer_c_operands(proj3, *params, bb, tt, _flat_index(dims))
    nt = dims[1]

    def init(ins, scr, s):
        _mixer_c_init(ins[8], ins[9], scr[1], scr[2], lax.rem(s, S5_KT),
                      lax.rem(s // S5_KT, nt) == 0)

    def step(ins, outs, scr, s, side_matmul):
        _mixer_c_step(*ins[:8], *outs, scr[0], scr[1], scr[2], scr[3:],
                      lax.rem(s, S5_KT), bb=bb, tt=tt, side_matmul=side_matmul)

    return ops, init, step


def _out_proj_kernel(ma_ref, mb_ref, mc_ref, w_ref, x_ref, gate_ref, o_ref, *, bb, tt):
    w = w_ref[0].astype(BF16)
    acc = jnp.dot(ma_ref[...], w[0:W_A, :], preferred_element_type=F32)
    acc = acc + jnp.dot(mb_ref[...], w[W_A:W_A + W_B, :], preferred_element_type=F32)
    acc = acc + jnp.dot(mc_ref[...], w[W_A + W_B:, :], preferred_element_type=F32)
    o_ref[...] = x_ref[...] + gate_ref[...] * acc.reshape(bb, tt, acc.shape[-1])


def _out_projection(out_a, out_b, out_c, w_out, x, gate, layer, bb, tt):
    b, t, d = x.shape
    tm = bb * tt
    tn = 512
    nt = t // tt
    m = b * t

    def rows(width):
        return pl.BlockSpec((tm, width), lambda i, j: (i, 0))

    vmem = 2 * (_nbytes((tm, d), BF16) + _nbytes((d, tn), F32)
                + 2 * _nbytes((tm, tn), F32)) + _nbytes((d, tn), BF16) \
        + 3 * _nbytes((tm, tn), F32)
    return pl.pallas_call(
        functools.partial(_out_proj_kernel, bb=bb, tt=tt),
        grid=(m // tm, d // tn),
        in_specs=[
            rows(W_A), rows(W_B), rows(W_C),
            pl.BlockSpec((1, d, tn), lambda i, j: (layer, 0, j)),
            pl.BlockSpec((bb, tt, tn), lambda i, j: (i // nt, i % nt, j)),
            pl.BlockSpec((bb, 1, tn), lambda i, j: (i // nt, 0, j)),
        ],
        out_specs=pl.BlockSpec((bb, tt, tn), lambda i, j: (i // nt, i % nt, j)),
        out_shape=jax.ShapeDtypeStruct((b, t, d), F32),
        compiler_params=_params(("arbitrary", "arbitrary"), vmem + (4 << 20)),
        name="out_projection",
    )(out_a, out_b, out_c, w_out, x, gate)


def _gate_weights(w_rg, w_ig):
    hpt = CH_TILE // LRU_HD
    pairs = LRU_HEADS // hpt
    w = jnp.stack([w_rg, w_ig], axis=2)
    w = w.reshape(pairs, hpt, LRU_HD, 2, LRU_HD)
    eye = jnp.eye(hpt, dtype=bool)
    out = jnp.where(eye[None, :, None, None, :, None],
                    w[:, :, :, :, None, :], 0.0)
    return out.reshape(pairs, CH_TILE, 2 * CH_TILE).astype(BF16)


def _layer(x, mod, states, wts, layer, tiles):
    b, t, d = x.shape
    conv_a, conv_b, lru_h, s5_re, s5_im = states
    shift, scale, gate = (mod[:, None, k * d:(k + 1) * d] for k in range(3))

    h = _norm_modulate(x, scale, shift, wts["g_norm"], *tiles["norm"])
    w_in = wts["w_in"]
    col_a, col_b, col_c = 0, 4 * W_A, 4 * W_A + 2 * W_B
    b_params = (wts["w_conv_b"], wts["b_conv_b"], wts["w_gates"], wts["b_rg"], wts["b_ig"],
                wts["lru_lambda"], conv_b, lru_h.reshape(b, 1, W_B))
    c_params = (wts["bmat"], wts["cmat"], wts["tables"], wts["s5_d"], wts["w_glu"],
                wts["b_glu"], s5_re.reshape(b, 1, S5_STATES), s5_im.reshape(b, 1, S5_STATES),
                layer)
    bb_b, tt_b = tiles["b"]
    bb_c, tt_c = tiles["c"]

    proj_c = _in_projection(h, w_in, layer, col_c, 2 * W_C).reshape(b, t, 2 * W_C)
    if tiles["fuse"]:
        proj_b, out_c, re_new, im_new = _fused_proj_call(
            "proj_b_mixer_c", h, w_in, layer, col_b, 2 * W_B,
            [_flat_mixer_c(proj_c, *c_params, bb=bb_c, tt=tt_c)])
        proj_b = proj_b.reshape(b, t, 2 * W_B)
        proj_a, out_b, conv_b_new, lru_new = _fused_proj_call(
            "proj_a_mixer_b", h, w_in, layer, col_a, 4 * W_A,
            [_flat_mixer_b(proj_b, *b_params, bb=bb_b, tt=tt_b)])
    else:
        out_c, re_new, im_new = _mixer_c(proj_c, *c_params, bb=bb_c, tt=tt_c)
        proj_b = _in_projection(h, w_in, layer, col_b, 2 * W_B).reshape(b, t, 2 * W_B)
        out_b, conv_b_new, lru_new = _mixer_b(proj_b, *b_params, bb=bb_b, tt=tt_b)
        proj_a = _in_projection(h, w_in, layer, col_a, 4 * W_A)
    proj_a = proj_a.reshape(b, t, 4 * W_A)
    out_a, conv_a_new = _mixer_a(proj_a, wts["w_conv_a"], wts["b_conv_a"], conv_a,
                                 *tiles["a"])

    x_new = _out_projection(out_a, out_b, out_c, wts["w_out"], x, gate, layer, *tiles["out"])
    return (x_new, conv_a_new, conv_b_new, lru_new.reshape(b, W_B),
            re_new.reshape(b, S5_G, S5_P), im_new.reshape(b, S5_G, S5_P))


PROMPT_TILES = {"norm": (1, 256), "a": (1, 1024), "b": (1, 1024), "c": (1, 256),
                "out": (1, 1024), "final": (1, 256), "fuse": True}
SAMPLE_TILES = {"norm": (32, 8), "a": (128, 8), "b": (64, 8), "c": (32, 8),
                "out": (128, 8), "final": (32, 8), "fuse": True}


def kernel(x_prompt, x_sample, c_prompt, c_sample, state_conv_a, state_conv_b, state_lru_h,
           state_s5_re, state_s5_im, g_norm, w_ada, b_ada, w_in, w_conv_a, b_conv_a,
           w_conv_b, b_conv_b, w_rg, b_rg, w_ig, b_ig, lru_lambda, s5_lambda_re,
           s5_lambda_im, s5_log_dt, s5_b_re, s5_b_im, s5_c_re, s5_c_im, s5_d, w_glu,
           b_glu, w_out, g_final):
    depth = w_in.shape[0]
    bp = x_prompt.shape[0]
    bs = x_sample.shape[0]

    c_all = jnp.concatenate([c_prompt, c_sample], axis=0)
    pad = (-c_all.shape[0]) % SUBLANES
    c_all = jnp.pad(c_all, ((0, pad), (0, 0)))
    mod = _modulation(c_all, w_ada, b_ada)

    ab_re, ab_im, bb_re, bb_im = _s5_discretize(s5_lambda_re, s5_lambda_im, s5_log_dt,
                                               s5_b_re, s5_b_im)
    tables = _s5_power_tables(ab_re, ab_im)
    bmat, cmat = _s5_block_matrices(bb_re, bb_im, s5_c_re, s5_c_im)
    w_glu_bf16 = w_glu.astype(BF16)

    xp, xs = x_prompt, x_sample
    zeros_p = (jnp.zeros((bp, CONV_A - 1, W_A), F32), jnp.zeros((bp, CONV_B - 1, W_B), F32),
               jnp.zeros((bp, W_B), F32), jnp.zeros((bp, S5_G, S5_P), F32),
               jnp.zeros((bp, S5_G, S5_P), F32))
    outs_p, outs_s = [], []
    for l in range(depth):
        wts = {
            "g_norm": g_norm[l], "w_in": w_in,
            "w_conv_a": w_conv_a[l], "b_conv_a": b_conv_a[l],
            "w_conv_b": w_conv_b[l], "b_conv_b": b_conv_b[l],
            "w_gates": _gate_weights(w_rg[l], w_ig[l]), "b_rg": b_rg[l], "b_ig": b_ig[l],
            "lru_lambda": lru_lambda[l], "bmat": bmat, "cmat": cmat,
            "tables": tables, "s5_d": s5_d[l],
            "w_glu": w_glu_bf16, "b_glu": b_glu[l], "w_out": w_out,
        }
        res_p = _layer(xp, mod[l, :bp], zeros_p, wts, l, PROMPT_TILES)
        xp = res_p[0]
        outs_p.append(res_p[1:])
        st_s = (state_conv_a[l], state_conv_b[l], state_lru_h[l], state_s5_re[l],
                state_s5_im[l])
        res_s = _layer(xs, mod[l, bp:bp + bs], st_s, wts, l, SAMPLE_TILES)
        xs = res_s[0]
        outs_s.append(res_s[1:])

    y_prompt = _final_norm(xp, g_final, *PROMPT_TILES["final"])
    y_sample = _final_norm(xs, g_final, *SAMPLE_TILES["final"])
    stack = lambda outs, k: jnp.stack([o[k] for o in outs])
    return (y_prompt, y_sample,
            *(stack(outs_p, k) for k in range(5)),
            *(stack(outs_s, k) for k in range(5)))
```

```python
import functools

import jax
import jax.numpy as jnp
from jax import lax
from jax.experimental import pallas as pl
from jax.experimental.pallas import tpu as pltpu

F32 = jnp.float32
BF16 = jnp.bfloat16

D_MODEL = 4096
W_A = D_MODEL // 4
W_B = D_MODEL // 2
W_C = D_MODEL // 4
IN_COLS = 4 * W_A + 2 * W_B + 2 * W_C
CONV_A = 3
CONV_B = 4
LRU_HEADS = 16
LRU_HD = W_B // LRU_HEADS
LRU_C = 8.0
S5_GROUP = 16
S5_G = W_C // S5_GROUP
S5_P = 64
S5_STATES = S5_G * S5_P
EPS = 1e-6

SUBLANES = 8
LANES = 128
MXU_DIM = 256
VMEM_LIMIT_CAP = 60000 * 1024
CH_TILE = MXU_DIM
S5_KT = W_C // CH_TILE
S5_Q = (CH_TILE // S5_GROUP) * S5_P // LANES
HALO = SUBLANES
N_TABLES = 16


def _params(sem, vmem_bytes):
    return pltpu.CompilerParams(
        dimension_semantics=sem,
        vmem_limit_bytes=int(min(VMEM_LIMIT_CAP, vmem_bytes)))


def _nbytes(shape, dtype):
    n = 1
    for s in shape:
        n *= s
    return n * jnp.dtype(dtype).itemsize


def _mod_kernel(c_ref, w_ref, b_ref, o_ref):
    c = c_ref[...].astype(BF16)
    w = w_ref[0].astype(BF16)
    o_ref[0] = jnp.dot(c, w, preferred_element_type=F32) + b_ref[0]


def _modulation(c_all, w_ada, b_ada):
    depth, d, n = w_ada.shape
    rows = c_all.shape[0]
    tn = 512
    vmem = 2 * (_nbytes((rows, d), F32) + _nbytes((d, tn), F32)
                + _nbytes((rows, tn), F32)) + _nbytes((d, tn), F32)
    return pl.pallas_call(
        _mod_kernel,
        grid=(depth, n // tn),
        in_specs=[
            pl.BlockSpec((rows, d), lambda l, j: (0, 0)),
            pl.BlockSpec((1, d, tn), lambda l, j: (l, 0, j)),
            pl.BlockSpec((1, 1, tn), lambda l, j: (l, 0, j)),
        ],
        out_specs=pl.BlockSpec((1, rows, tn), lambda l, j: (l, 0, j)),
        out_shape=jax.ShapeDtypeStruct((depth, rows, n), F32),
        compiler_params=_params(("arbitrary", "arbitrary"), vmem + (8 << 20)),
        name="adaln_modulation",
    )(c_all, w_ada, b_ada.reshape(depth, 1, n))


def _norm_mod_kernel(x_ref, scale_ref, shift_ref, g_ref, o_ref):
    x = x_ref[...]
    ms = jnp.mean(x * x, axis=-1, keepdims=True)
    y = (x * lax.rsqrt(ms + EPS)) * g_ref[...]
    h = y * (1.0 + scale_ref[...]) + shift_ref[...]
    o_ref[...] = h.reshape(o_ref.shape).astype(o_ref.dtype)


def _norm_modulate(x, scale, shift, g, bb, tt):
    b, t, d = x.shape
    nt = t // tt
    vmem = 2 * (_nbytes((bb, tt, d), F32) + _nbytes((bb, tt, d), BF16)) \
        + 4 * _nbytes((bb, tt, d), F32)
    return pl.pallas_call(
        _norm_mod_kernel,
        grid=(b // bb, nt),
        in_specs=[
            pl.BlockSpec((bb, tt, d), lambda i, j: (i, j, 0)),
            pl.BlockSpec((bb, 1, d), lambda i, j: (i, 0, 0)),
            pl.BlockSpec((bb, 1, d), lambda i, j: (i, 0, 0)),
            pl.BlockSpec((1, 1, d), lambda i, j: (0, 0, 0)),
        ],
        out_specs=pl.BlockSpec((bb * tt, d), lambda i, j: (i * nt + j, 0)),
        out_shape=jax.ShapeDtypeStruct((b * t, d), BF16),
        compiler_params=_params(("arbitrary", "arbitrary"), vmem),
        name="norm_modulate",
    )(x, scale, shift, g.reshape(1, 1, d))


def _final_norm_kernel(x_ref, g_ref, o_ref):
    x = x_ref[...]
    ms = jnp.mean(x * x, axis=-1, keepdims=True)
    o_ref[...] = (x * lax.rsqrt(ms + EPS)) * g_ref[...]


def _final_norm(x, g, bb, tt):
    b, t, d = x.shape
    vmem = 8 * _nbytes((bb, tt, d), F32)
    return pl.pallas_call(
        _final_norm_kernel,
        grid=(b // bb, t // tt),
        in_specs=[
            pl.BlockSpec((bb, tt, d), lambda i, j: (i, j, 0)),
            pl.BlockSpec((1, 1, d), lambda i, j: (0, 0, 0)),
        ],
        out_specs=pl.BlockSpec((bb, tt, d), lambda i, j: (i, j, 0)),
        out_shape=jax.ShapeDtypeStruct((b, t, d), F32),
        compiler_params=_params(("arbitrary", "arbitrary"), vmem),
        name="final_norm",
    )(x, g.reshape(1, 1, d))


PROJ_TN = 512


def _cast_weight_tile(w_ref, wb_ref, first_row_tile):
    @pl.when(first_row_tile)
    def _():
        wb_ref[...] = w_ref[0].astype(BF16)


def _in_proj_kernel(a_ref, w_ref, o_ref, wb_ref):
    _cast_weight_tile(w_ref, wb_ref, pl.program_id(1) == 0)
    o_ref[...] = jnp.dot(a_ref[...], wb_ref[...], preferred_element_type=F32)


def _in_proj_operands(h2d, w_in, layer, col0, ncols, tm, idx):
    m, k = h2d.shape
    tn = PROJ_TN
    return dict(
        args=[h2d, w_in],
        in_specs=[
            pl.BlockSpec((tm, k), lambda *g: (idx(*g)[1], 0)),
            pl.BlockSpec((1, k, tn), lambda *g: (layer, 0, col0 // tn + idx(*g)[0])),
        ],
        out_specs=[pl.BlockSpec((tm, tn), lambda *g: (idx(*g)[1], idx(*g)[0]))],
        out_shape=[jax.ShapeDtypeStruct((m, ncols), F32)],
        scratch=[pltpu.VMEM((k, tn), BF16)],
        vmem=2 * (_nbytes((tm, k), BF16) + _nbytes((k, tn), F32) + _nbytes((tm, tn), F32))
        + _nbytes((k, tn), BF16) + _nbytes((tm, tn), F32),
    )


def _in_projection(h2d, w_in, layer, col0, ncols):
    m = h2d.shape[0]
    tm = 1024
    ops = _in_proj_operands(h2d, w_in, layer, col0, ncols, tm, lambda j, i: (j, i))
    return pl.pallas_call(
        _in_proj_kernel,
        grid=(ncols // PROJ_TN, m // tm),
        in_specs=ops["in_specs"],
        out_specs=ops["out_specs"][0],
        out_shape=ops["out_shape"][0],
        scratch_shapes=ops["scratch"],
        compiler_params=_params(("arbitrary", "arbitrary"), ops["vmem"] + (4 << 20)),
        name="in_projection",
    )(*ops["args"])


def _causal_taps(scr, v, w_ref, taps, tt):
    scr[:, HALO:HALO + tt, :] = v
    acc = None
    for k in range(taps):
        src = v if k == taps - 1 else scr[:, HALO - (taps - 1) + k:HALO - (taps - 1) + k + tt, :]
        term = w_ref[k:k + 1, :] * src
        acc = term if acc is None else acc + term
    return acc


def _block_scan_real(a, b):
    row = lax.broadcasted_iota(jnp.int32, (1, SUBLANES, a.shape[-1]), 1)
    for d in (1, 2, 4):
        keep = row >= d
        a_sh = pltpu.roll(a, d, 1)
        b_sh = pltpu.roll(b, d, 1)
        b = b + a * jnp.where(keep, b_sh, 0.0)
        a = a * jnp.where(keep, a_sh, 1.0)
    return a, b


def _block_scan_cplx(br, bi, tab_ref, base, sl):
    for lvl, d in enumerate((1, 2, 4)):
        lr = tab_ref[0, base + 2 * lvl, :, sl]
        li = tab_ref[0, base + 2 * lvl + 1, :, sl]
        sr = pltpu.roll(br, d, 1)
        si = pltpu.roll(bi, d, 1)
        br, bi = br + (lr * sr - li * si), bi + (lr * si + li * sr)
    return br, bi


def _spread_rows(scr, nblk, bb):
    return jnp.concatenate(
        [jnp.broadcast_to(scr[:, k:k + 1, :], (bb, SUBLANES, LANES)) for k in range(nblk)],
        axis=1)


def _mixer_a_kernel(ab_ref, ac_ref, ax_ref, ag_ref, w_ref, b_ref, st_ref,
                    o_ref, st_out_ref, scr, *, tt):
    keep = CONV_A - 1

    @pl.when(pl.program_id(2) == 0)
    def _():
        scr[:, HALO - keep:HALO, :] = st_ref[...]

    conv_in = ac_ref[...] * ax_ref[...]
    y = b_ref[...] + _causal_taps(scr, conv_in, w_ref, CONV_A, tt)
    out = (ab_ref[...] * y) * jax.nn.silu(ag_ref[...])
    o_ref[...] = out.reshape(o_ref.shape).astype(o_ref.dtype)
    tail = scr[:, HALO + tt - keep:HALO + tt, :]
    st_out_ref[...] = tail
    scr[:, HALO - keep:HALO, :] = tail


def _mixer_a(proj3, w_conv, b_conv, state, bb, tt):
    b, t, _ = proj3.shape
    cw = CH_TILE
    nc = W_A // cw
    nt = t // tt
    blk = (bb, tt, cw)

    def col(off):
        return pl.BlockSpec(blk, lambda i, c, j, off=off: (i, j, off // cw + c))

    vmem = 2 * (4 * _nbytes(blk, F32) + _nbytes(blk, BF16)) \
        + _nbytes((bb, HALO + tt, cw), F32) + 6 * _nbytes(blk, F32)
    return pl.pallas_call(
        functools.partial(_mixer_a_kernel, tt=tt),
        grid=(b // bb, nc, nt),
        in_specs=[
            col(0), col(W_A), col(2 * W_A), col(3 * W_A),
            pl.BlockSpec((CONV_A, cw), lambda i, c, j: (0, c)),
            pl.BlockSpec((1, cw), lambda i, c, j: (0, c)),
            pl.BlockSpec((bb, CONV_A - 1, cw), lambda i, c, j: (i, 0, c)),
        ],
        out_specs=[
            pl.BlockSpec((bb * tt, cw), lambda i, c, j: (i * nt + j, c)),
            pl.BlockSpec((bb, CONV_A - 1, cw), lambda i, c, j: (i, 0, c)),
        ],
        out_shape=[
            jax.ShapeDtypeStruct((b * t, W_A), BF16),
            jax.ShapeDtypeStruct((b, CONV_A - 1, W_A), F32),
        ],
        scratch_shapes=[pltpu.VMEM((bb, HALO + tt, cw), F32)],
        compiler_params=_params(("arbitrary",) * 3, vmem),
        name="mixer_a",
    )(proj3, proj3, proj3, proj3, w_conv, b_conv.reshape(1, W_A), state)


MIXER_B_INS = 10
MIXER_B_OUTS = 3


def _mixer_b_init(cst_ref, hst_ref, scr, carry, first_time_tile):
    @pl.when(first_time_tile)
    def _():
        scr[:, HALO - (CONV_B - 1):HALO, :] = cst_ref[...]
        carry[...] = hst_ref[...]


def _mixer_b_kernel(*refs, bb, tt):
    ins = refs[:MIXER_B_INS]
    outs = refs[MIXER_B_INS:MIXER_B_INS + MIXER_B_OUTS]
    scr, carry, *scan_scr = refs[MIXER_B_INS + MIXER_B_OUTS:]
    _mixer_b_init(ins[8], ins[9], scr, carry, pl.program_id(2) == 0)
    _mixer_b_step(*ins[:8], *outs, scr, carry, scan_scr, bb=bb, tt=tt)


def _mixer_b_step(bx_ref, bg_ref, w_ref, b_ref, wg_ref, brg_ref, big_ref, lam_ref,
                  o_ref, cst_out_ref, hst_out_ref, scr, carry, scan_scr, *, bb, tt,
                  side_matmul=None):
    keep = CONV_B - 1
    cw = CH_TILE
    rows = bb * tt
    nblk = tt // SUBLANES
    nv = nblk // SUBLANES

    bx = bx_ref[...]
    xb = b_ref[...] + _causal_taps(scr, bx, w_ref, CONV_B, tt)
    tail = scr[:, HALO + tt - keep:HALO + tt, :]
    cst_out_ref[...] = tail
    scr[:, HALO - keep:HALO, :] = tail

    xb_bf16 = xb.reshape(rows, cw).astype(BF16)
    softplus_neg_lam = jax.nn.softplus(-lam_ref[...])
    cin_all = carry[...]
    h_parts, last_parts = [], []
    assert LRU_HD == LANES
    n_heads = cw // LRU_HD
    for s in range(n_heads):
        sl = slice(s * LANES, (s + 1) * LANES)
        w_head = jnp.concatenate([wg_ref[0, sl, sl], wg_ref[0, sl, cw + s * LANES:cw + (s + 1) * LANES]],
                                 axis=1)
        gates = jnp.dot(xb_bf16[:, sl], w_head, preferred_element_type=F32)
        if side_matmul is not None:
            side_matmul(s, n_heads)
        xs = xb[:, :, sl]
        r = jax.nn.sigmoid(gates[:, :LANES] + brg_ref[:, sl]).reshape(bb, tt, LANES)
        ig = jax.nn.sigmoid(gates[:, LANES:] + big_ref[:, sl]).reshape(bb, tt, LANES)
        log_a = (-LRU_C * r) * softplus_neg_lam[:, sl]
        a = jnp.exp(log_a)
        beta = jnp.sqrt(-jnp.tanh(log_a) * (a * a + 1.0))
        bt = (beta * ig) * xs
        cin = cin_all[:, :, sl]
        a_blk, b_blk = _block_scan_real(a.reshape(bb * nblk, SUBLANES, LANES),
                                        bt.reshape(bb * nblk, SUBLANES, LANES))
        if nblk == 1:
            h = b_blk + a_blk * cin
            last = h[:, SUBLANES - 1:SUBLANES, :]
        else:
            acum, bcum, cscr = scan_scr
            acum[s] = a_blk.reshape(bb, tt, LANES)
            bcum[s] = b_blk.reshape(bb, tt, LANES)
            a2 = acum[s, :, pl.ds(SUBLANES - 1, nblk, stride=SUBLANES), :]
            b2 = bcum[s, :, pl.ds(SUBLANES - 1, nblk, stride=SUBLANES), :]
            a2, b2 = _block_scan_real(a2.reshape(bb * nv, SUBLANES, LANES),
                                      b2.reshape(bb * nv, SUBLANES, LANES))
            a2 = a2.reshape(bb, nblk, LANES)
            b2 = b2.reshape(bb, nblk, LANES)
            prev = cin
            ends = []
            for j in range(nv):
                seg = slice(j * SUBLANES, (j + 1) * SUBLANES)
                e = b2[:, seg, :] + a2[:, seg, :] * prev
                ends.append(e)
                prev = e[:, SUBLANES - 1:SUBLANES, :]
            last = prev
            cscr[s, :, 0:1, :] = cin
            cscr[s, :, 1:1 + nblk, :] = jnp.concatenate(ends, axis=1)
            spread = _spread_rows(cscr.at[s], nblk, bb)
            h = b_blk + a_blk * spread.reshape(bb * nblk, SUBLANES, LANES)
        h_parts.append(h.reshape(bb, tt, LANES))
        last_parts.append(last)
    h = jnp.concatenate(h_parts, axis=-1)
    last = jnp.concatenate(last_parts, axis=-1)
    carry[...] = last
    hst_out_ref[...] = last
    out = h * jax.nn.silu(bg_ref[...])
    o_ref[...] = out.reshape(o_ref.shape).astype(o_ref.dtype)


def _mixer_b_operands(proj3, w_conv, b_conv, w_gates, b_rg, b_ig, lam, conv_state, h_state,
                      bb, tt, idx):
    b, t, _ = proj3.shape
    cw = CH_TILE
    nt = t // tt
    nblk = tt // SUBLANES
    blk = (bb, tt, cw)

    def spec(shape, fn):
        return pl.BlockSpec(shape, lambda *g: fn(*idx(*g)))

    vec = spec((1, cw), lambda i, c, j: (0, c))
    scratch = [pltpu.VMEM((bb, HALO + tt, cw), F32), pltpu.VMEM((bb, 1, cw), F32)]
    if nblk > 1:
        assert nblk % SUBLANES == 0
        slabs = cw // LANES
        scratch += [pltpu.VMEM((slabs, bb, tt, LANES), F32),
                    pltpu.VMEM((slabs, bb, tt, LANES), F32),
                    pltpu.VMEM((slabs, bb, HALO + nblk, LANES), F32)]
    return dict(
        args=[proj3, proj3, w_conv, b_conv.reshape(1, W_B), w_gates,
              b_rg.reshape(1, W_B), b_ig.reshape(1, W_B), lam.reshape(1, W_B),
              conv_state, h_state],
        in_specs=[
            spec(blk, lambda i, c, j: (i, j, c)),
            spec(blk, lambda i, c, j: (i, j, W_B // cw + c)),
            spec((CONV_B, cw), lambda i, c, j: (0, c)),
            vec,
            spec((1, cw, 2 * cw), lambda i, c, j: (c, 0, 0)),
            vec, vec, vec,
            spec((bb, CONV_B - 1, cw), lambda i, c, j: (i, 0, c)),
            spec((bb, 1, cw), lambda i, c, j: (i, 0, c)),
        ],
        out_specs=[
            spec((bb * tt, cw), lambda i, c, j: (i * nt + j, c)),
            spec((bb, CONV_B - 1, cw), lambda i, c, j: (i, 0, c)),
            spec((bb, 1, cw), lambda i, c, j: (i, 0, c)),
        ],
        out_shape=[
            jax.ShapeDtypeStruct((b * t, W_B), BF16),
            jax.ShapeDtypeStruct((b, CONV_B - 1, W_B), F32),
            jax.ShapeDtypeStruct((b, 1, W_B), F32),
        ],
        scratch=scratch,
        vmem=2 * (2 * _nbytes(blk, F32) + _nbytes(blk, BF16) + _nbytes((cw, 2 * cw), BF16))
        + _nbytes((bb, HALO + tt, cw), F32) + 16 * _nbytes(blk, F32),
        grid=(b // bb, W_B // cw, nt),
    )


def _mixer_b(proj3, *params, bb, tt):
    ops = _mixer_b_operands(proj3, *params, bb, tt, lambda i, c, j: (i, c, j))
    return pl.pallas_call(
        functools.partial(_mixer_b_kernel, bb=bb, tt=tt),
        grid=ops["grid"],
        in_specs=ops["in_specs"],
        out_specs=ops["out_specs"],
        out_shape=ops["out_shape"],
        scratch_shapes=ops["scratch"],
        compiler_params=_params(("arbitrary",) * 3, ops["vmem"]),
        name="mixer_b",
    )(*ops["args"])


def _fused_proj_kernel(*refs, mixers, ni):
    a_ref, w_ref = refs[:2]
    mixers = [dict(mx) for mx in mixers]
    pos = 2
    for mx in mixers:
        mx["ins"] = refs[pos:pos + mx["n_ins"]]
        pos += mx["n_ins"]
    proj_ref = refs[pos]
    pos += 1
    for mx in mixers:
        mx["outs"] = refs[pos:pos + mx["n_outs"]]
        pos += mx["n_outs"]
    wb_ref = refs[pos]
    pos += 1
    for mx in mixers:
        mx["scr"] = refs[pos:pos + mx["n_scr"]]
        pos += mx["n_scr"]
    s = pl.program_id(0)
    _cast_weight_tile(w_ref, wb_ref, lax.rem(s, ni) == 0)
    for mx in mixers:
        mx["init"](mx["ins"], mx["scr"], s)
    k_total = a_ref.shape[1]
    acc = []

    def projection_part(part, n_parts):
        kc = MXU_DIM
        per_part = k_total // (kc * n_parts)
        for sub in range(part * per_part, (part + 1) * per_part):
            ks = slice(sub * kc, (sub + 1) * kc)
            term = jnp.dot(a_ref[:, ks], wb_ref[ks, :], preferred_element_type=F32)
            acc[:] = [term if not acc else acc[0] + term]
        if part == n_parts - 1:
            proj_ref[...] = acc[0]

    for mx in mixers[:-1]:
        mx["step"](mx["ins"], mx["outs"], mx["scr"], s, None)
    mx = mixers[-1]
    mx["step"](mx["ins"], mx["outs"], mx["scr"], s, projection_part)


def _fused_proj_call(name, h2d, w_in, layer, col0, ncols, mixers):
    steps = None
    for ops, _, _ in mixers:
        n = 1
        for g in ops["grid"]:
            n *= g
        assert steps in (None, n)
        steps = n
    nj = ncols // PROJ_TN
    ni = steps // nj
    tm = h2d.shape[0] // ni
    assert nj * ni == steps and tm * ni == h2d.shape[0] and tm % SUBLANES == 0
    mm = _in_proj_operands(h2d, w_in, layer, col0, ncols, tm, lambda s: (s // ni, s % ni))
    specs = [dict(n_ins=len(ops["args"]), n_outs=len(ops["out_shape"]),
                  n_scr=len(ops["scratch"]), init=init, step=step)
             for ops, init, step in mixers]
    cat = lambda key: [x for ops, _, _ in mixers for x in ops[key]]
    return pl.pallas_call(
        functools.partial(_fused_proj_kernel, mixers=specs, ni=ni),
        grid=(steps,),
        in_specs=mm["in_specs"] + cat("in_specs"),
        out_specs=mm["out_specs"] + cat("out_specs"),
        out_shape=mm["out_shape"] + cat("out_shape"),
        scratch_shapes=mm["scratch"] + cat("scratch"),
        compiler_params=_params(("arbitrary",),
                                mm["vmem"] + sum(ops["vmem"] for ops, _, _ in mixers)),
        name=name,
    )(*mm["args"], *cat("args"))


def _flat_index(dims):
    def idx(s):
        out = []
        for k, d in enumerate(dims):
            stride = 1
            for e in dims[k + 1:]:
                stride *= e
            out.append((s // stride) % d if k else s // stride)
        return tuple(out)
    return idx


def _flat_mixer_b(proj3, *params, bb, tt):
    b, t, _ = proj3.shape
    dims = (b // bb, W_B // CH_TILE, t // tt)
    ops = _mixer_b_operands(proj3, *params, bb, tt, _flat_index(dims))
    nt = dims[2]

    def init(ins, scr, s):
        _mixer_b_init(ins[8], ins[9], scr[0], scr[1], lax.rem(s, nt) == 0)

    def step(ins, outs, scr, s, side_matmul):
        _mixer_b_step(*ins[:8], *outs, scr[0], scr[1], scr[2:], bb=bb, tt=tt,
                      side_matmul=side_matmul)

    return ops, init, step


def _s5_disc_kernel(lre_ref, lim_ref, ldt_ref, btre_ref, btim_ref,
                    abre_ref, abim_ref, bbre_ref, bbim_ref):
    lre = lre_ref[0]
    lim = lim_ref[0]
    dt = jnp.exp(ldt_ref[0])
    mag = jnp.exp(lre * dt)
    ab_re = mag * jnp.cos(lim * dt)
    ab_im = mag * jnp.sin(lim * dt)
    nr = ab_re - 1.0
    den = lre * lre + lim * lim
    fr = (nr * lre + ab_im * lim) / den
    fi = (ab_im * lre - nr * lim) / den
    bre = btre_ref[0]
    bim = btim_ref[0]
    bbre_ref[0] = fr * bre - fi * bim
    bbim_ref[0] = fr * bim + fi * bre
    abre_ref[0] = ab_re
    abim_ref[0] = ab_im


def _s5_discretize(lam_re, lam_im, log_dt, b_re, b_im):
    depth = lam_re.shape[0]
    gp = (depth, S5_G, 1, S5_P)
    ldt = jnp.broadcast_to(log_dt[:, :, None, None], gp)
    bt_shape = (depth, S5_G, S5_GROUP, S5_P)
    small = pl.BlockSpec((1, S5_G, 1, S5_P), lambda l: (l, 0, 0, 0))
    big = pl.BlockSpec((1, S5_G, S5_GROUP, S5_P), lambda l: (l, 0, 0, 0))
    ab_re, ab_im, bb_re, bb_im = pl.pallas_call(
        _s5_disc_kernel,
        grid=(depth,),
        in_specs=[small, small, small, big, big],
        out_specs=[small, small, big, big],
        out_shape=[jax.ShapeDtypeStruct(gp, F32), jax.ShapeDtypeStruct(gp, F32),
                   jax.ShapeDtypeStruct(bt_shape, F32), jax.ShapeDtypeStruct(bt_shape, F32)],
        name="s5_discretize",
    )(lam_re.reshape(gp), lam_im.reshape(gp), ldt,
      jnp.swapaxes(b_re, 2, 3), jnp.swapaxes(b_im, 2, 3))
    return (ab_re.reshape(depth, 1, S5_STATES), ab_im.reshape(depth, 1, S5_STATES),
            bb_re, bb_im)


def _s5_pow_kernel(ar_ref, ai_ref, tab_ref):
    def cmul(x, y):
        return x[0] * y[0] - x[1] * y[1], x[0] * y[1] + x[1] * y[0]

    def powers(b1):
        b2 = cmul(b1, b1)
        b3 = cmul(b2, b1)
        b4 = cmul(b2, b2)
        return (b1, b2, b3, b4, cmul(b4, b1), cmul(b4, b2), cmul(b4, b3), cmul(b4, b4))

    shape = (SUBLANES, S5_STATES)
    row = lax.broadcasted_iota(jnp.int32, shape, 0)
    zero = jnp.zeros(shape, F32)
    pw_a = powers((ar_ref[0], ai_ref[0]))
    pw_a8 = powers(pw_a[SUBLANES - 1])
    for base, pw in ((0, pw_a), (N_TABLES // 2, pw_a8)):
        for part in (0, 1):
            tab_ref[0, base + part] = jnp.where(row >= 1, pw[0][part], zero)
            tab_ref[0, base + 2 + part] = jnp.where(row >= 2, pw[1][part], zero)
            tab_ref[0, base + 4 + part] = jnp.where(row >= 4, pw[3][part], zero)
            p = zero
            for r, v in enumerate(pw):
                p = jnp.where(row == r, v[part], p)
            tab_ref[0, base + 6 + part] = p


def _s5_power_tables(ab_re, ab_im):
    depth = ab_re.shape[0]
    vec = pl.BlockSpec((1, 1, S5_STATES), lambda l: (l, 0, 0))
    return pl.pallas_call(
        _s5_pow_kernel,
        grid=(depth,),
        in_specs=[vec, vec],
        out_specs=pl.BlockSpec((1, N_TABLES, SUBLANES, S5_STATES), lambda l: (l, 0, 0, 0)),
        out_shape=jax.ShapeDtypeStruct((depth, N_TABLES, SUBLANES, S5_STATES), F32),
        name="s5_power_tables",
    )(ab_re, ab_im)


def _s5_block_matrices(bb_re, bb_im, c_re, c_im):
    depth = bb_re.shape[0]
    gpt = CH_TILE // S5_GROUP
    chan = lax.broadcasted_iota(jnp.int32, (S5_Q, CH_TILE, CH_TILE), 1)
    state = lax.broadcasted_iota(jnp.int32, (S5_Q, CH_TILE, CH_TILE), 2)
    slab = lax.broadcasted_iota(jnp.int32, (S5_Q, CH_TILE, CH_TILE), 0)
    gps = LANES // S5_P
    keep = (chan // S5_GROUP) == gps * slab + (state % LANES) // S5_P

    def dense_b(x):
        return x.reshape(depth * S5_KT, gpt * S5_GROUP, S5_P)

    db = jnp.concatenate([dense_b(bb_re)] * gps + [dense_b(bb_im)] * gps, axis=-1)
    bm = jnp.where(keep[None], db[:, None], 0.0).astype(BF16)

    def dense_c(x):
        x = x.reshape(depth * S5_KT, gpt, S5_GROUP, S5_P)
        return jnp.transpose(x, (0, 3, 1, 2)).reshape(depth * S5_KT, S5_P, gpt * S5_GROUP)

    dc = jnp.concatenate([dense_c(c_re)] * gps + [dense_c(-c_im)] * gps, axis=1)
    keep_c = jnp.swapaxes(keep, 1, 2)
    cm = jnp.where(keep_c[None], dc[:, None], 0.0).astype(BF16)
    return bm, cm


MIXER_C_INS = 10
MIXER_C_OUTS = 3


def _mixer_c_init(sre_ref, sim_ref, car_r, car_i, kt, first_time_tile):
    @pl.when(first_time_tile)
    def _():
        car_r[kt] = sre_ref[...]
        car_i[kt] = sim_ref[...]


def _mixer_c_kernel(*refs, bb, tt):
    ins = refs[:MIXER_C_INS]
    outs = refs[MIXER_C_INS:MIXER_C_INS + MIXER_C_OUTS]
    ystore, car_r, car_i, *scan_scr = refs[MIXER_C_INS + MIXER_C_OUTS:]
    kt = pl.program_id(2)
    _mixer_c_init(ins[8], ins[9], car_r, car_i, kt, pl.program_id(1) == 0)
    _mixer_c_step(*ins[:8], *outs, ystore, car_r, car_i, scan_scr, kt, bb=bb, tt=tt)


def _mixer_c_step(cu_ref, cg_ref, bm_ref, cm_ref, tab_ref, d_ref, wglu_ref, bglu_ref,
                  o_ref, sre_out_ref, sim_out_ref,
                  ystore, car_r, car_i, scan_scr, kt, *, bb, tt, side_matmul=None):
    cw = CH_TILE
    rows = bb * tt
    nblk = tt // SUBLANES
    nv = nblk // SUBLANES
    n = bb * nblk
    base8 = N_TABLES // 2

    u = cu_ref[...].reshape(rows, cw)
    ub = u.astype(BF16)
    y = d_ref[...] * u
    cr_all = car_r[kt]
    ci_all = car_i[kt]
    new_r, new_i = [], []
    for q in range(S5_Q):
        sl = slice(LANES * q, LANES * (q + 1))
        bu = jnp.dot(ub, bm_ref[0, q], preferred_element_type=F32)
        if side_matmul is not None:
            side_matmul(q, S5_Q)
        br, bi = _block_scan_cplx(bu[:, :LANES].reshape(n, SUBLANES, LANES),
                                  bu[:, LANES:].reshape(n, SUBLANES, LANES), tab_ref, 0, sl)
        p_r = tab_ref[0, 6, :, sl]
        p_i = tab_ref[0, 7, :, sl]
        cr = cr_all[:, :, sl]
        ci = ci_all[:, :, sl]
        if nblk == 1:
            kr, ki = cr, ci
        else:
            hsr, hsi, csr, csi = scan_scr
            hsr[...] = br.reshape(bb, tt, LANES)
            hsi[...] = bi.reshape(bb, tt, LANES)
            xr = hsr[:, pl.ds(SUBLANES - 1, nblk, stride=SUBLANES), :]
            xi = hsi[:, pl.ds(SUBLANES - 1, nblk, stride=SUBLANES), :]
            xr, xi = _block_scan_cplx(xr.reshape(bb * nv, SUBLANES, LANES),
                                      xi.reshape(bb * nv, SUBLANES, LANES), tab_ref, base8, sl)
            xr = xr.reshape(bb, nblk, LANES)
            xi = xi.reshape(bb, nblk, LANES)
            q_r = tab_ref[0, base8 + 6, :, sl]
            q_i = tab_ref[0, base8 + 7, :, sl]
            prev_r, prev_i = cr, ci
            ends_r, ends_i = [], []
            for j in range(nv):
                seg = slice(j * SUBLANES, (j + 1) * SUBLANES)
                er = xr[:, seg, :] + (q_r * prev_r - q_i * prev_i)
                ei = xi[:, seg, :] + (q_r * prev_i + q_i * prev_r)
                ends_r.append(er)
                ends_i.append(ei)
                prev_r = er[:, SUBLANES - 1:SUBLANES, :]
                prev_i = ei[:, SUBLANES - 1:SUBLANES, :]
            csr[:, 0:1, :] = cr
            csi[:, 0:1, :] = ci
            csr[:, 1:1 + nblk, :] = jnp.concatenate(ends_r, axis=1)
            csi[:, 1:1 + nblk, :] = jnp.concatenate(ends_i, axis=1)
            kr = _spread_rows(csr, nblk, bb).reshape(n, SUBLANES, LANES)
            ki = _spread_rows(csi, nblk, bb).reshape(n, SUBLANES, LANES)
        hr = br + (p_r * kr - p_i * ki)
        hi = bi + (p_r * ki + p_i * kr)
        hr = hr.reshape(bb, tt, LANES)
        hi = hi.reshape(bb, tt, LANES)
        new_r.append(hr[:, tt - 1:tt, :])
        new_i.append(hi[:, tt - 1:tt, :])
        hcat = jnp.concatenate([hr.reshape(rows, LANES), hi.reshape(rows, LANES)], axis=-1)
        y = y + jnp.dot(hcat.astype(BF16), cm_ref[0, q], preferred_element_type=F32)

    ncr_all = jnp.concatenate(new_r, axis=-1)
    nci_all = jnp.concatenate(new_i, axis=-1)
    car_r[kt] = ncr_all
    car_i[kt] = nci_all
    sre_out_ref[:, kt] = ncr_all
    sim_out_ref[:, kt] = nci_all
    ystore[kt] = y

    @pl.when(kt == S5_KT - 1)
    def _():
        y_all = jnp.concatenate([ystore[k] for k in range(S5_KT)], axis=-1)
        yg = jax.nn.gelu(y_all)
        z = jnp.dot(yg.astype(BF16), wglu_ref[0].astype(BF16),
                    preferred_element_type=F32) + bglu_ref[...]
        yy = yg * jax.nn.sigmoid(z)
        out = yy * jax.nn.silu(cg_ref[...].reshape(rows, W_C))
        o_ref[...] = out.astype(o_ref.dtype)


def _mixer_c_operands(proj3, bmat, cmat, tables, d_skip, w_glu, b_glu, s_re, s_im, layer,
                      bb, tt, idx):
    b, t, _ = proj3.shape
    u_col = 0
    cw = CH_TILE
    nt = t // tt
    nblk = tt // SUBLANES
    kslab = S5_Q * LANES

    def spec(shape, fn):
        return pl.BlockSpec(shape, lambda *g: fn(*idx(*g)))

    mat = spec((1, S5_Q, cw, cw), lambda i, j, k: (layer * S5_KT + k, 0, 0, 0))
    st = spec((bb, 1, kslab), lambda i, j, k: (i, 0, k))
    st_out = spec((bb, S5_KT, 1, kslab), lambda i, j, k: (i, 0, 0, 0))
    rows = bb * tt
    scratch = [pltpu.VMEM((S5_KT, rows, cw), F32),
               pltpu.VMEM((S5_KT, bb, 1, kslab), F32),
               pltpu.VMEM((S5_KT, bb, 1, kslab), F32)]
    if nblk > 1:
        assert nblk % SUBLANES == 0 and nblk // SUBLANES <= SUBLANES
        scratch += [pltpu.VMEM((bb, tt, LANES), F32), pltpu.VMEM((bb, tt, LANES), F32),
                    pltpu.VMEM((bb, HALO + nblk, LANES), F32),
                    pltpu.VMEM((bb, HALO + nblk, LANES), F32)]
    vmem = 2 * (_nbytes((bb, tt, cw), F32) + _nbytes((bb, tt, W_C), F32)
                + 2 * _nbytes((S5_Q, cw, cw), BF16)
                + _nbytes((N_TABLES, SUBLANES, kslab), F32)
                + _nbytes((W_C, W_C), F32) + _nbytes((bb, tt, W_C), BF16)
                + 4 * _nbytes((bb, SUBLANES, kslab), F32)) \
        + _nbytes((S5_KT, rows, cw), F32) + 2 * _nbytes((bb, tt, LANES), F32) \
        + 2 * _nbytes((S5_KT, bb, SUBLANES, kslab), F32) + 10 * _nbytes((rows, W_C), F32)
    return dict(
        args=[proj3, proj3, bmat, cmat, tables, d_skip.reshape(1, W_C), w_glu,
              b_glu.reshape(1, W_C), s_re, s_im],
        in_specs=[
            spec((bb, tt, cw), lambda i, j, k: (i, j, u_col // cw + k)),
            spec((bb, tt, W_C), lambda i, j, k: (i, j, u_col // W_C + 1)),
            mat, mat,
            spec((1, N_TABLES, SUBLANES, kslab), lambda i, j, k: (layer, 0, 0, k)),
            spec((1, cw), lambda i, j, k: (0, k)),
            spec((1, W_C, W_C), lambda i, j, k: (layer, 0, 0)),
            spec((1, W_C), lambda i, j, k: (0, 0)),
            st, st,
        ],
        out_specs=[
            spec((rows, W_C), lambda i, j, k: (i * nt + j, 0)),
            st_out, st_out,
        ],
        out_shape=[
            jax.ShapeDtypeStruct((b * t, W_C), BF16),
            jax.ShapeDtypeStruct((b, S5_KT, 1, kslab), F32),
            jax.ShapeDtypeStruct((b, S5_KT, 1, kslab), F32),
        ],
        scratch=scratch,
        vmem=vmem,
        grid=(b // bb, nt, S5_KT),
    )


def _mixer_c(proj3, *params, bb, tt):
    ops = _mixer_c_operands(proj3, *params, bb, tt, lambda i, j, k: (i, j, k))
    return pl.pallas_call(
        functools.partial(_mixer_c_kernel, bb=bb, tt=tt),
        grid=ops["grid"],
        in_specs=ops["in_specs"],
        out_specs=ops["out_specs"],
        out_shape=ops["out_shape"],
        scratch_shapes=ops["scratch"],
        compiler_params=_params(("arbitrary",) * 3, ops["vmem"]),
        name="mixer_c",
    )(*ops["args"])


def _flat_mixer_c(proj3, *params, bb, tt):
    b, t, _ = proj3.shape
    dims = (b // bb, t // tt, S5_KT)
    ops = _mixer_c_operands(proj3, *params, bb, tt, _flat_index(dims))
    nt = dims[1]

    def init(ins, scr, s):
        _mixer_c_init(ins[8], ins[9], scr[1], scr[2], lax.rem(s, S5_KT),
                      lax.rem(s // S5_KT, nt) == 0)

    def step(ins, outs, scr, s, side_matmul):
        _mixer_c_step(*ins[:8], *outs, scr[0], scr[1], scr[2], scr[3:],
                      lax.rem(s, S5_KT), bb=bb, tt=tt, side_matmul=side_matmul)

    return ops, init, step


def _out_proj_kernel(ma_ref, mb_ref, mc_ref, w_ref, x_ref, gate_ref, o_ref, *, bb, tt):
    w = w_ref[0].astype(BF16)
    acc = jnp.dot(ma_ref[...], w[0:W_A, :], preferred_element_type=F32)
    acc = acc + jnp.dot(mb_ref[...], w[W_A:W_A + W_B, :], preferred_element_type=F32)
    acc = acc + jnp.dot(mc_ref[...], w[W_A + W_B:, :], preferred_element_type=F32)
    o_ref[...] = x_ref[...] + gate_ref[...] * acc.reshape(bb, tt, acc.shape[-1])


def _out_projection(out_a, out_b, out_c, w_out, x, gate, layer, bb, tt):
    b, t, d = x.shape
    tm = bb * tt
    tn = 512
    nt = t // tt
    m = b * t

    def rows(width):
        return pl.BlockSpec((tm, width), lambda i, j: (i, 0))

    vmem = 2 * (_nbytes((tm, d), BF16) + _nbytes((d, tn), F32)
                + 2 * _nbytes((tm, tn), F32)) + _nbytes((d, tn), BF16) \
        + 3 * _nbytes((tm, tn), F32)
    return pl.pallas_call(
        functools.partial(_out_proj_kernel, bb=bb, tt=tt),
        grid=(m // tm, d // tn),
        in_specs=[
            rows(W_A), rows(W_B), rows(W_C),
            pl.BlockSpec((1, d, tn), lambda i, j: (layer, 0, j)),
            pl.BlockSpec((bb, tt, tn), lambda i, j: (i // nt, i % nt, j)),
            pl.BlockSpec((bb, 1, tn), lambda i, j: (i // nt, 0, j)),
        ],
        out_specs=pl.BlockSpec((bb, tt, tn), lambda i, j: (i // nt, i % nt, j)),
        out_shape=jax.ShapeDtypeStruct((b, t, d), F32),
        compiler_params=_params(("arbitrary", "arbitrary"), vmem + (4 << 20)),
        name="out_projection",
    )(out_a, out_b, out_c, w_out, x, gate)


def _gate_weights(w_rg, w_ig):
    hpt = CH_TILE // LRU_HD
    pairs = LRU_HEADS // hpt
    w = jnp.stack([w_rg, w_ig], axis=2)
    w = w.reshape(pairs, hpt, LRU_HD, 2, LRU_HD)
    eye = jnp.eye(hpt, dtype=bool)
    out = jnp.where(eye[None, :, None, None, :, None],
                    w[:, :, :, :, None, :], 0.0)
    return out.reshape(pairs, CH_TILE, 2 * CH_TILE).astype(BF16)


def _layer(x, mod, states, wts, layer, tiles):
    b, t, d = x.shape
    conv_a, conv_b, lru_h, s5_re, s5_im = states
    shift, scale, gate = (mod[:, None, k * d:(k + 1) * d] for k in range(3))

    h = _norm_modulate(x, scale, shift, wts["g_norm"], *tiles["norm"])
    w_in = wts["w_in"]
    col_a, col_b, col_c = 0, 4 * W_A, 4 * W_A + 2 * W_B
    b_params = (wts["w_conv_b"], wts["b_conv_b"], wts["w_gates"], wts["b_rg"], wts["b_ig"],
                wts["lru_lambda"], conv_b, lru_h.reshape(b, 1, W_B))
    c_params = (wts["bmat"], wts["cmat"], wts["tables"], wts["s5_d"], wts["w_glu"],
                wts["b_glu"], s5_re.reshape(b, 1, S5_STATES), s5_im.reshape(b, 1, S5_STATES),
                layer)
    bb_b, tt_b = tiles["b"]
    bb_c, tt_c = tiles["c"]

    proj_c = _in_projection(h, w_in, layer, col_c, 2 * W_C).reshape(b, t, 2 * W_C)
    if tiles["fuse"]:
        proj_b, out_c, re_new, im_new = _fused_proj_call(
            "proj_b_mixer_c", h, w_in, layer, col_b, 2 * W_B,
            [_flat_mixer_c(proj_c, *c_params, bb=bb_c, tt=tt_c)])
        proj_b = proj_b.reshape(b, t, 2 * W_B)
        proj_a, out_b, conv_b_new, lru_new = _fused_proj_call(
            "proj_a_mixer_b", h, w_in, layer, col_a, 4 * W_A,
            [_flat_mixer_b(proj_b, *b_params, bb=bb_b, tt=tt_b)])
    else:
        out_c, re_new, im_new = _mixer_c(proj_c, *c_params, bb=bb_c, tt=tt_c)
        proj_b = _in_projection(h, w_in, layer, col_b, 2 * W_B).reshape(b, t, 2 * W_B)
        out_b, conv_b_new, lru_new = _mixer_b(proj_b, *b_params, bb=bb_b, tt=tt_b)
        proj_a = _in_projection(h, w_in, layer, col_a, 4 * W_A)
    proj_a = proj_a.reshape(b, t, 4 * W_A)
    out_a, conv_a_new = _mixer_a(proj_a, wts["w_conv_a"], wts["b_conv_a"], conv_a,
                                 *tiles["a"])

    x_new = _out_projection(out_a, out_b, out_c, wts["w_out"], x, gate, layer, *tiles["out"])
    return (x_new, conv_a_new, conv_b_new, lru_new.reshape(b, W_B),
            re_new.reshape(b, S5_G, S5_P), im_new.reshape(b, S5_G, S5_P))


PROMPT_TILES = {"norm": (1, 256), "a": (1, 1024), "b": (1, 1024), "c": (1, 256),
                "out": (1, 1024), "final": (1, 256), "fuse": True}
SAMPLE_TILES = {"norm": (32, 8), "a": (128, 8), "b": (64, 8), "c": (32, 8),
                "out": (128, 8), "final": (32, 8), "fuse": False}


def kernel(x_prompt, x_sample, c_prompt, c_sample, state_conv_a, state_conv_b, state_lru_h,
           state_s5_re, state_s5_im, g_norm, w_ada, b_ada, w_in, w_conv_a, b_conv_a,
           w_conv_b, b_conv_b, w_rg, b_rg, w_ig, b_ig, lru_lambda, s5_lambda_re,
           s5_lambda_im, s5_log_dt, s5_b_re, s5_b_im, s5_c_re, s5_c_im, s5_d, w_glu,
           b_glu, w_out, g_final):
    depth = w_in.shape[0]
    bp = x_prompt.shape[0]
    bs = x_sample.shape[0]

    c_all = jnp.concatenate([c_prompt, c_sample], axis=0)
    pad = (-c_all.shape[0]) % SUBLANES
    c_all = jnp.pad(c_all, ((0, pad), (0, 0)))
    mod = _modulation(c_all, w_ada, b_ada)

    ab_re, ab_im, bb_re, bb_im = _s5_discretize(s5_lambda_re, s5_lambda_im, s5_log_dt,
                                               s5_b_re, s5_b_im)
    tables = _s5_power_tables(ab_re, ab_im)
    bmat, cmat = _s5_block_matrices(bb_re, bb_im, s5_c_re, s5_c_im)
    w_glu_bf16 = w_glu.astype(BF16)

    xp, xs = x_prompt, x_sample
    zeros_p = (jnp.zeros((bp, CONV_A - 1, W_A), F32), jnp.zeros((bp, CONV_B - 1, W_B), F32),
               jnp.zeros((bp, W_B), F32), jnp.zeros((bp, S5_G, S5_P), F32),
               jnp.zeros((bp, S5_G, S5_P), F32))
    outs_p, outs_s = [], []
    for l in range(depth):
        wts = {
            "g_norm": g_norm[l], "w_in": w_in,
            "w_conv_a": w_conv_a[l], "b_conv_a": b_conv_a[l],
            "w_conv_b": w_conv_b[l], "b_conv_b": b_conv_b[l],
            "w_gates": _gate_weights(w_rg[l], w_ig[l]), "b_rg": b_rg[l], "b_ig": b_ig[l],
            "lru_lambda": lru_lambda[l], "bmat": bmat, "cmat": cmat,
            "tables": tables, "s5_d": s5_d[l],
            "w_glu": w_glu_bf16, "b_glu": b_glu[l], "w_out": w_out,
        }
        res_p = _layer(xp, mod[l, :bp], zeros_p, wts, l, PROMPT_TILES)
        xp = res_p[0]
        outs_p.append(res_p[1:])
        st_s = (state_conv_a[l], state_conv_b[l], state_lru_h[l], state_s5_re[l],
                state_s5_im[l])
        res_s = _layer(xs, mod[l, bp:bp + bs], st_s, wts, l, SAMPLE_TILES)
        xs = res_s[0]
        outs_s.append(res_s[1:])

    y_prompt = _final_norm(xp, g_final, *PROMPT_TILES["final"])
    y_sample = _final_norm(xs, g_final, *SAMPLE_TILES["final"])
    stack = lambda outs, k: jnp.stack([o[k] for o in outs])
    return (y_prompt, y_sample,
            *(stack(outs_p, k) for k in range(5)),
            *(stack(outs_s, k) for k in range(5)))
```

```python
import functools

import jax
import jax.numpy as jnp
from jax import lax
from jax.experimental import pallas as pl
from jax.experimental.pallas import tpu as pltpu

F32 = jnp.float32
BF16 = jnp.bfloat16

D_MODEL = 4096
W_A = D_MODEL // 4
W_B = D_MODEL // 2
W_C = D_MODEL // 4
IN_COLS = 4 * W_A + 2 * W_B + 2 * W_C
CONV_A = 3
CONV_B = 4
LRU_HEADS = 16
LRU_HD = W_B // LRU_HEADS
LRU_C = 8.0
S5_GROUP = 16
S5_G = W_C // S5_GROUP
S5_P = 64
S5_STATES = S5_G * S5_P
EPS = 1e-6

SUBLANES = 8
LANES = 128
MXU_DIM = 256
VMEM_LIMIT_CAP = 60000 * 1024
CH_TILE = MXU_DIM
S5_KT = W_C // CH_TILE
S5_Q = (CH_TILE // S5_GROUP) * S5_P // LANES
HALO = SUBLANES
N_TABLES = 16
GATE_CHUNK_ROWS = 128


def _params(sem, vmem_bytes):
    return pltpu.CompilerParams(
        dimension_semantics=sem,
        vmem_limit_bytes=int(min(VMEM_LIMIT_CAP, vmem_bytes)))


def _nbytes(shape, dtype):
    n = 1
    for s in shape:
        n *= s
    return n * jnp.dtype(dtype).itemsize


def _mod_kernel(c_ref, w_ref, b_ref, o_ref):
    c = c_ref[...].astype(BF16)
    w = w_ref[0].astype(BF16)
    o_ref[0] = jnp.dot(c, w, preferred_element_type=F32) + b_ref[0]


def _modulation(c_all, w_ada, b_ada):
    depth, d, n = w_ada.shape
    rows = c_all.shape[0]
    tn = 512
    vmem = 2 * (_nbytes((rows, d), F32) + _nbytes((d, tn), F32)
                + _nbytes((rows, tn), F32)) + _nbytes((d, tn), F32)
    return pl.pallas_call(
        _mod_kernel,
        grid=(depth, n // tn),
        in_specs=[
            pl.BlockSpec((rows, d), lambda l, j: (0, 0)),
            pl.BlockSpec((1, d, tn), lambda l, j: (l, 0, j)),
            pl.BlockSpec((1, 1, tn), lambda l, j: (l, 0, j)),
        ],
        out_specs=pl.BlockSpec((1, rows, tn), lambda l, j: (l, 0, j)),
        out_shape=jax.ShapeDtypeStruct((depth, rows, n), F32),
        compiler_params=_params(("arbitrary", "arbitrary"), vmem + (8 << 20)),
        name="adaln_modulation",
    )(c_all, w_ada, b_ada.reshape(depth, 1, n))


def _norm_mod_kernel(x_ref, scale_ref, shift_ref, g_ref, o_ref):
    x = x_ref[...]
    ms = jnp.mean(x * x, axis=-1, keepdims=True)
    y = (x * lax.rsqrt(ms + EPS)) * g_ref[...]
    h = y * (1.0 + scale_ref[...]) + shift_ref[...]
    o_ref[...] = h.reshape(o_ref.shape).astype(o_ref.dtype)


def _norm_modulate(x, scale, shift, g, bb, tt):
    b, t, d = x.shape
    nt = t // tt
    vmem = 2 * (_nbytes((bb, tt, d), F32) + _nbytes((bb, tt, d), BF16)) \
        + 4 * _nbytes((bb, tt, d), F32)
    return pl.pallas_call(
        _norm_mod_kernel,
        grid=(b // bb, nt),
        in_specs=[
            pl.BlockSpec((bb, tt, d), lambda i, j: (i, j, 0)),
            pl.BlockSpec((bb, 1, d), lambda i, j: (i, 0, 0)),
            pl.BlockSpec((bb, 1, d), lambda i, j: (i, 0, 0)),
            pl.BlockSpec((1, 1, d), lambda i, j: (0, 0, 0)),
        ],
        out_specs=pl.BlockSpec((bb * tt, d), lambda i, j: (i * nt + j, 0)),
        out_shape=jax.ShapeDtypeStruct((b * t, d), BF16),
        compiler_params=_params(("arbitrary", "arbitrary"), vmem),
        name="norm_modulate",
    )(x, scale, shift, g.reshape(1, 1, d))


def _final_norm_kernel(x_ref, g_ref, o_ref):
    x = x_ref[...]
    ms = jnp.mean(x * x, axis=-1, keepdims=True)
    o_ref[...] = (x * lax.rsqrt(ms + EPS)) * g_ref[...]


def _final_norm(x, g, bb, tt):
    b, t, d = x.shape
    vmem = 8 * _nbytes((bb, tt, d), F32)
    return pl.pallas_call(
        _final_norm_kernel,
        grid=(b // bb, t // tt),
        in_specs=[
            pl.BlockSpec((bb, tt, d), lambda i, j: (i, j, 0)),
            pl.BlockSpec((1, 1, d), lambda i, j: (0, 0, 0)),
        ],
        out_specs=pl.BlockSpec((bb, tt, d), lambda i, j: (i, j, 0)),
        out_shape=jax.ShapeDtypeStruct((b, t, d), F32),
        compiler_params=_params(("arbitrary", "arbitrary"), vmem),
        name="final_norm",
    )(x, g.reshape(1, 1, d))


PROJ_TN = 512


def _cast_weight_tile(w_ref, wb_ref, first_row_tile):
    @pl.when(first_row_tile)
    def _():
        wb_ref[...] = w_ref[0].astype(BF16)


def _in_proj_kernel(a_ref, w_ref, o_ref, wb_ref):
    _cast_weight_tile(w_ref, wb_ref, pl.program_id(1) == 0)
    o_ref[...] = jnp.dot(a_ref[...], wb_ref[...], preferred_element_type=F32)


def _in_proj_operands(h2d, w_in, layer, col0, ncols, tm, idx):
    m, k = h2d.shape
    tn = PROJ_TN
    return dict(
        args=[h2d, w_in],
        in_specs=[
            pl.BlockSpec((tm, k), lambda *g: (idx(*g)[1], 0)),
            pl.BlockSpec((1, k, tn), lambda *g: (layer, 0, col0 // tn + idx(*g)[0])),
        ],
        out_specs=[pl.BlockSpec((tm, tn), lambda *g: (idx(*g)[1], idx(*g)[0]))],
        out_shape=[jax.ShapeDtypeStruct((m, ncols), F32)],
        scratch=[pltpu.VMEM((k, tn), BF16)],
        vmem=2 * (_nbytes((tm, k), BF16) + _nbytes((k, tn), F32) + _nbytes((tm, tn), F32))
        + _nbytes((k, tn), BF16) + _nbytes((tm, tn), F32),
    )


def _in_projection(h2d, w_in, layer, col0, ncols):
    m = h2d.shape[0]
    tm = 1024
    ops = _in_proj_operands(h2d, w_in, layer, col0, ncols, tm, lambda j, i: (j, i))
    return pl.pallas_call(
        _in_proj_kernel,
        grid=(ncols // PROJ_TN, m // tm),
        in_specs=ops["in_specs"],
        out_specs=ops["out_specs"][0],
        out_shape=ops["out_shape"][0],
        scratch_shapes=ops["scratch"],
        compiler_params=_params(("arbitrary", "arbitrary"), ops["vmem"] + (4 << 20)),
        name="in_projection",
    )(*ops["args"])


def _causal_taps(scr, v, w_ref, taps, tt):
    scr[:, HALO:HALO + tt, :] = v
    acc = None
    for k in range(taps):
        src = v if k == taps - 1 else scr[:, HALO - (taps - 1) + k:HALO - (taps - 1) + k + tt, :]
        term = w_ref[k:k + 1, :] * src
        acc = term if acc is None else acc + term
    return acc


def _block_scan_real(a, b):
    row = lax.broadcasted_iota(jnp.int32, (1, SUBLANES, a.shape[-1]), 1)
    for d in (1, 2, 4):
        keep = row >= d
        a_sh = pltpu.roll(a, d, 1)
        b_sh = pltpu.roll(b, d, 1)
        b = b + a * jnp.where(keep, b_sh, 0.0)
        a = a * jnp.where(keep, a_sh, 1.0)
    return a, b


def _block_scan_cplx(br, bi, tab_ref, base, sl):
    for lvl, d in enumerate((1, 2, 4)):
        lr = tab_ref[0, base + 2 * lvl, :, sl]
        li = tab_ref[0, base + 2 * lvl + 1, :, sl]
        sr = pltpu.roll(br, d, 1)
        si = pltpu.roll(bi, d, 1)
        br, bi = br + (lr * sr - li * si), bi + (lr * si + li * sr)
    return br, bi


def _spread_rows(scr, nblk, bb):
    return jnp.concatenate(
        [jnp.broadcast_to(scr[:, k:k + 1, :], (bb, SUBLANES, LANES)) for k in range(nblk)],
        axis=1)


def _mixer_a_kernel(ab_ref, ac_ref, ax_ref, ag_ref, w_ref, b_ref, st_ref,
                    o_ref, st_out_ref, scr, *, tt):
    keep = CONV_A - 1

    @pl.when(pl.program_id(2) == 0)
    def _():
        scr[:, HALO - keep:HALO, :] = st_ref[...]

    conv_in = ac_ref[...] * ax_ref[...]
    y = b_ref[...] + _causal_taps(scr, conv_in, w_ref, CONV_A, tt)
    out = (ab_ref[...] * y) * jax.nn.silu(ag_ref[...])
    o_ref[...] = out.reshape(o_ref.shape).astype(o_ref.dtype)
    tail = scr[:, HALO + tt - keep:HALO + tt, :]
    st_out_ref[...] = tail
    scr[:, HALO - keep:HALO, :] = tail


def _mixer_a(proj3, w_conv, b_conv, state, bb, tt):
    b, t, _ = proj3.shape
    cw = CH_TILE
    nc = W_A // cw
    nt = t // tt
    blk = (bb, tt, cw)

    def col(off):
        return pl.BlockSpec(blk, lambda i, c, j, off=off: (i, j, off // cw + c))

    vmem = 2 * (4 * _nbytes(blk, F32) + _nbytes(blk, BF16)) \
        + _nbytes((bb, HALO + tt, cw), F32) + 6 * _nbytes(blk, F32)
    return pl.pallas_call(
        functools.partial(_mixer_a_kernel, tt=tt),
        grid=(b // bb, nc, nt),
        in_specs=[
            col(0), col(W_A), col(2 * W_A), col(3 * W_A),
            pl.BlockSpec((CONV_A, cw), lambda i, c, j: (0, c)),
            pl.BlockSpec((1, cw), lambda i, c, j: (0, c)),
            pl.BlockSpec((bb, CONV_A - 1, cw), lambda i, c, j: (i, 0, c)),
        ],
        out_specs=[
            pl.BlockSpec((bb * tt, cw), lambda i, c, j: (i * nt + j, c)),
            pl.BlockSpec((bb, CONV_A - 1, cw), lambda i, c, j: (i, 0, c)),
        ],
        out_shape=[
            jax.ShapeDtypeStruct((b * t, W_A), BF16),
            jax.ShapeDtypeStruct((b, CONV_A - 1, W_A), F32),
        ],
        scratch_shapes=[pltpu.VMEM((bb, HALO + tt, cw), F32)],
        compiler_params=_params(("arbitrary",) * 3, vmem),
        name="mixer_a",
    )(proj3, proj3, proj3, proj3, w_conv, b_conv.reshape(1, W_A), state)


MIXER_B_INS = 10
MIXER_B_OUTS = 3


def _mixer_b_init(cst_ref, hst_ref, scr, carry, first_time_tile):
    @pl.when(first_time_tile)
    def _():
        scr[:, HALO - (CONV_B - 1):HALO, :] = cst_ref[...]
        carry[...] = hst_ref[...]


def _mixer_b_kernel(*refs, bb, tt):
    ins = refs[:MIXER_B_INS]
    outs = refs[MIXER_B_INS:MIXER_B_INS + MIXER_B_OUTS]
    scr, carry, *scan_scr = refs[MIXER_B_INS + MIXER_B_OUTS:]
    _mixer_b_init(ins[8], ins[9], scr, carry, pl.program_id(2) == 0)
    _mixer_b_step(*ins[:8], *outs, scr, carry, scan_scr, bb=bb, tt=tt)


def _mixer_b_step(bx_ref, bg_ref, w_ref, b_ref, wg_ref, brg_ref, big_ref, lam_ref,
                  o_ref, cst_out_ref, hst_out_ref, scr, carry, scan_scr, *, bb, tt,
                  side_matmul=None):
    keep = CONV_B - 1
    cw = CH_TILE
    rows = bb * tt
    nblk = tt // SUBLANES
    nv = nblk // SUBLANES

    scr[:, HALO:HALO + tt, :] = bx_ref[...]
    cst_out_ref[...] = scr[:, HALO + tt - keep:HALO + tt, :]
    softplus_neg_lam = jax.nn.softplus(-lam_ref[...])
    cin_all = carry[...]
    h_parts, last_parts = [], []
    assert LRU_HD == LANES
    n_heads = cw // LRU_HD
    n_chunks = 1 if side_matmul is None else rows // GATE_CHUNK_ROWS
    rc = rows // n_chunks
    for s in range(n_heads):
        sl = slice(s * LANES, (s + 1) * LANES)
        w_head = jnp.concatenate([wg_ref[0, sl, sl], wg_ref[0, sl, cw + s * LANES:cw + (s + 1) * LANES]],
                                 axis=1)
        a_chunks, b_chunks = [], []
        for k in range(n_chunks):
            acc = None
            for j in range(CONV_B):
                lo = HALO - keep + j
                if tt >= rc:
                    b0, t0 = divmod(k * rc, tt)
                    win = scr[b0:b0 + 1, lo + t0:lo + t0 + rc, sl]
                else:
                    nb = rc // tt
                    win = scr[k * nb:(k + 1) * nb, lo:lo + tt, sl]
                term = w_ref[j:j + 1, sl] * win.reshape(rc, LANES)
                acc = term if acc is None else acc + term
            xb = b_ref[:, sl] + acc
            gates = jnp.dot(xb.astype(BF16), w_head, preferred_element_type=F32)
            if side_matmul is not None:
                side_matmul(s * n_chunks + k, n_heads * n_chunks)
            r = jax.nn.sigmoid(gates[:, :LANES] + brg_ref[:, sl])
            ig = jax.nn.sigmoid(gates[:, LANES:] + big_ref[:, sl])
            log_a = (-LRU_C * r) * softplus_neg_lam[:, sl]
            a = jnp.exp(log_a)
            beta = jnp.sqrt(-jnp.tanh(log_a) * (a * a + 1.0))
            bt = (beta * ig) * xb
            a_k, b_k = _block_scan_real(a.reshape(rc // SUBLANES, SUBLANES, LANES),
                                        bt.reshape(rc // SUBLANES, SUBLANES, LANES))
            a_chunks.append(a_k)
            b_chunks.append(b_k)
        a_blk = jnp.concatenate(a_chunks, axis=0)
        b_blk = jnp.concatenate(b_chunks, axis=0)
        cin = cin_all[:, :, sl]
        if nblk == 1:
            h = b_blk + a_blk * cin
            last = h[:, SUBLANES - 1:SUBLANES, :]
        else:
            acum, bcum, cscr = scan_scr
            acum[s] = a_blk.reshape(bb, tt, LANES)
            bcum[s] = b_blk.reshape(bb, tt, LANES)
            a2 = acum[s, :, pl.ds(SUBLANES - 1, nblk, stride=SUBLANES), :]
            b2 = bcum[s, :, pl.ds(SUBLANES - 1, nblk, stride=SUBLANES), :]
            a2, b2 = _block_scan_real(a2.reshape(bb * nv, SUBLANES, LANES),
                                      b2.reshape(bb * nv, SUBLANES, LANES))
            a2 = a2.reshape(bb, nblk, LANES)
            b2 = b2.reshape(bb, nblk, LANES)
            prev = cin
            ends = []
            for j in range(nv):
                seg = slice(j * SUBLANES, (j + 1) * SUBLANES)
                e = b2[:, seg, :] + a2[:, seg, :] * prev
                ends.append(e)
                prev = e[:, SUBLANES - 1:SUBLANES, :]
            last = prev
            cscr[s, :, 0:1, :] = cin
            cscr[s, :, 1:1 + nblk, :] = jnp.concatenate(ends, axis=1)
            spread = _spread_rows(cscr.at[s], nblk, bb)
            h = b_blk + a_blk * spread.reshape(bb * nblk, SUBLANES, LANES)
        h_parts.append(h.reshape(bb, tt, LANES))
        last_parts.append(last)
    h = jnp.concatenate(h_parts, axis=-1)
    last = jnp.concatenate(last_parts, axis=-1)
    carry[...] = last
    hst_out_ref[...] = last
    out = h * jax.nn.silu(bg_ref[...])
    o_ref[...] = out.reshape(o_ref.shape).astype(o_ref.dtype)
    scr[:, HALO - keep:HALO, :] = scr[:, HALO + tt - keep:HALO + tt, :]


def _mixer_b_operands(proj3, w_conv, b_conv, w_gates, b_rg, b_ig, lam, conv_state, h_state,
                      bb, tt, idx):
    b, t, _ = proj3.shape
    cw = CH_TILE
    nt = t // tt
    nblk = tt // SUBLANES
    blk = (bb, tt, cw)

    def spec(shape, fn):
        return pl.BlockSpec(shape, lambda *g: fn(*idx(*g)))

    vec = spec((1, cw), lambda i, c, j: (0, c))
    scratch = [pltpu.VMEM((bb, HALO + tt, cw), F32), pltpu.VMEM((bb, 1, cw), F32)]
    if nblk > 1:
        assert nblk % SUBLANES == 0
        slabs = cw // LANES
        scratch += [pltpu.VMEM((slabs, bb, tt, LANES), F32),
                    pltpu.VMEM((slabs, bb, tt, LANES), F32),
                    pltpu.VMEM((slabs, bb, HALO + nblk, LANES), F32)]
    return dict(
        args=[proj3, proj3, w_conv, b_conv.reshape(1, W_B), w_gates,
              b_rg.reshape(1, W_B), b_ig.reshape(1, W_B), lam.reshape(1, W_B),
              conv_state, h_state],
        in_specs=[
            spec(blk, lambda i, c, j: (i, j, c)),
            spec(blk, lambda i, c, j: (i, j, W_B // cw + c)),
            spec((CONV_B, cw), lambda i, c, j: (0, c)),
            vec,
            spec((1, cw, 2 * cw), lambda i, c, j: (c, 0, 0)),
            vec, vec, vec,
            spec((bb, CONV_B - 1, cw), lambda i, c, j: (i, 0, c)),
            spec((bb, 1, cw), lambda i, c, j: (i, 0, c)),
        ],
        out_specs=[
            spec((bb * tt, cw), lambda i, c, j: (i * nt + j, c)),
            spec((bb, CONV_B - 1, cw), lambda i, c, j: (i, 0, c)),
            spec((bb, 1, cw), lambda i, c, j: (i, 0, c)),
        ],
        out_shape=[
            jax.ShapeDtypeStruct((b * t, W_B), BF16),
            jax.ShapeDtypeStruct((b, CONV_B - 1, W_B), F32),
            jax.ShapeDtypeStruct((b, 1, W_B), F32),
        ],
        scratch=scratch,
        vmem=2 * (2 * _nbytes(blk, F32) + _nbytes(blk, BF16) + _nbytes((cw, 2 * cw), BF16))
        + _nbytes((bb, HALO + tt, cw), F32) + 16 * _nbytes(blk, F32),
        grid=(b // bb, W_B // cw, nt),
    )


def _mixer_b(proj3, *params, bb, tt):
    ops = _mixer_b_operands(proj3, *params, bb, tt, lambda i, c, j: (i, c, j))
    return pl.pallas_call(
        functools.partial(_mixer_b_kernel, bb=bb, tt=tt),
        grid=ops["grid"],
        in_specs=ops["in_specs"],
        out_specs=ops["out_specs"],
        out_shape=ops["out_shape"],
        scratch_shapes=ops["scratch"],
        compiler_params=_params(("arbitrary",) * 3, ops["vmem"]),
        name="mixer_b",
    )(*ops["args"])


def _fused_proj_kernel(*refs, mixers, ni):
    a_ref, w_ref = refs[:2]
    mixers = [dict(mx) for mx in mixers]
    pos = 2
    for mx in mixers:
        mx["ins"] = refs[pos:pos + mx["n_ins"]]
        pos += mx["n_ins"]
    proj_ref = refs[pos]
    pos += 1
    for mx in mixers:
        mx["outs"] = refs[pos:pos + mx["n_outs"]]
        pos += mx["n_outs"]
    wb_ref = refs[pos]
    pos += 1
    for mx in mixers:
        mx["scr"] = refs[pos:pos + mx["n_scr"]]
        pos += mx["n_scr"]
    s = pl.program_id(0)
    _cast_weight_tile(w_ref, wb_ref, lax.rem(s, ni) == 0)
    for mx in mixers:
        mx["init"](mx["ins"], mx["scr"], s)
    k_total = a_ref.shape[1]
    acc = []

    def projection_part(part, n_parts):
        kc = MXU_DIM
        per_part = k_total // (kc * n_parts)
        assert per_part * kc * n_parts == k_total
        for sub in range(part * per_part, (part + 1) * per_part):
            ks = slice(sub * kc, (sub + 1) * kc)
            term = jnp.dot(a_ref[:, ks], wb_ref[ks, :], preferred_element_type=F32)
            acc[:] = [term if not acc else acc[0] + term]
        if part == n_parts - 1:
            proj_ref[...] = acc[0]

    for mx in mixers[:-1]:
        mx["step"](mx["ins"], mx["outs"], mx["scr"], s, None)
    mx = mixers[-1]
    mx["step"](mx["ins"], mx["outs"], mx["scr"], s, projection_part)


def _fused_proj_call(name, h2d, w_in, layer, col0, ncols, mixers):
    steps = None
    for ops, _, _ in mixers:
        n = 1
        for g in ops["grid"]:
            n *= g
        assert steps in (None, n)
        steps = n
    nj = ncols // PROJ_TN
    ni = steps // nj
    tm = h2d.shape[0] // ni
    assert nj * ni == steps and tm * ni == h2d.shape[0] and tm % SUBLANES == 0
    mm = _in_proj_operands(h2d, w_in, layer, col0, ncols, tm, lambda s: (s // ni, s % ni))
    specs = [dict(n_ins=len(ops["args"]), n_outs=len(ops["out_shape"]),
                  n_scr=len(ops["scratch"]), init=init, step=step)
             for ops, init, step in mixers]
    cat = lambda key: [x for ops, _, _ in mixers for x in ops[key]]
    return pl.pallas_call(
        functools.partial(_fused_proj_kernel, mixers=specs, ni=ni),
        grid=(steps,),
        in_specs=mm["in_specs"] + cat("in_specs"),
        out_specs=mm["out_specs"] + cat("out_specs"),
        out_shape=mm["out_shape"] + cat("out_shape"),
        scratch_shapes=mm["scratch"] + cat("scratch"),
        compiler_params=_params(("arbitrary",),
                                mm["vmem"] + sum(ops["vmem"] for ops, _, _ in mixers)),
        name=name,
    )(*mm["args"], *cat("args"))


def _flat_index(dims):
    def idx(s):
        out = []
        for k, d in enumerate(dims):
            stride = 1
            for e in dims[k + 1:]:
                stride *= e
            out.append((s // stride) % d if k else s // stride)
        return tuple(out)
    return idx


def _flat_mixer_b(proj3, *params, bb, tt):
    b, t, _ = proj3.shape
    dims = (b // bb, W_B // CH_TILE, t // tt)
    ops = _mixer_b_operands(proj3, *params, bb, tt, _flat_index(dims))
    nt = dims[2]

    def init(ins, scr, s):
        _mixer_b_init(ins[8], ins[9], scr[0], scr[1], lax.rem(s, nt) == 0)

    def step(ins, outs, scr, s, side_matmul):
        _mixer_b_step(*ins[:8], *outs, scr[0], scr[1], scr[2:], bb=bb, tt=tt,
                      side_matmul=side_matmul)

    return ops, init, step


def _s5_disc_kernel(lre_ref, lim_ref, ldt_ref, btre_ref, btim_ref,
                    abre_ref, abim_ref, bbre_ref, bbim_ref):
    lre = lre_ref[0]
    lim = lim_ref[0]
    dt = jnp.exp(ldt_ref[0])
    mag = jnp.exp(lre * dt)
    ab_re = mag * jnp.cos(lim * dt)
    ab_im = mag * jnp.sin(lim * dt)
    nr = ab_re - 1.0
    den = lre * lre + lim * lim
    fr = (nr * lre + ab_im * lim) / den
    fi = (ab_im * lre - nr * lim) / den
    bre = btre_ref[0]
    bim = btim_ref[0]
    bbre_ref[0] = fr * bre - fi * bim
    bbim_ref[0] = fr * bim + fi * bre
    abre_ref[0] = ab_re
    abim_ref[0] = ab_im


def _s5_discretize(lam_re, lam_im, log_dt, b_re, b_im):
    depth = lam_re.shape[0]
    gp = (depth, S5_G, 1, S5_P)
    ldt = jnp.broadcast_to(log_dt[:, :, None, None], gp)
    bt_shape = (depth, S5_G, S5_GROUP, S5_P)
    small = pl.BlockSpec((1, S5_G, 1, S5_P), lambda l: (l, 0, 0, 0))
    big = pl.BlockSpec((1, S5_G, S5_GROUP, S5_P), lambda l: (l, 0, 0, 0))
    ab_re, ab_im, bb_re, bb_im = pl.pallas_call(
        _s5_disc_kernel,
        grid=(depth,),
        in_specs=[small, small, small, big, big],
        out_specs=[small, small, big, big],
        out_shape=[jax.ShapeDtypeStruct(gp, F32), jax.ShapeDtypeStruct(gp, F32),
                   jax.ShapeDtypeStruct(bt_shape, F32), jax.ShapeDtypeStruct(bt_shape, F32)],
        name="s5_discretize",
    )(lam_re.reshape(gp), lam_im.reshape(gp), ldt,
      jnp.swapaxes(b_re, 2, 3), jnp.swapaxes(b_im, 2, 3))
    return (ab_re.reshape(depth, 1, S5_STATES), ab_im.reshape(depth, 1, S5_STATES),
            bb_re, bb_im)


def _s5_pow_kernel(ar_ref, ai_ref, tab_ref):
    def cmul(x, y):
        return x[0] * y[0] - x[1] * y[1], x[0] * y[1] + x[1] * y[0]

    def powers(b1):
        b2 = cmul(b1, b1)
        b3 = cmul(b2, b1)
        b4 = cmul(b2, b2)
        return (b1, b2, b3, b4, cmul(b4, b1), cmul(b4, b2), cmul(b4, b3), cmul(b4, b4))

    shape = (SUBLANES, S5_STATES)
    row = lax.broadcasted_iota(jnp.int32, shape, 0)
    zero = jnp.zeros(shape, F32)
    pw_a = powers((ar_ref[0], ai_ref[0]))
    pw_a8 = powers(pw_a[SUBLANES - 1])
    for base, pw in ((0, pw_a), (N_TABLES // 2, pw_a8)):
        for part in (0, 1):
            tab_ref[0, base + part] = jnp.where(row >= 1, pw[0][part], zero)
            tab_ref[0, base + 2 + part] = jnp.where(row >= 2, pw[1][part], zero)
            tab_ref[0, base + 4 + part] = jnp.where(row >= 4, pw[3][part], zero)
            p = zero
            for r, v in enumerate(pw):
                p = jnp.where(row == r, v[part], p)
            tab_ref[0, base + 6 + part] = p


def _s5_power_tables(ab_re, ab_im):
    depth = ab_re.shape[0]
    vec = pl.BlockSpec((1, 1, S5_STATES), lambda l: (l, 0, 0))
    return pl.pallas_call(
        _s5_pow_kernel,
        grid=(depth,),
        in_specs=[vec, vec],
        out_specs=pl.BlockSpec((1, N_TABLES, SUBLANES, S5_STATES), lambda l: (l, 0, 0, 0)),
        out_shape=jax.ShapeDtypeStruct((depth, N_TABLES, SUBLANES, S5_STATES), F32),
        name="s5_power_tables",
    )(ab_re, ab_im)


def _s5_block_matrices(bb_re, bb_im, c_re, c_im):
    depth = bb_re.shape[0]
    gpt = CH_TILE // S5_GROUP
    chan = lax.broadcasted_iota(jnp.int32, (S5_Q, CH_TILE, CH_TILE), 1)
    state = lax.broadcasted_iota(jnp.int32, (S5_Q, CH_TILE, CH_TILE), 2)
    slab = lax.broadcasted_iota(jnp.int32, (S5_Q, CH_TILE, CH_TILE), 0)
    gps = LANES // S5_P
    keep = (chan // S5_GROUP) == gps * slab + (state % LANES) // S5_P

    def dense_b(x):
        return x.reshape(depth * S5_KT, gpt * S5_GROUP, S5_P)

    db = jnp.concatenate([dense_b(bb_re)] * gps + [dense_b(bb_im)] * gps, axis=-1)
    bm = jnp.where(keep[None], db[:, None], 0.0).astype(BF16)

    def dense_c(x):
        x = x.reshape(depth * S5_KT, gpt, S5_GROUP, S5_P)
        return jnp.transpose(x, (0, 3, 1, 2)).reshape(depth * S5_KT, S5_P, gpt * S5_GROUP)

    dc = jnp.concatenate([dense_c(c_re)] * gps + [dense_c(-c_im)] * gps, axis=1)
    keep_c = jnp.swapaxes(keep, 1, 2)
    cm = jnp.where(keep_c[None], dc[:, None], 0.0).astype(BF16)
    return bm, cm


MIXER_C_INS = 10
MIXER_C_OUTS = 3


def _mixer_c_init(sre_ref, sim_ref, car_r, car_i, kt, first_time_tile):
    @pl.when(first_time_tile)
    def _():
        car_r[kt] = sre_ref[...]
        car_i[kt] = sim_ref[...]


def _mixer_c_kernel(*refs, bb, tt):
    ins = refs[:MIXER_C_INS]
    outs = refs[MIXER_C_INS:MIXER_C_INS + MIXER_C_OUTS]
    ystore, car_r, car_i, *scan_scr = refs[MIXER_C_INS + MIXER_C_OUTS:]
    kt = pl.program_id(2)
    _mixer_c_init(ins[8], ins[9], car_r, car_i, kt, pl.program_id(1) == 0)
    _mixer_c_step(*ins[:8], *outs, ystore, car_r, car_i, scan_scr, kt, bb=bb, tt=tt)


def _mixer_c_step(cu_ref, cg_ref, bm_ref, cm_ref, tab_ref, d_ref, wglu_ref, bglu_ref,
                  o_ref, sre_out_ref, sim_out_ref,
                  ystore, car_r, car_i, scan_scr, kt, *, bb, tt, side_matmul=None):
    cw = CH_TILE
    rows = bb * tt
    nblk = tt // SUBLANES
    nv = nblk // SUBLANES
    n = bb * nblk
    base8 = N_TABLES // 2

    u = cu_ref[...].reshape(rows, cw)
    ub = u.astype(BF16)
    y = d_ref[...] * u
    cr_all = car_r[kt]
    ci_all = car_i[kt]
    new_r, new_i = [], []
    for q in range(S5_Q):
        sl = slice(LANES * q, LANES * (q + 1))
        bu = jnp.dot(ub, bm_ref[0, q], preferred_element_type=F32)
        if side_matmul is not None:
            side_matmul(q, S5_Q)
        br, bi = _block_scan_cplx(bu[:, :LANES].reshape(n, SUBLANES, LANES),
                                  bu[:, LANES:].reshape(n, SUBLANES, LANES), tab_ref, 0, sl)
        p_r = tab_ref[0, 6, :, sl]
        p_i = tab_ref[0, 7, :, sl]
        cr = cr_all[:, :, sl]
        ci = ci_all[:, :, sl]
        if nblk == 1:
            kr, ki = cr, ci
        else:
            hsr, hsi, csr, csi = scan_scr
            hsr[...] = br.reshape(bb, tt, LANES)
            hsi[...] = bi.reshape(bb, tt, LANES)
            xr = hsr[:, pl.ds(SUBLANES - 1, nblk, stride=SUBLANES), :]
            xi = hsi[:, pl.ds(SUBLANES - 1, nblk, stride=SUBLANES), :]
            xr, xi = _block_scan_cplx(xr.reshape(bb * nv, SUBLANES, LANES),
                                      xi.reshape(bb * nv, SUBLANES, LANES), tab_ref, base8, sl)
            xr = xr.reshape(bb, nblk, LANES)
            xi = xi.reshape(bb, nblk, LANES)
            q_r = tab_ref[0, base8 + 6, :, sl]
            q_i = tab_ref[0, base8 + 7, :, sl]
            prev_r, prev_i = cr, ci
            ends_r, ends_i = [], []
            for j in range(nv):
                seg = slice(j * SUBLANES, (j + 1) * SUBLANES)
                er = xr[:, seg, :] + (q_r * prev_r - q_i * prev_i)
                ei = xi[:, seg, :] + (q_r * prev_i + q_i * prev_r)
                ends_r.append(er)
                ends_i.append(ei)
                prev_r = er[:, SUBLANES - 1:SUBLANES, :]
                prev_i = ei[:, SUBLANES - 1:SUBLANES, :]
            csr[:, 0:1, :] = cr
            csi[:, 0:1, :] = ci
            csr[:, 1:1 + nblk, :] = jnp.concatenate(ends_r, axis=1)
            csi[:, 1:1 + nblk, :] = jnp.concatenate(ends_i, axis=1)
            kr = _spread_rows(csr, nblk, bb).reshape(n, SUBLANES, LANES)
            ki = _spread_rows(csi, nblk, bb).reshape(n, SUBLANES, LANES)
        hr = br + (p_r * kr - p_i * ki)
        hi = bi + (p_r * ki + p_i * kr)
        hr = hr.reshape(bb, tt, LANES)
        hi = hi.reshape(bb, tt, LANES)
        new_r.append(hr[:, tt - 1:tt, :])
        new_i.append(hi[:, tt - 1:tt, :])
        hcat = jnp.concatenate([hr.reshape(rows, LANES), hi.reshape(rows, LANES)], axis=-1)
        y = y + jnp.dot(hcat.astype(BF16), cm_ref[0, q], preferred_element_type=F32)

    ncr_all = jnp.concatenate(new_r, axis=-1)
    nci_all = jnp.concatenate(new_i, axis=-1)
    car_r[kt] = ncr_all
    car_i[kt] = nci_all
    sre_out_ref[:, kt] = ncr_all
    sim_out_ref[:, kt] = nci_all
    ystore[kt] = y

    @pl.when(kt == S5_KT - 1)
    def _():
        y_all = jnp.concatenate([ystore[k] for k in range(S5_KT)], axis=-1)
        yg = jax.nn.gelu(y_all)
        z = jnp.dot(yg.astype(BF16), wglu_ref[0].astype(BF16),
                    preferred_element_type=F32) + bglu_ref[...]
        yy = yg * jax.nn.sigmoid(z)
        out = yy * jax.nn.silu(cg_ref[...].reshape(rows, W_C))
        o_ref[...] = out.astype(o_ref.dtype)


def _mixer_c_operands(proj3, bmat, cmat, tables, d_skip, w_glu, b_glu, s_re, s_im, layer,
                      bb, tt, idx):
    b, t, _ = proj3.shape
    u_col = 0
    cw = CH_TILE
    nt = t // tt
    nblk = tt // SUBLANES
    kslab = S5_Q * LANES

    def spec(shape, fn):
        return pl.BlockSpec(shape, lambda *g: fn(*idx(*g)))

    mat = spec((1, S5_Q, cw, cw), lambda i, j, k: (layer * S5_KT + k, 0, 0, 0))
    st = spec((bb, 1, kslab), lambda i, j, k: (i, 0, k))
    st_out = spec((bb, S5_KT, 1, kslab), lambda i, j, k: (i, 0, 0, 0))
    rows = bb * tt
    scratch = [pltpu.VMEM((S5_KT, rows, cw), F32),
               pltpu.VMEM((S5_KT, bb, 1, kslab), F32),
               pltpu.VMEM((S5_KT, bb, 1, kslab), F32)]
    if nblk > 1:
        assert nblk % SUBLANES == 0 and nblk // SUBLANES <= SUBLANES
        scratch += [pltpu.VMEM((bb, tt, LANES), F32), pltpu.VMEM((bb, tt, LANES), F32),
                    pltpu.VMEM((bb, HALO + nblk, LANES), F32),
                    pltpu.VMEM((bb, HALO + nblk, LANES), F32)]
    vmem = 2 * (_nbytes((bb, tt, cw), F32) + _nbytes((bb, tt, W_C), F32)
                + 2 * _nbytes((S5_Q, cw, cw), BF16)
                + _nbytes((N_TABLES, SUBLANES, kslab), F32)
                + _nbytes((W_C, W_C), F32) + _nbytes((bb, tt, W_C), BF16)
                + 4 * _nbytes((bb, SUBLANES, kslab), F32)) \
        + _nbytes((S5_KT, rows, cw), F32) + 2 * _nbytes((bb, tt, LANES), F32) \
        + 2 * _nbytes((S5_KT, bb, SUBLANES, kslab), F32) + 10 * _nbytes((rows, W_C), F32)
    return dict(
        args=[proj3, proj3, bmat, cmat, tables, d_skip.reshape(1, W_C), w_glu,
              b_glu.reshape(1, W_C), s_re, s_im],
        in_specs=[
            spec((bb, tt, cw), lambda i, j, k: (i, j, u_col // cw + k)),
            spec((bb, tt, W_C), lambda i, j, k: (i, j, u_col // W_C + 1)),
            mat, mat,
            spec((1, N_TABLES, SUBLANES, kslab), lambda i, j, k: (layer, 0, 0, k)),
            spec((1, cw), lambda i, j, k: (0, k)),
            spec((1, W_C, W_C), lambda i, j, k: (layer, 0, 0)),
            spec((1, W_C), lambda i, j, k: (0, 0)),
            st, st,
        ],
        out_specs=[
            spec((rows, W_C), lambda i, j, k: (i * nt + j, 0)),
            st_out, st_out,
        ],
        out_shape=[
            jax.ShapeDtypeStruct((b * t, W_C), BF16),
            jax.ShapeDtypeStruct((b, S5_KT, 1, kslab), F32),
            jax.ShapeDtypeStruct((b, S5_KT, 1, kslab), F32),
        ],
        scratch=scratch,
        vmem=vmem,
        grid=(b // bb, nt, S5_KT),
    )


def _mixer_c(proj3, *params, bb, tt):
    ops = _mixer_c_operands(proj3, *params, bb, tt, lambda i, j, k: (i, j, k))
    return pl.pallas_call(
        functools.partial(_mixer_c_kernel, bb=bb, tt=tt),
        grid=ops["grid"],
        in_specs=ops["in_specs"],
        out_specs=ops["out_specs"],
        out_shape=ops["out_shape"],
        scratch_shapes=ops["scratch"],
        compiler_params=_params(("arbitrary",) * 3, ops["vmem"]),
        name="mixer_c",
    )(*ops["args"])


def _flat_mixer_c(proj3, *params, bb, tt):
    b, t, _ = proj3.shape
    dims = (b // bb, t // tt, S5_KT)
    ops = _mixer_c_operands(proj3, *params, bb, tt, _flat_index(dims))
    nt = dims[1]

    def init(ins, scr, s):
        _mixer_c_init(ins[8], ins[9], scr[1], scr[2], lax.rem(s, S5_KT),
                      lax.rem(s // S5_KT, nt) == 0)

    def step(ins, outs, scr, s, side_matmul):
        _mixer_c_step(*ins[:8], *outs, scr[0], scr[1], scr[2], scr[3:],
                      lax.rem(s, S5_KT), bb=bb, tt=tt, side_matmul=side_matmul)

    return ops, init, step


def _out_proj_kernel(ma_ref, mb_ref, mc_ref, w_ref, x_ref, gate_ref, o_ref, *, bb, tt):
    w = w_ref[0].astype(BF16)
    acc = jnp.dot(ma_ref[...], w[0:W_A, :], preferred_element_type=F32)
    acc = acc + jnp.dot(mb_ref[...], w[W_A:W_A + W_B, :], preferred_element_type=F32)
    acc = acc + jnp.dot(mc_ref[...], w[W_A + W_B:, :], preferred_element_type=F32)
    o_ref[...] = x_ref[...] + gate_ref[...] * acc.reshape(bb, tt, acc.shape[-1])


def _out_projection(out_a, out_b, out_c, w_out, x, gate, layer, bb, tt):
    b, t, d = x.shape
    tm = bb * tt
    tn = 512
    nt = t // tt
    m = b * t

    def rows(width):
        return pl.BlockSpec((tm, width), lambda i, j: (i, 0))

    vmem = 2 * (_nbytes((tm, d), BF16) + _nbytes((d, tn), F32)
                + 2 * _nbytes((tm, tn), F32)) + _nbytes((d, tn), BF16) \
        + 3 * _nbytes((tm, tn), F32)
    return pl.pallas_call(
        functools.partial(_out_proj_kernel, bb=bb, tt=tt),
        grid=(m // tm, d // tn),
        in_specs=[
            rows(W_A), rows(W_B), rows(W_C),
            pl.BlockSpec((1, d, tn), lambda i, j: (layer, 0, j)),
            pl.BlockSpec((bb, tt, tn), lambda i, j: (i // nt, i % nt, j)),
            pl.BlockSpec((bb, 1, tn), lambda i, j: (i // nt, 0, j)),
        ],
        out_specs=pl.BlockSpec((bb, tt, tn), lambda i, j: (i // nt, i % nt, j)),
        out_shape=jax.ShapeDtypeStruct((b, t, d), F32),
        compiler_params=_params(("arbitrary", "arbitrary"), vmem + (4 << 20)),
        name="out_projection",
    )(out_a, out_b, out_c, w_out, x, gate)


def _gate_weights(w_rg, w_ig):
    hpt = CH_TILE // LRU_HD
    pairs = LRU_HEADS // hpt
    w = jnp.stack([w_rg, w_ig], axis=2)
    w = w.reshape(pairs, hpt, LRU_HD, 2, LRU_HD)
    eye = jnp.eye(hpt, dtype=bool)
    out = jnp.where(eye[None, :, None, None, :, None],
                    w[:, :, :, :, None, :], 0.0)
    return out.reshape(pairs, CH_TILE, 2 * CH_TILE).astype(BF16)


def _layer(x, mod, states, wts, layer, tiles):
    b, t, d = x.shape
    conv_a, conv_b, lru_h, s5_re, s5_im = states
    shift, scale, gate = (mod[:, None, k * d:(k + 1) * d] for k in range(3))

    h = _norm_modulate(x, scale, shift, wts["g_norm"], *tiles["norm"])
    w_in = wts["w_in"]
    col_a, col_b, col_c = 0, 4 * W_A, 4 * W_A + 2 * W_B
    b_params = (wts["w_conv_b"], wts["b_conv_b"], wts["w_gates"], wts["b_rg"], wts["b_ig"],
                wts["lru_lambda"], conv_b, lru_h.reshape(b, 1, W_B))
    c_params = (wts["bmat"], wts["cmat"], wts["tables"], wts["s5_d"], wts["w_glu"],
                wts["b_glu"], s5_re.reshape(b, 1, S5_STATES), s5_im.reshape(b, 1, S5_STATES),
                layer)
    bb_b, tt_b = tiles["b"]
    bb_c, tt_c = tiles["c"]

    proj_c = _in_projection(h, w_in, layer, col_c, 2 * W_C).reshape(b, t, 2 * W_C)
    if tiles["fuse"]:
        proj_b, out_c, re_new, im_new = _fused_proj_call(
            "proj_b_mixer_c", h, w_in, layer, col_b, 2 * W_B,
            [_flat_mixer_c(proj_c, *c_params, bb=bb_c, tt=tt_c)])
        proj_b = proj_b.reshape(b, t, 2 * W_B)
        proj_a, out_b, conv_b_new, lru_new = _fused_proj_call(
            "proj_a_mixer_b", h, w_in, layer, col_a, 4 * W_A,
            [_flat_mixer_b(proj_b, *b_params, bb=bb_b, tt=tt_b)])
    else:
        out_c, re_new, im_new = _mixer_c(proj_c, *c_params, bb=bb_c, tt=tt_c)
        proj_b = _in_projection(h, w_in, layer, col_b, 2 * W_B).reshape(b, t, 2 * W_B)
        out_b, conv_b_new, lru_new = _mixer_b(proj_b, *b_params, bb=bb_b, tt=tt_b)
        proj_a = _in_projection(h, w_in, layer, col_a, 4 * W_A)
    proj_a = proj_a.reshape(b, t, 4 * W_A)
    out_a, conv_a_new = _mixer_a(proj_a, wts["w_conv_a"], wts["b_conv_a"], conv_a,
                                 *tiles["a"])

    x_new = _out_projection(out_a, out_b, out_c, wts["w_out"], x, gate, layer, *tiles["out"])
    return (x_new, conv_a_new, conv_b_new, lru_new.reshape(b, W_B),
            re_new.reshape(b, S5_G, S5_P), im_new.reshape(b, S5_G, S5_P))


PROMPT_TILES = {"norm": (1, 512), "a": (1, 1024), "b": (1, 1024), "c": (1, 256),
                "out": (1, 1024), "final": (1, 512), "fuse": True}
SAMPLE_TILES = {"norm": (32, 8), "a": (128, 8), "b": (64, 8), "c": (32, 8),
                "out": (128, 8), "final": (32, 8), "fuse": False}


def kernel(x_prompt, x_sample, c_prompt, c_sample, state_conv_a, state_conv_b, state_lru_h,
           state_s5_re, state_s5_im, g_norm, w_ada, b_ada, w_in, w_conv_a, b_conv_a,
           w_conv_b, b_conv_b, w_rg, b_rg, w_ig, b_ig, lru_lambda, s5_lambda_re,
           s5_lambda_im, s5_log_dt, s5_b_re, s5_b_im, s5_c_re, s5_c_im, s5_d, w_glu,
           b_glu, w_out, g_final):
    depth = w_in.shape[0]
    bp = x_prompt.shape[0]
    bs = x_sample.shape[0]

    c_all = jnp.concatenate([c_prompt, c_sample], axis=0)
    pad = (-c_all.shape[0]) % SUBLANES
    c_all = jnp.pad(c_all, ((0, pad), (0, 0)))
    mod = _modulation(c_all, w_ada, b_ada)

    ab_re, ab_im, bb_re, bb_im = _s5_discretize(s5_lambda_re, s5_lambda_im, s5_log_dt,
                                               s5_b_re, s5_b_im)
    tables = _s5_power_tables(ab_re, ab_im)
    bmat, cmat = _s5_block_matrices(bb_re, bb_im, s5_c_re, s5_c_im)
    w_glu_bf16 = w_glu.astype(BF16)

    xp, xs = x_prompt, x_sample
    zeros_p = (jnp.zeros((bp, CONV_A - 1, W_A), F32), jnp.zeros((bp, CONV_B - 1, W_B), F32),
               jnp.zeros((bp, W_B), F32), jnp.zeros((bp, S5_G, S5_P), F32),
               jnp.zeros((bp, S5_G, S5_P), F32))
    outs_p, outs_s = [], []
    for l in range(depth):
        wts = {
            "g_norm": g_norm[l], "w_in": w_in,
            "w_conv_a": w_conv_a[l], "b_conv_a": b_conv_a[l],
            "w_conv_b": w_conv_b[l], "b_conv_b": b_conv_b[l],
            "w_gates": _gate_weights(w_rg[l], w_ig[l]), "b_rg": b_rg[l], "b_ig": b_ig[l],
            "lru_lambda": lru_lambda[l], "bmat": bmat, "cmat": cmat,
            "tables": tables, "s5_d": s5_d[l],
            "w_glu": w_glu_bf16, "b_glu": b_glu[l], "w_out": w_out,
        }
        res_p = _layer(xp, mod[l, :bp], zeros_p, wts, l, PROMPT_TILES)
        xp = res_p[0]
        outs_p.append(res_p[1:])
        st_s = (state_conv_a[l], state_conv_b[l], state_lru_h[l], state_s5_re[l],
                state_s5_im[l])
        res_s = _layer(xs, mod[l, bp:bp + bs], st_s, wts, l, SAMPLE_TILES)
        xs = res_s[0]
        outs_s.append(res_s[1:])

    y_prompt = _final_norm(xp, g_final, *PROMPT_TILES["final"])
    y_sample = _final_norm(xs, g_final, *SAMPLE_TILES["final"])
    stack = lambda outs, k: jnp.stack([o[k] for o in outs])
    return (y_prompt, y_sample,
            *(stack(outs_p, k) for k in range(5)),
            *(stack(outs_s, k) for k in range(5)))
```

```python
import functools

import jax
import jax.numpy as jnp
from jax import lax
from jax.experimental import pallas as pl
from jax.experimental.pallas import tpu as pltpu

F32 = jnp.float32
BF16 = jnp.bfloat16

D_MODEL = 4096
W_A = D_MODEL // 4
W_B = D_MODEL // 2
W_C = D_MODEL // 4
IN_COLS = 4 * W_A + 2 * W_B + 2 * W_C
CONV_A = 3
CONV_B = 4
LRU_HEADS = 16
LRU_HD = W_B // LRU_HEADS
LRU_C = 8.0
S5_GROUP = 16
S5_G = W_C // S5_GROUP
S5_P = 64
S5_STATES = S5_G * S5_P
EPS = 1e-6

SUBLANES = 8
LANES = 128
MXU_DIM = 256
VMEM_LIMIT_CAP = 60000 * 1024
CH_TILE = MXU_DIM
S5_KT = W_C // CH_TILE
S5_Q = (CH_TILE // S5_GROUP) * S5_P // LANES
HALO = SUBLANES
N_TABLES = 16
GATE_CHUNK_ROWS = 128


def _params(sem, vmem_bytes):
    return pltpu.CompilerParams(
        dimension_semantics=sem,
        vmem_limit_bytes=int(min(VMEM_LIMIT_CAP, vmem_bytes)))


def _nbytes(shape, dtype):
    n = 1
    for s in shape:
        n *= s
    return n * jnp.dtype(dtype).itemsize


def _mod_kernel(c_ref, w_ref, b_ref, o_ref):
    c = c_ref[...].astype(BF16)
    w = w_ref[0].astype(BF16)
    o_ref[0] = jnp.dot(c, w, preferred_element_type=F32) + b_ref[0]


def _modulation(c_all, w_ada, b_ada):
    depth, d, n = w_ada.shape
    rows = c_all.shape[0]
    tn = 1024
    vmem = 2 * (_nbytes((rows, d), F32) + _nbytes((d, tn), F32)
                + _nbytes((rows, tn), F32)) + _nbytes((d, tn), F32)
    return pl.pallas_call(
        _mod_kernel,
        grid=(depth, n // tn),
        in_specs=[
            pl.BlockSpec((rows, d), lambda l, j: (0, 0)),
            pl.BlockSpec((1, d, tn), lambda l, j: (l, 0, j)),
            pl.BlockSpec((1, 1, tn), lambda l, j: (l, 0, j)),
        ],
        out_specs=pl.BlockSpec((1, rows, tn), lambda l, j: (l, 0, j)),
        out_shape=jax.ShapeDtypeStruct((depth, rows, n), F32),
        compiler_params=_params(("arbitrary", "arbitrary"), vmem + (8 << 20)),
        name="adaln_modulation",
    )(c_all, w_ada, b_ada.reshape(depth, 1, n))


def _norm_mod_kernel(x_ref, scale_ref, shift_ref, g_ref, o_ref):
    x = x_ref[...]
    ms = jnp.mean(x * x, axis=-1, keepdims=True)
    y = (x * lax.rsqrt(ms + EPS)) * g_ref[...]
    h = y * (1.0 + scale_ref[...]) + shift_ref[...]
    o_ref[...] = h.reshape(o_ref.shape).astype(o_ref.dtype)


def _norm_modulate(x, scale, shift, g, bb, tt):
    b, t, d = x.shape
    nt = t // tt
    vmem = 2 * (_nbytes((bb, tt, d), F32) + _nbytes((bb, tt, d), BF16)) \
        + 4 * _nbytes((bb, tt, d), F32)
    return pl.pallas_call(
        _norm_mod_kernel,
        grid=(b // bb, nt),
        in_specs=[
            pl.BlockSpec((bb, tt, d), lambda i, j: (i, j, 0)),
            pl.BlockSpec((bb, 1, d), lambda i, j: (i, 0, 0)),
            pl.BlockSpec((bb, 1, d), lambda i, j: (i, 0, 0)),
            pl.BlockSpec((1, 1, d), lambda i, j: (0, 0, 0)),
        ],
        out_specs=pl.BlockSpec((bb * tt, d), lambda i, j: (i * nt + j, 0)),
        out_shape=jax.ShapeDtypeStruct((b * t, d), BF16),
        compiler_params=_params(("arbitrary", "arbitrary"), vmem),
        name="norm_modulate",
    )(x, scale, shift, g.reshape(1, 1, d))


def _final_norm_kernel(x_ref, g_ref, o_ref):
    x = x_ref[...]
    ms = jnp.mean(x * x, axis=-1, keepdims=True)
    o_ref[...] = (x * lax.rsqrt(ms + EPS)) * g_ref[...]


def _final_norm(x, g, bb, tt):
    b, t, d = x.shape
    vmem = 8 * _nbytes((bb, tt, d), F32)
    return pl.pallas_call(
        _final_norm_kernel,
        grid=(b // bb, t // tt),
        in_specs=[
            pl.BlockSpec((bb, tt, d), lambda i, j: (i, j, 0)),
            pl.BlockSpec((1, 1, d), lambda i, j: (0, 0, 0)),
        ],
        out_specs=pl.BlockSpec((bb, tt, d), lambda i, j: (i, j, 0)),
        out_shape=jax.ShapeDtypeStruct((b, t, d), F32),
        compiler_params=_params(("arbitrary", "arbitrary"), vmem),
        name="final_norm",
    )(x, g.reshape(1, 1, d))


PROJ_TN = 512


def _cast_weight_tile(w_ref, wb_ref, first_row_tile):
    @pl.when(first_row_tile)
    def _():
        wb_ref[...] = w_ref[0].astype(BF16)


def _in_proj_kernel(a_ref, w_ref, o_ref, wb_ref):
    _cast_weight_tile(w_ref, wb_ref, pl.program_id(1) == 0)
    o_ref[...] = jnp.dot(a_ref[...], wb_ref[...], preferred_element_type=F32)


def _in_proj_operands(h2d, w_in, layer, col0, ncols, tm, idx):
    m, k = h2d.shape
    tn = PROJ_TN
    return dict(
        args=[h2d, w_in],
        in_specs=[
            pl.BlockSpec((tm, k), lambda *g: (idx(*g)[1], 0)),
            pl.BlockSpec((1, k, tn), lambda *g: (layer, 0, col0 // tn + idx(*g)[0])),
        ],
        out_specs=[pl.BlockSpec((tm, tn), lambda *g: (idx(*g)[1], idx(*g)[0]))],
        out_shape=[jax.ShapeDtypeStruct((m, ncols), F32)],
        scratch=[pltpu.VMEM((k, tn), BF16)],
        vmem=2 * (_nbytes((tm, k), BF16) + _nbytes((k, tn), F32) + _nbytes((tm, tn), F32))
        + _nbytes((k, tn), BF16) + _nbytes((tm, tn), F32),
    )


def _in_projection(h2d, w_in, layer, col0, ncols):
    m = h2d.shape[0]
    tm = 1024
    ops = _in_proj_operands(h2d, w_in, layer, col0, ncols, tm, lambda j, i: (j, i))
    return pl.pallas_call(
        _in_proj_kernel,
        grid=(ncols // PROJ_TN, m // tm),
        in_specs=ops["in_specs"],
        out_specs=ops["out_specs"][0],
        out_shape=ops["out_shape"][0],
        scratch_shapes=ops["scratch"],
        compiler_params=_params(("arbitrary", "arbitrary"), ops["vmem"] + (4 << 20)),
        name="in_projection",
    )(*ops["args"])


def _causal_taps(scr, v, w_ref, taps, tt):
    scr[:, HALO:HALO + tt, :] = v
    acc = None
    for k in range(taps):
        src = v if k == taps - 1 else scr[:, HALO - (taps - 1) + k:HALO - (taps - 1) + k + tt, :]
        term = w_ref[k:k + 1, :] * src
        acc = term if acc is None else acc + term
    return acc


def _block_scan_real(a, b):
    row = lax.broadcasted_iota(jnp.int32, (1, SUBLANES, a.shape[-1]), 1)
    for d in (1, 2, 4):
        keep = row >= d
        a_sh = pltpu.roll(a, d, 1)
        b_sh = pltpu.roll(b, d, 1)
        b = b + a * jnp.where(keep, b_sh, 0.0)
        a = a * jnp.where(keep, a_sh, 1.0)
    return a, b


def _block_scan_cplx(br, bi, tab_ref, kt, base, sl):
    for lvl, d in enumerate((1, 2, 4)):
        lr = tab_ref[kt, base + 2 * lvl, :, sl]
        li = tab_ref[kt, base + 2 * lvl + 1, :, sl]
        sr = pltpu.roll(br, d, 1)
        si = pltpu.roll(bi, d, 1)
        br, bi = br + (lr * sr - li * si), bi + (lr * si + li * sr)
    return br, bi


def _spread_rows(scr, nblk, bb):
    return jnp.concatenate(
        [jnp.broadcast_to(scr[:, k:k + 1, :], (bb, SUBLANES, LANES)) for k in range(nblk)],
        axis=1)


def _mixer_a_kernel(ab_ref, ac_ref, ax_ref, ag_ref, w_ref, b_ref, st_ref,
                    o_ref, st_out_ref, scr, *, tt):
    keep = CONV_A - 1

    @pl.when(pl.program_id(2) == 0)
    def _():
        scr[:, HALO - keep:HALO, :] = st_ref[...]

    conv_in = ac_ref[...] * ax_ref[...]
    y = b_ref[...] + _causal_taps(scr, conv_in, w_ref, CONV_A, tt)
    out = (ab_ref[...] * y) * jax.nn.silu(ag_ref[...])
    o_ref[...] = out.reshape(o_ref.shape).astype(o_ref.dtype)
    tail = scr[:, HALO + tt - keep:HALO + tt, :]
    st_out_ref[...] = tail
    scr[:, HALO - keep:HALO, :] = tail


def _mixer_a(proj3, w_conv, b_conv, state, bb, tt):
    b, t, _ = proj3.shape
    cw = CH_TILE
    nc = W_A // cw
    nt = t // tt
    blk = (bb, tt, cw)

    def col(off):
        return pl.BlockSpec(blk, lambda i, c, j, off=off: (i, j, off // cw + c))

    vmem = 2 * (4 * _nbytes(blk, F32) + _nbytes(blk, BF16)) \
        + _nbytes((bb, HALO + tt, cw), F32) + 6 * _nbytes(blk, F32)
    return pl.pallas_call(
        functools.partial(_mixer_a_kernel, tt=tt),
        grid=(b // bb, nc, nt),
        in_specs=[
            col(0), col(W_A), col(2 * W_A), col(3 * W_A),
            pl.BlockSpec((CONV_A, cw), lambda i, c, j: (0, c)),
            pl.BlockSpec((1, cw), lambda i, c, j: (0, c)),
            pl.BlockSpec((bb, CONV_A - 1, cw), lambda i, c, j: (i, 0, c)),
        ],
        out_specs=[
            pl.BlockSpec((bb * tt, cw), lambda i, c, j: (i * nt + j, c)),
            pl.BlockSpec((bb, CONV_A - 1, cw), lambda i, c, j: (i, 0, c)),
        ],
        out_shape=[
            jax.ShapeDtypeStruct((b * t, W_A), BF16),
            jax.ShapeDtypeStruct((b, CONV_A - 1, W_A), F32),
        ],
        scratch_shapes=[pltpu.VMEM((bb, HALO + tt, cw), F32)],
        compiler_params=_params(("arbitrary",) * 3, vmem),
        name="mixer_a",
    )(proj3, proj3, proj3, proj3, w_conv, b_conv.reshape(1, W_A), state)


MIXER_B_INS = 10
MIXER_B_OUTS = 3


def _mixer_b_init(cst_ref, hst_ref, scr, carry, first_time_tile):
    @pl.when(first_time_tile)
    def _():
        scr[:, HALO - (CONV_B - 1):HALO, :] = cst_ref[...]
        carry[...] = hst_ref[...]


def _mixer_b_kernel(*refs, bb, tt):
    ins = refs[:MIXER_B_INS]
    outs = refs[MIXER_B_INS:MIXER_B_INS + MIXER_B_OUTS]
    scr, carry, *scan_scr = refs[MIXER_B_INS + MIXER_B_OUTS:]
    _mixer_b_init(ins[8], ins[9], scr, carry, pl.program_id(2) == 0)
    _mixer_b_step(*ins[:8], *outs, scr, carry, scan_scr, bb=bb, tt=tt)


def _mixer_b_step(bx_ref, bg_ref, w_ref, b_ref, wg_ref, brg_ref, big_ref, lam_ref,
                  o_ref, cst_out_ref, hst_out_ref, scr, carry, scan_scr, *, bb, tt,
                  side_matmul=None):
    keep = CONV_B - 1
    cw = CH_TILE
    rows = bb * tt
    nblk = tt // SUBLANES
    nv = nblk // SUBLANES

    scr[:, HALO:HALO + tt, :] = bx_ref[...]
    cst_out_ref[...] = scr[:, HALO + tt - keep:HALO + tt, :]
    softplus_neg_lam = jax.nn.softplus(-lam_ref[...])
    cin_all = carry[...]
    h_parts, last_parts = [], []
    assert LRU_HD == LANES
    n_heads = cw // LRU_HD
    n_chunks = 1 if side_matmul is None else rows // GATE_CHUNK_ROWS
    rc = rows // n_chunks
    for s in range(n_heads):
        sl = slice(s * LANES, (s + 1) * LANES)
        w_head = jnp.concatenate([wg_ref[0, sl, sl], wg_ref[0, sl, cw + s * LANES:cw + (s + 1) * LANES]],
                                 axis=1)
        a_chunks, b_chunks = [], []
        for k in range(n_chunks):
            acc = None
            for j in range(CONV_B):
                lo = HALO - keep + j
                if tt >= rc:
                    b0, t0 = divmod(k * rc, tt)
                    win = scr[b0:b0 + 1, lo + t0:lo + t0 + rc, sl]
                else:
                    nb = rc // tt
                    win = scr[k * nb:(k + 1) * nb, lo:lo + tt, sl]
                term = w_ref[j:j + 1, sl] * win.reshape(rc, LANES)
                acc = term if acc is None else acc + term
            xb = b_ref[:, sl] + acc
            gates = jnp.dot(xb.astype(BF16), w_head, preferred_element_type=F32)
            if side_matmul is not None:
                side_matmul(s * n_chunks + k, n_heads * n_chunks)
            r = jax.nn.sigmoid(gates[:, :LANES] + brg_ref[:, sl])
            ig = jax.nn.sigmoid(gates[:, LANES:] + big_ref[:, sl])
            log_a = (-LRU_C * r) * softplus_neg_lam[:, sl]
            a = jnp.exp(log_a)
            beta = jnp.sqrt(-jnp.tanh(log_a) * (a * a + 1.0))
            bt = (beta * ig) * xb
            a_k, b_k = _block_scan_real(a.reshape(rc // SUBLANES, SUBLANES, LANES),
                                        bt.reshape(rc // SUBLANES, SUBLANES, LANES))
            a_chunks.append(a_k)
            b_chunks.append(b_k)
        a_blk = jnp.concatenate(a_chunks, axis=0)
        b_blk = jnp.concatenate(b_chunks, axis=0)
        cin = cin_all[:, :, sl]
        if nblk == 1:
            h = b_blk + a_blk * cin
            last = h[:, SUBLANES - 1:SUBLANES, :]
        else:
            acum, bcum, cscr = scan_scr
            acum[s] = a_blk.reshape(bb, tt, LANES)
            bcum[s] = b_blk.reshape(bb, tt, LANES)
            a2 = acum[s, :, pl.ds(SUBLANES - 1, nblk, stride=SUBLANES), :]
            b2 = bcum[s, :, pl.ds(SUBLANES - 1, nblk, stride=SUBLANES), :]
            a2, b2 = _block_scan_real(a2.reshape(bb * nv, SUBLANES, LANES),
                                      b2.reshape(bb * nv, SUBLANES, LANES))
            a2 = a2.reshape(bb, nblk, LANES)
            b2 = b2.reshape(bb, nblk, LANES)
            prev = cin
            ends = []
            for j in range(nv):
                seg = slice(j * SUBLANES, (j + 1) * SUBLANES)
                e = b2[:, seg, :] + a2[:, seg, :] * prev
                ends.append(e)
                prev = e[:, SUBLANES - 1:SUBLANES, :]
            last = prev
            cscr[s, :, 0:1, :] = cin
            cscr[s, :, 1:1 + nblk, :] = jnp.concatenate(ends, axis=1)
            spread = _spread_rows(cscr.at[s], nblk, bb)
            h = b_blk + a_blk * spread.reshape(bb * nblk, SUBLANES, LANES)
        h_parts.append(h.reshape(bb, tt, LANES))
        last_parts.append(last)
    h = jnp.concatenate(h_parts, axis=-1)
    last = jnp.concatenate(last_parts, axis=-1)
    carry[...] = last
    hst_out_ref[...] = last
    out = h * jax.nn.silu(bg_ref[...])
    o_ref[...] = out.reshape(o_ref.shape).astype(o_ref.dtype)
    scr[:, HALO - keep:HALO, :] = scr[:, HALO + tt - keep:HALO + tt, :]


def _mixer_b_operands(proj3, w_conv, b_conv, w_gates, b_rg, b_ig, lam, conv_state, h_state,
                      bb, tt, idx):
    b, t, _ = proj3.shape
    cw = CH_TILE
    nt = t // tt
    nblk = tt // SUBLANES
    blk = (bb, tt, cw)

    def spec(shape, fn):
        return pl.BlockSpec(shape, lambda *g: fn(*idx(*g)))

    vec = spec((1, cw), lambda i, c, j: (0, c))
    scratch = [pltpu.VMEM((bb, HALO + tt, cw), F32), pltpu.VMEM((bb, 1, cw), F32)]
    if nblk > 1:
        assert nblk % SUBLANES == 0
        slabs = cw // LANES
        scratch += [pltpu.VMEM((slabs, bb, tt, LANES), F32),
                    pltpu.VMEM((slabs, bb, tt, LANES), F32),
                    pltpu.VMEM((slabs, bb, HALO + nblk, LANES), F32)]
    return dict(
        args=[proj3, proj3, w_conv, b_conv.reshape(1, W_B), w_gates,
              b_rg.reshape(1, W_B), b_ig.reshape(1, W_B), lam.reshape(1, W_B),
              conv_state, h_state],
        in_specs=[
            spec(blk, lambda i, c, j: (i, j, c)),
            spec(blk, lambda i, c, j: (i, j, W_B // cw + c)),
            spec((CONV_B, cw), lambda i, c, j: (0, c)),
            vec,
            spec((1, cw, 2 * cw), lambda i, c, j: (c, 0, 0)),
            vec, vec, vec,
            spec((bb, CONV_B - 1, cw), lambda i, c, j: (i, 0, c)),
            spec((bb, 1, cw), lambda i, c, j: (i, 0, c)),
        ],
        out_specs=[
            spec((bb * tt, cw), lambda i, c, j: (i * nt + j, c)),
            spec((bb, CONV_B - 1, cw), lambda i, c, j: (i, 0, c)),
            spec((bb, 1, cw), lambda i, c, j: (i, 0, c)),
        ],
        out_shape=[
            jax.ShapeDtypeStruct((b * t, W_B), BF16),
            jax.ShapeDtypeStruct((b, CONV_B - 1, W_B), F32),
            jax.ShapeDtypeStruct((b, 1, W_B), F32),
        ],
        scratch=scratch,
        vmem=2 * (2 * _nbytes(blk, F32) + _nbytes(blk, BF16) + _nbytes((cw, 2 * cw), BF16))
        + _nbytes((bb, HALO + tt, cw), F32) + 16 * _nbytes(blk, F32),
        grid=(b // bb, W_B // cw, nt),
    )


def _mixer_b(proj3, *params, bb, tt):
    ops = _mixer_b_operands(proj3, *params, bb, tt, lambda i, c, j: (i, c, j))
    return pl.pallas_call(
        functools.partial(_mixer_b_kernel, bb=bb, tt=tt),
        grid=ops["grid"],
        in_specs=ops["in_specs"],
        out_specs=ops["out_specs"],
        out_shape=ops["out_shape"],
        scratch_shapes=ops["scratch"],
        compiler_params=_params(("arbitrary",) * 3, ops["vmem"]),
        name="mixer_b",
    )(*ops["args"])


def _fused_proj_kernel(*refs, mixers, ni):
    a_ref, w_ref = refs[:2]
    mixers = [dict(mx) for mx in mixers]
    pos = 2
    for mx in mixers:
        mx["ins"] = refs[pos:pos + mx["n_ins"]]
        pos += mx["n_ins"]
    proj_ref = refs[pos]
    pos += 1
    for mx in mixers:
        mx["outs"] = refs[pos:pos + mx["n_outs"]]
        pos += mx["n_outs"]
    wb_ref = refs[pos]
    pos += 1
    for mx in mixers:
        mx["scr"] = refs[pos:pos + mx["n_scr"]]
        pos += mx["n_scr"]
    s = pl.program_id(0)
    _cast_weight_tile(w_ref, wb_ref, lax.rem(s, ni) == 0)
    for mx in mixers:
        mx["init"](mx["ins"], mx["scr"], s)
    k_total = a_ref.shape[1]
    acc = []

    def projection_part(part, n_parts):
        kc = MXU_DIM
        per_part = k_total // (kc * n_parts)
        assert per_part * kc * n_parts == k_total
        for sub in range(part * per_part, (part + 1) * per_part):
            ks = slice(sub * kc, (sub + 1) * kc)
            term = jnp.dot(a_ref[:, ks], wb_ref[ks, :], preferred_element_type=F32)
            acc[:] = [term if not acc else acc[0] + term]
        if part == n_parts - 1:
            proj_ref[...] = acc[0]

    for mx in mixers[:-1]:
        mx["step"](mx["ins"], mx["outs"], mx["scr"], s, None)
    mx = mixers[-1]
    mx["step"](mx["ins"], mx["outs"], mx["scr"], s, projection_part)


def _fused_proj_call(name, h2d, w_in, layer, col0, ncols, mixers):
    steps = None
    for ops, _, _ in mixers:
        n = 1
        for g in ops["grid"]:
            n *= g
        assert steps in (None, n)
        steps = n
    nj = ncols // PROJ_TN
    ni = steps // nj
    tm = h2d.shape[0] // ni
    assert nj * ni == steps and tm * ni == h2d.shape[0] and tm % SUBLANES == 0
    mm = _in_proj_operands(h2d, w_in, layer, col0, ncols, tm, lambda s: (s // ni, s % ni))
    specs = [dict(n_ins=len(ops["args"]), n_outs=len(ops["out_shape"]),
                  n_scr=len(ops["scratch"]), init=init, step=step)
             for ops, init, step in mixers]
    cat = lambda key: [x for ops, _, _ in mixers for x in ops[key]]
    return pl.pallas_call(
        functools.partial(_fused_proj_kernel, mixers=specs, ni=ni),
        grid=(steps,),
        in_specs=mm["in_specs"] + cat("in_specs"),
        out_specs=mm["out_specs"] + cat("out_specs"),
        out_shape=mm["out_shape"] + cat("out_shape"),
        scratch_shapes=mm["scratch"] + cat("scratch"),
        compiler_params=_params(("arbitrary",),
                                mm["vmem"] + sum(ops["vmem"] for ops, _, _ in mixers)),
        name=name,
    )(*mm["args"], *cat("args"))


def _flat_index(dims):
    def idx(s):
        out = []
        for k, d in enumerate(dims):
            stride = 1
            for e in dims[k + 1:]:
                stride *= e
            out.append((s // stride) % d if k else s // stride)
        return tuple(out)
    return idx


def _flat_mixer_b(proj3, *params, bb, tt):
    b, t, _ = proj3.shape
    dims = (b // bb, W_B // CH_TILE, t // tt)
    ops = _mixer_b_operands(proj3, *params, bb, tt, _flat_index(dims))
    nt = dims[2]

    def init(ins, scr, s):
        _mixer_b_init(ins[8], ins[9], scr[0], scr[1], lax.rem(s, nt) == 0)

    def step(ins, outs, scr, s, side_matmul):
        _mixer_b_step(*ins[:8], *outs, scr[0], scr[1], scr[2:], bb=bb, tt=tt,
                      side_matmul=side_matmul)

    return ops, init, step


def _s5_disc_kernel(lre_ref, lim_ref, ldt_ref, btre_ref, btim_ref,
                    abre_ref, abim_ref, bbre_ref, bbim_ref):
    lre = lre_ref[0]
    lim = lim_ref[0]
    dt = jnp.exp(ldt_ref[0])
    mag = jnp.exp(lre * dt)
    ab_re = mag * jnp.cos(lim * dt)
    ab_im = mag * jnp.sin(lim * dt)
    nr = ab_re - 1.0
    den = lre * lre + lim * lim
    fr = (nr * lre + ab_im * lim) / den
    fi = (ab_im * lre - nr * lim) / den
    bre = btre_ref[0]
    bim = btim_ref[0]
    bbre_ref[0] = fr * bre - fi * bim
    bbim_ref[0] = fr * bim + fi * bre
    abre_ref[0] = ab_re
    abim_ref[0] = ab_im


def _s5_discretize(lam_re, lam_im, log_dt, b_re, b_im):
    depth = lam_re.shape[0]
    gp = (depth, S5_G, 1, S5_P)
    ldt = jnp.broadcast_to(log_dt[:, :, None, None], gp)
    bt_shape = (depth, S5_G, S5_GROUP, S5_P)
    small = pl.BlockSpec((1, S5_G, 1, S5_P), lambda l: (l, 0, 0, 0))
    big = pl.BlockSpec((1, S5_G, S5_GROUP, S5_P), lambda l: (l, 0, 0, 0))
    ab_re, ab_im, bb_re, bb_im = pl.pallas_call(
        _s5_disc_kernel,
        grid=(depth,),
        in_specs=[small, small, small, big, big],
        out_specs=[small, small, big, big],
        out_shape=[jax.ShapeDtypeStruct(gp, F32), jax.ShapeDtypeStruct(gp, F32),
                   jax.ShapeDtypeStruct(bt_shape, F32), jax.ShapeDtypeStruct(bt_shape, F32)],
        name="s5_discretize",
    )(lam_re.reshape(gp), lam_im.reshape(gp), ldt,
      jnp.swapaxes(b_re, 2, 3), jnp.swapaxes(b_im, 2, 3))
    return (ab_re.reshape(depth, 1, S5_STATES), ab_im.reshape(depth, 1, S5_STATES),
            bb_re, bb_im)


def _s5_pow_kernel(ar_ref, ai_ref, tab_ref):
    def cmul(x, y):
        return x[0] * y[0] - x[1] * y[1], x[0] * y[1] + x[1] * y[0]

    def powers(b1):
        b2 = cmul(b1, b1)
        b3 = cmul(b2, b1)
        b4 = cmul(b2, b2)
        return (b1, b2, b3, b4, cmul(b4, b1), cmul(b4, b2), cmul(b4, b3), cmul(b4, b4))

    shape = (SUBLANES, S5_STATES)
    row = lax.broadcasted_iota(jnp.int32, shape, 0)
    zero = jnp.zeros(shape, F32)
    pw_a = powers((ar_ref[0], ai_ref[0]))
    pw_a8 = powers(pw_a[SUBLANES - 1])
    for base, pw in ((0, pw_a), (N_TABLES // 2, pw_a8)):
        for part in (0, 1):
            tab_ref[0, base + part] = jnp.where(row >= 1, pw[0][part], zero)
            tab_ref[0, base + 2 + part] = jnp.where(row >= 2, pw[1][part], zero)
            tab_ref[0, base + 4 + part] = jnp.where(row >= 4, pw[3][part], zero)
            p = zero
            for r, v in enumerate(pw):
                p = jnp.where(row == r, v[part], p)
            tab_ref[0, base + 6 + part] = p


def _s5_power_tables(ab_re, ab_im):
    depth = ab_re.shape[0]
    vec = pl.BlockSpec((1, 1, S5_STATES), lambda l: (l, 0, 0))
    tables = pl.pallas_call(
        _s5_pow_kernel,
        grid=(depth,),
        in_specs=[vec, vec],
        out_specs=pl.BlockSpec((1, N_TABLES, SUBLANES, S5_STATES), lambda l: (l, 0, 0, 0)),
        out_shape=jax.ShapeDtypeStruct((depth, N_TABLES, SUBLANES, S5_STATES), F32),
        name="s5_power_tables",
    )(ab_re, ab_im)
    kslab = S5_STATES // S5_KT
    tables = tables.reshape(depth, N_TABLES, SUBLANES, S5_KT, kslab)
    return jnp.transpose(tables, (0, 3, 1, 2, 4)).reshape(
        depth * S5_KT, N_TABLES, SUBLANES, kslab)


def _s5_block_matrices(bb_re, bb_im, c_re, c_im):
    depth = bb_re.shape[0]
    gpt = CH_TILE // S5_GROUP
    chan = lax.broadcasted_iota(jnp.int32, (S5_Q, CH_TILE, CH_TILE), 1)
    state = lax.broadcasted_iota(jnp.int32, (S5_Q, CH_TILE, CH_TILE), 2)
    slab = lax.broadcasted_iota(jnp.int32, (S5_Q, CH_TILE, CH_TILE), 0)
    gps = LANES // S5_P
    keep = (chan // S5_GROUP) == gps * slab + (state % LANES) // S5_P

    def dense_b(x):
        return x.reshape(depth * S5_KT, gpt * S5_GROUP, S5_P)

    db = jnp.concatenate([dense_b(bb_re)] * gps + [dense_b(bb_im)] * gps, axis=-1)
    bm = jnp.where(keep[None], db[:, None], 0.0).astype(BF16)

    def dense_c(x):
        x = x.reshape(depth * S5_KT, gpt, S5_GROUP, S5_P)
        return jnp.transpose(x, (0, 3, 1, 2)).reshape(depth * S5_KT, S5_P, gpt * S5_GROUP)

    dc = jnp.concatenate([dense_c(c_re)] * gps + [dense_c(-c_im)] * gps, axis=1)
    keep_c = jnp.swapaxes(keep, 1, 2)
    cm = jnp.where(keep_c[None], dc[:, None], 0.0).astype(BF16)
    return bm, cm


MIXER_C_INS = 10
MIXER_C_OUTS = 3


def _mixer_c_init(sre_ref, sim_ref, car_r, car_i, kt, first_time_tile):
    @pl.when(first_time_tile)
    def _():
        car_r[kt] = sre_ref[...]
        car_i[kt] = sim_ref[...]


def _mixer_c_kernel(*refs, bb, tt):
    ins = refs[:MIXER_C_INS]
    outs = refs[MIXER_C_INS:MIXER_C_INS + MIXER_C_OUTS]
    ystore, car_r, car_i, *scan_scr = refs[MIXER_C_INS + MIXER_C_OUTS:]
    kt = pl.program_id(2)
    _mixer_c_init(ins[8], ins[9], car_r, car_i, kt, pl.program_id(1) == 0)
    _mixer_c_step(*ins[:8], *outs, ystore, car_r, car_i, scan_scr, kt, bb=bb, tt=tt)


def _mixer_c_step(cu_ref, cg_ref, bm_ref, cm_ref, tab_ref, d_ref, wglu_ref, bglu_ref,
                  o_ref, sre_out_ref, sim_out_ref,
                  ystore, car_r, car_i, scan_scr, kt, *, bb, tt, side_matmul=None):
    cw = CH_TILE
    rows = bb * tt
    nblk = tt // SUBLANES
    nv = nblk // SUBLANES
    n = bb * nblk
    base8 = N_TABLES // 2

    u = cu_ref[...].reshape(rows, cw)
    ub = u.astype(BF16)
    y = d_ref[...] * u
    cr_all = car_r[kt]
    ci_all = car_i[kt]
    new_r, new_i = [], []
    for q in range(S5_Q):
        sl = slice(LANES * q, LANES * (q + 1))
        bu = jnp.dot(ub, bm_ref[kt, q], preferred_element_type=F32)
        if side_matmul is not None:
            side_matmul(q, S5_Q)
        br, bi = _block_scan_cplx(bu[:, :LANES].reshape(n, SUBLANES, LANES),
                                  bu[:, LANES:].reshape(n, SUBLANES, LANES), tab_ref, kt, 0, sl)
        p_r = tab_ref[kt, 6, :, sl]
        p_i = tab_ref[kt, 7, :, sl]
        cr = cr_all[:, :, sl]
        ci = ci_all[:, :, sl]
        if nblk == 1:
            kr, ki = cr, ci
        else:
            hsr, hsi, csr, csi = scan_scr
            hsr[...] = br.reshape(bb, tt, LANES)
            hsi[...] = bi.reshape(bb, tt, LANES)
            xr = hsr[:, pl.ds(SUBLANES - 1, nblk, stride=SUBLANES), :]
            xi = hsi[:, pl.ds(SUBLANES - 1, nblk, stride=SUBLANES), :]
            xr, xi = _block_scan_cplx(xr.reshape(bb * nv, SUBLANES, LANES),
                                      xi.reshape(bb * nv, SUBLANES, LANES), tab_ref, kt, base8,
                                      sl)
            xr = xr.reshape(bb, nblk, LANES)
            xi = xi.reshape(bb, nblk, LANES)
            q_r = tab_ref[kt, base8 + 6, :, sl]
            q_i = tab_ref[kt, base8 + 7, :, sl]
            prev_r, prev_i = cr, ci
            ends_r, ends_i = [], []
            for j in range(nv):
                seg = slice(j * SUBLANES, (j + 1) * SUBLANES)
                er = xr[:, seg, :] + (q_r * prev_r - q_i * prev_i)
                ei = xi[:, seg, :] + (q_r * prev_i + q_i * prev_r)
                ends_r.append(er)
                ends_i.append(ei)
                prev_r = er[:, SUBLANES - 1:SUBLANES, :]
                prev_i = ei[:, SUBLANES - 1:SUBLANES, :]
            csr[:, 0:1, :] = cr
            csi[:, 0:1, :] = ci
            csr[:, 1:1 + nblk, :] = jnp.concatenate(ends_r, axis=1)
            csi[:, 1:1 + nblk, :] = jnp.concatenate(ends_i, axis=1)
            kr = _spread_rows(csr, nblk, bb).reshape(n, SUBLANES, LANES)
            ki = _spread_rows(csi, nblk, bb).reshape(n, SUBLANES, LANES)
        hr = br + (p_r * kr - p_i * ki)
        hi = bi + (p_r * ki + p_i * kr)
        hr = hr.reshape(bb, tt, LANES)
        hi = hi.reshape(bb, tt, LANES)
        new_r.append(hr[:, tt - 1:tt, :])
        new_i.append(hi[:, tt - 1:tt, :])
        hcat = jnp.concatenate([hr.reshape(rows, LANES), hi.reshape(rows, LANES)], axis=-1)
        y = y + jnp.dot(hcat.astype(BF16), cm_ref[kt, q], preferred_element_type=F32)

    ncr_all = jnp.concatenate(new_r, axis=-1)
    nci_all = jnp.concatenate(new_i, axis=-1)
    car_r[kt] = ncr_all
    car_i[kt] = nci_all
    sre_out_ref[:, kt] = ncr_all
    sim_out_ref[:, kt] = nci_all
    ystore[kt] = y

    @pl.when(kt == S5_KT - 1)
    def _():
        y_all = jnp.concatenate([ystore[k] for k in range(S5_KT)], axis=-1)
        yg = jax.nn.gelu(y_all)
        z = jnp.dot(yg.astype(BF16), wglu_ref[0].astype(BF16),
                    preferred_element_type=F32) + bglu_ref[...]
        yy = yg * jax.nn.sigmoid(z)
        out = yy * jax.nn.silu(cg_ref[...].reshape(rows, W_C))
        o_ref[...] = out.astype(o_ref.dtype)


def _mixer_c_operands(proj3, bmat, cmat, tables, d_skip, w_glu, b_glu, s_re, s_im, layer,
                      bb, tt, idx):
    b, t, _ = proj3.shape
    u_col = 0
    cw = CH_TILE
    nt = t // tt
    nblk = tt // SUBLANES
    kslab = S5_Q * LANES

    def spec(shape, fn):
        return pl.BlockSpec(shape, lambda *g: fn(*idx(*g)))

    mat = spec((S5_KT, S5_Q, cw, cw), lambda i, j, k: (layer, 0, 0, 0))
    st = spec((bb, 1, kslab), lambda i, j, k: (i, 0, k))
    st_out = spec((bb, S5_KT, 1, kslab), lambda i, j, k: (i, 0, 0, 0))
    rows = bb * tt
    scratch = [pltpu.VMEM((S5_KT, rows, cw), F32),
               pltpu.VMEM((S5_KT, bb, 1, kslab), F32),
               pltpu.VMEM((S5_KT, bb, 1, kslab), F32)]
    if nblk > 1:
        assert nblk % SUBLANES == 0 and nblk // SUBLANES <= SUBLANES
        scratch += [pltpu.VMEM((bb, tt, LANES), F32), pltpu.VMEM((bb, tt, LANES), F32),
                    pltpu.VMEM((bb, HALO + nblk, LANES), F32),
                    pltpu.VMEM((bb, HALO + nblk, LANES), F32)]
    vmem = 2 * (_nbytes((bb, tt, cw), F32) + _nbytes((bb, tt, W_C), F32)
                + 2 * _nbytes((S5_KT, S5_Q, cw, cw), BF16)
                + _nbytes((S5_KT, N_TABLES, SUBLANES, kslab), F32)
                + _nbytes((W_C, W_C), F32) + _nbytes((bb, tt, W_C), BF16)
                + 4 * _nbytes((bb, SUBLANES, kslab), F32)) \
        + _nbytes((S5_KT, rows, cw), F32) + 2 * _nbytes((bb, tt, LANES), F32) \
        + 2 * _nbytes((S5_KT, bb, SUBLANES, kslab), F32) + 10 * _nbytes((rows, W_C), F32)
    return dict(
        args=[proj3, proj3, bmat, cmat, tables, d_skip.reshape(1, W_C), w_glu,
              b_glu.reshape(1, W_C), s_re, s_im],
        in_specs=[
            spec((bb, tt, cw), lambda i, j, k: (i, j, u_col // cw + k)),
            spec((bb, tt, W_C), lambda i, j, k: (i, j, u_col // W_C + 1)),
            mat, mat,
            spec((S5_KT, N_TABLES, SUBLANES, kslab), lambda i, j, k: (layer, 0, 0, 0)),
            spec((1, cw), lambda i, j, k: (0, k)),
            spec((1, W_C, W_C), lambda i, j, k: (layer, 0, 0)),
            spec((1, W_C), lambda i, j, k: (0, 0)),
            st, st,
        ],
        out_specs=[
            spec((rows, W_C), lambda i, j, k: (i * nt + j, 0)),
            st_out, st_out,
        ],
        out_shape=[
            jax.ShapeDtypeStruct((b * t, W_C), BF16),
            jax.ShapeDtypeStruct((b, S5_KT, 1, kslab), F32),
            jax.ShapeDtypeStruct((b, S5_KT, 1, kslab), F32),
        ],
        scratch=scratch,
        vmem=vmem,
        grid=(b // bb, nt, S5_KT),
    )


def _mixer_c(proj3, *params, bb, tt):
    ops = _mixer_c_operands(proj3, *params, bb, tt, lambda i, j, k: (i, j, k))
    return pl.pallas_call(
        functools.partial(_mixer_c_kernel, bb=bb, tt=tt),
        grid=ops["grid"],
        in_specs=ops["in_specs"],
        out_specs=ops["out_specs"],
        out_shape=ops["out_shape"],
        scratch_shapes=ops["scratch"],
        compiler_params=_params(("arbitrary",) * 3, ops["vmem"]),
        name="mixer_c",
    )(*ops["args"])


def _flat_mixer_c(proj3, *params, bb, tt):
    b, t, _ = proj3.shape
    dims = (b // bb, t // tt, S5_KT)
    ops = _mixer_c_operands(proj3, *params, bb, tt, _flat_index(dims))
    nt = dims[1]

    def init(ins, scr, s):
        _mixer_c_init(ins[8], ins[9], scr[1], scr[2], lax.rem(s, S5_KT),
                      lax.rem(s // S5_KT, nt) == 0)

    def step(ins, outs, scr, s, side_matmul):
        _mixer_c_step(*ins[:8], *outs, scr[0], scr[1], scr[2], scr[3:],
                      lax.rem(s, S5_KT), bb=bb, tt=tt, side_matmul=side_matmul)

    return ops, init, step


def _out_proj_kernel(ma_ref, mb_ref, mc_ref, w_ref, x_ref, gate_ref, o_ref, *, bb, tt):
    w = w_ref[0].astype(BF16)
    acc = jnp.dot(ma_ref[...], w[0:W_A, :], preferred_element_type=F32)
    acc = acc + jnp.dot(mb_ref[...], w[W_A:W_A + W_B, :], preferred_element_type=F32)
    acc = acc + jnp.dot(mc_ref[...], w[W_A + W_B:, :], preferred_element_type=F32)
    o_ref[...] = x_ref[...] + gate_ref[...] * acc.reshape(bb, tt, acc.shape[-1])


def _out_projection(out_a, out_b, out_c, w_out, x, gate, layer, bb, tt):
    b, t, d = x.shape
    tm = bb * tt
    tn = 512
    nt = t // tt
    m = b * t

    def rows(width):
        return pl.BlockSpec((tm, width), lambda i, j: (i, 0))

    vmem = 2 * (_nbytes((tm, d), BF16) + _nbytes((d, tn), F32)
                + 2 * _nbytes((tm, tn), F32)) + _nbytes((d, tn), BF16) \
        + 3 * _nbytes((tm, tn), F32)
    return pl.pallas_call(
        functools.partial(_out_proj_kernel, bb=bb, tt=tt),
        grid=(m // tm, d // tn),
        in_specs=[
            rows(W_A), rows(W_B), rows(W_C),
            pl.BlockSpec((1, d, tn), lambda i, j: (layer, 0, j)),
            pl.BlockSpec((bb, tt, tn), lambda i, j: (i // nt, i % nt, j)),
            pl.BlockSpec((bb, 1, tn), lambda i, j: (i // nt, 0, j)),
        ],
        out_specs=pl.BlockSpec((bb, tt, tn), lambda i, j: (i // nt, i % nt, j)),
        out_shape=jax.ShapeDtypeStruct((b, t, d), F32),
        compiler_params=_params(("arbitrary", "arbitrary"), vmem + (4 << 20)),
        name="out_projection",
    )(out_a, out_b, out_c, w_out, x, gate)


def _gate_weights(w_rg, w_ig):
    hpt = CH_TILE // LRU_HD
    pairs = LRU_HEADS // hpt
    w = jnp.stack([w_rg, w_ig], axis=2)
    w = w.reshape(pairs, hpt, LRU_HD, 2, LRU_HD)
    eye = jnp.eye(hpt, dtype=bool)
    out = jnp.where(eye[None, :, None, None, :, None],
                    w[:, :, :, :, None, :], 0.0)
    return out.reshape(pairs, CH_TILE, 2 * CH_TILE).astype(BF16)


def _layer(x, mod, states, wts, layer, tiles):
    b, t, d = x.shape
    conv_a, conv_b, lru_h, s5_re, s5_im = states
    shift, scale, gate = (mod[:, None, k * d:(k + 1) * d] for k in range(3))

    h = _norm_modulate(x, scale, shift, wts["g_norm"], *tiles["norm"])
    w_in = wts["w_in"]
    col_a, col_b, col_c = 0, 4 * W_A, 4 * W_A + 2 * W_B
    b_params = (wts["w_conv_b"], wts["b_conv_b"], wts["w_gates"], wts["b_rg"], wts["b_ig"],
                wts["lru_lambda"], conv_b, lru_h.reshape(b, 1, W_B))
    c_params = (wts["bmat"], wts["cmat"], wts["tables"], wts["s5_d"], wts["w_glu"],
                wts["b_glu"], s5_re.reshape(b, 1, S5_STATES), s5_im.reshape(b, 1, S5_STATES),
                layer)
    bb_b, tt_b = tiles["b"]
    bb_c, tt_c = tiles["c"]

    proj_c = _in_projection(h, w_in, layer, col_c, 2 * W_C).reshape(b, t, 2 * W_C)
    if tiles["fuse"]:
        proj_b, out_c, re_new, im_new = _fused_proj_call(
            "proj_b_mixer_c", h, w_in, layer, col_b, 2 * W_B,
            [_flat_mixer_c(proj_c, *c_params, bb=bb_c, tt=tt_c)])
        proj_b = proj_b.reshape(b, t, 2 * W_B)
        proj_a, out_b, conv_b_new, lru_new = _fused_proj_call(
            "proj_a_mixer_b", h, w_in, layer, col_a, 4 * W_A,
            [_flat_mixer_b(proj_b, *b_params, bb=bb_b, tt=tt_b)])
    else:
        out_c, re_new, im_new = _mixer_c(proj_c, *c_params, bb=bb_c, tt=tt_c)
        proj_b = _in_projection(h, w_in, layer, col_b, 2 * W_B).reshape(b, t, 2 * W_B)
        out_b, conv_b_new, lru_new = _mixer_b(proj_b, *b_params, bb=bb_b, tt=tt_b)
        proj_a = _in_projection(h, w_in, layer, col_a, 4 * W_A)
    proj_a = proj_a.reshape(b, t, 4 * W_A)
    out_a, conv_a_new = _mixer_a(proj_a, wts["w_conv_a"], wts["b_conv_a"], conv_a,
                                 *tiles["a"])

    x_new = _out_projection(out_a, out_b, out_c, wts["w_out"], x, gate, layer, *tiles["out"])
    return (x_new, conv_a_new, conv_b_new, lru_new.reshape(b, W_B),
            re_new.reshape(b, S5_G, S5_P), im_new.reshape(b, S5_G, S5_P))


PROMPT_TILES = {"norm": (1, 512), "a": (1, 2048), "b": (1, 1024), "c": (1, 256),
                "out": (1, 1024), "final": (1, 512), "fuse": True}
SAMPLE_TILES = {"norm": (32, 8), "a": (128, 8), "b": (64, 8), "c": (32, 8),
                "out": (128, 8), "final": (32, 8), "fuse": False}


def kernel(x_prompt, x_sample, c_prompt, c_sample, state_conv_a, state_conv_b, state_lru_h,
           state_s5_re, state_s5_im, g_norm, w_ada, b_ada, w_in, w_conv_a, b_conv_a,
           w_conv_b, b_conv_b, w_rg, b_rg, w_ig, b_ig, lru_lambda, s5_lambda_re,
           s5_lambda_im, s5_log_dt, s5_b_re, s5_b_im, s5_c_re, s5_c_im, s5_d, w_glu,
           b_glu, w_out, g_final):
    depth = w_in.shape[0]
    bp = x_prompt.shape[0]
    bs = x_sample.shape[0]

    c_all = jnp.concatenate([c_prompt, c_sample], axis=0)
    pad = (-c_all.shape[0]) % SUBLANES
    c_all = jnp.pad(c_all, ((0, pad), (0, 0)))
    mod = _modulation(c_all, w_ada, b_ada)

    ab_re, ab_im, bb_re, bb_im = _s5_discretize(s5_lambda_re, s5_lambda_im, s5_log_dt,
                                               s5_b_re, s5_b_im)
    tables = _s5_power_tables(ab_re, ab_im)
    bmat, cmat = _s5_block_matrices(bb_re, bb_im, s5_c_re, s5_c_im)
    w_glu_bf16 = w_glu.astype(BF16)

    xp, xs = x_prompt, x_sample
    zeros_p = (jnp.zeros((bp, CONV_A - 1, W_A), F32), jnp.zeros((bp, CONV_B - 1, W_B), F32),
               jnp.zeros((bp, W_B), F32), jnp.zeros((bp, S5_G, S5_P), F32),
               jnp.zeros((bp, S5_G, S5_P), F32))
    outs_p, outs_s = [], []
    for l in range(depth):
        wts = {
            "g_norm": g_norm[l], "w_in": w_in,
            "w_conv_a": w_conv_a[l], "b_conv_a": b_conv_a[l],
            "w_conv_b": w_conv_b[l], "b_conv_b": b_conv_b[l],
            "w_gates": _gate_weights(w_rg[l], w_ig[l]), "b_rg": b_rg[l], "b_ig": b_ig[l],
            "lru_lambda": lru_lambda[l], "bmat": bmat, "cmat": cmat,
            "tables": tables, "s5_d": s5_d[l],
            "w_glu": w_glu_bf16, "b_glu": b_glu[l], "w_out": w_out,
        }
        res_p = _layer(xp, mod[l, :bp], zeros_p, wts, l, PROMPT_TILES)
        xp = res_p[0]
        outs_p.append(res_p[1:])
        st_s = (state_conv_a[l], state_conv_b[l], state_lru_h[l], state_s5_re[l],
                state_s5_im[l])
        res_s = _layer(xs, mod[l, bp:bp + bs], st_s, wts, l, SAMPLE_TILES)
        xs = res_s[0]
        outs_s.append(res_s[1:])

    y_prompt = _final_norm(xp, g_final, *PROMPT_TILES["final"])
    y_sample = _final_norm(xs, g_final, *SAMPLE_TILES["final"])
    stack = lambda outs, k: jnp.stack([o[k] for o in outs])
    return (y_prompt, y_sample,
            *(stack(outs_p, k) for k in range(5)),
            *(stack(outs_s, k) for k in range(5)))
```

```python
import functools

import jax
import jax.numpy as jnp
from jax import lax
from jax.experimental import pallas as pl
from jax.experimental.pallas import tpu as pltpu

F32 = jnp.float32
BF16 = jnp.bfloat16

D_MODEL = 4096
W_A = D_MODEL // 4
W_B = D_MODEL // 2
W_C = D_MODEL // 4
IN_COLS = 4 * W_A + 2 * W_B + 2 * W_C
CONV_A = 3
CONV_B = 4
LRU_HEADS = 16
LRU_HD = W_B // LRU_HEADS
LRU_C = 8.0
S5_GROUP = 16
S5_G = W_C // S5_GROUP
S5_P = 64
S5_STATES = S5_G * S5_P
EPS = 1e-6

SUBLANES = 8
LANES = 128
MXU_DIM = 256
VMEM_LIMIT_CAP = 60000 * 1024
CH_TILE = MXU_DIM
S5_KT = W_C // CH_TILE
S5_Q = (CH_TILE // S5_GROUP) * S5_P // LANES
HALO = SUBLANES
N_TABLES = 16
GATE_CHUNK_ROWS = 128


def _params(sem, vmem_bytes):
    return pltpu.CompilerParams(
        dimension_semantics=sem,
        vmem_limit_bytes=int(min(VMEM_LIMIT_CAP, vmem_bytes)))


def _nbytes(shape, dtype):
    n = 1
    for s in shape:
        n *= s
    return n * jnp.dtype(dtype).itemsize


def _mod_kernel(c_ref, w_ref, b_ref, o_ref):
    c = c_ref[...].astype(BF16)
    w = w_ref[0].astype(BF16)
    o_ref[0] = jnp.dot(c, w, preferred_element_type=F32) + b_ref[0]


def _modulation(c_all, w_ada, b_ada):
    depth, d, n = w_ada.shape
    rows = c_all.shape[0]
    tn = 512
    vmem = 2 * (_nbytes((rows, d), F32) + _nbytes((d, tn), F32)
                + _nbytes((rows, tn), F32)) + _nbytes((d, tn), F32)
    return pl.pallas_call(
        _mod_kernel,
        grid=(depth, n // tn),
        in_specs=[
            pl.BlockSpec((rows, d), lambda l, j: (0, 0)),
            pl.BlockSpec((1, d, tn), lambda l, j: (l, 0, j)),
            pl.BlockSpec((1, 1, tn), lambda l, j: (l, 0, j)),
        ],
        out_specs=pl.BlockSpec((1, rows, tn), lambda l, j: (l, 0, j)),
        out_shape=jax.ShapeDtypeStruct((depth, rows, n), F32),
        compiler_params=_params(("arbitrary", "arbitrary"), vmem + (8 << 20)),
        name="adaln_modulation",
    )(c_all, w_ada, b_ada.reshape(depth, 1, n))


def _norm_mod_kernel(x_ref, scale_ref, shift_ref, g_ref, o_ref):
    x = x_ref[...]
    ms = jnp.mean(x * x, axis=-1, keepdims=True)
    y = (x * lax.rsqrt(ms + EPS)) * g_ref[...]
    h = y * (1.0 + scale_ref[...]) + shift_ref[...]
    o_ref[...] = h.reshape(o_ref.shape).astype(o_ref.dtype)


def _norm_modulate(x, scale, shift, g, bb, tt):
    b, t, d = x.shape
    nt = t // tt
    vmem = 2 * (_nbytes((bb, tt, d), F32) + _nbytes((bb, tt, d), BF16)) \
        + 4 * _nbytes((bb, tt, d), F32)
    return pl.pallas_call(
        _norm_mod_kernel,
        grid=(b // bb, nt),
        in_specs=[
            pl.BlockSpec((bb, tt, d), lambda i, j: (i, j, 0)),
            pl.BlockSpec((bb, 1, d), lambda i, j: (i, 0, 0)),
            pl.BlockSpec((bb, 1, d), lambda i, j: (i, 0, 0)),
            pl.BlockSpec((1, 1, d), lambda i, j: (0, 0, 0)),
        ],
        out_specs=pl.BlockSpec((bb * tt, d), lambda i, j: (i * nt + j, 0)),
        out_shape=jax.ShapeDtypeStruct((b * t, d), BF16),
        compiler_params=_params(("arbitrary", "arbitrary"), vmem),
        name="norm_modulate",
    )(x, scale, shift, g.reshape(1, 1, d))


def _final_norm_kernel(x_ref, g_ref, o_ref):
    x = x_ref[...]
    ms = jnp.mean(x * x, axis=-1, keepdims=True)
    o_ref[...] = (x * lax.rsqrt(ms + EPS)) * g_ref[...]


def _final_norm(x, g, bb, tt):
    b, t, d = x.shape
    vmem = 8 * _nbytes((bb, tt, d), F32)
    return pl.pallas_call(
        _final_norm_kernel,
        grid=(b // bb, t // tt),
        in_specs=[
            pl.BlockSpec((bb, tt, d), lambda i, j: (i, j, 0)),
            pl.BlockSpec((1, 1, d), lambda i, j: (0, 0, 0)),
        ],
        out_specs=pl.BlockSpec((bb, tt, d), lambda i, j: (i, j, 0)),
        out_shape=jax.ShapeDtypeStruct((b, t, d), F32),
        compiler_params=_params(("arbitrary", "arbitrary"), vmem),
        name="final_norm",
    )(x, g.reshape(1, 1, d))


PROJ_TN = 512
PROJ_ALONE_TILE = (512, 1024)


def _cast_weight_tile(w_ref, wb_ref, first_row_tile):
    @pl.when(first_row_tile)
    def _():
        wb_ref[...] = w_ref[0].astype(BF16)


def _in_proj_kernel(a_ref, w_ref, o_ref, wb_ref):
    _cast_weight_tile(w_ref, wb_ref, pl.program_id(1) == 0)
    o_ref[...] = jnp.dot(a_ref[...], wb_ref[...], preferred_element_type=F32)


def _in_proj_operands(h2d, w_in, layer, col0, ncols, tm, tn, idx):
    m, k = h2d.shape
    assert col0 % tn == 0 and ncols % tn == 0 and m % tm == 0
    return dict(
        args=[h2d, w_in],
        in_specs=[
            pl.BlockSpec((tm, k), lambda *g: (idx(*g)[1], 0)),
            pl.BlockSpec((1, k, tn), lambda *g: (layer, 0, col0 // tn + idx(*g)[0])),
        ],
        out_specs=[pl.BlockSpec((tm, tn), lambda *g: (idx(*g)[1], idx(*g)[0]))],
        out_shape=[jax.ShapeDtypeStruct((m, ncols), F32)],
        scratch=[pltpu.VMEM((k, tn), BF16)],
        vmem=2 * (_nbytes((tm, k), BF16) + _nbytes((k, tn), F32) + _nbytes((tm, tn), F32))
        + _nbytes((k, tn), BF16) + _nbytes((tm, tn), F32),
    )


def _in_projection(h2d, w_in, layer, col0, ncols):
    m = h2d.shape[0]
    tm, tn = PROJ_ALONE_TILE
    ops = _in_proj_operands(h2d, w_in, layer, col0, ncols, tm, tn, lambda j, i: (j, i))
    return pl.pallas_call(
        _in_proj_kernel,
        grid=(ncols // tn, m // tm),
        in_specs=ops["in_specs"],
        out_specs=ops["out_specs"][0],
        out_shape=ops["out_shape"][0],
        scratch_shapes=ops["scratch"],
        compiler_params=_params(("arbitrary", "arbitrary"), ops["vmem"] + (4 << 20)),
        name="in_projection",
    )(*ops["args"])


def _causal_taps(scr, v, w_ref, taps, tt):
    scr[:, HALO:HALO + tt, :] = v
    acc = None
    for k in range(taps):
        src = v if k == taps - 1 else scr[:, HALO - (taps - 1) + k:HALO - (taps - 1) + k + tt, :]
        term = w_ref[k:k + 1, :] * src
        acc = term if acc is None else acc + term
    return acc


def _block_scan_real(a, b):
    row = lax.broadcasted_iota(jnp.int32, (1, SUBLANES, a.shape[-1]), 1)
    for d in (1, 2, 4):
        keep = row >= d
        a_sh = pltpu.roll(a, d, 1)
        b_sh = pltpu.roll(b, d, 1)
        b = b + a * jnp.where(keep, b_sh, 0.0)
        a = a * jnp.where(keep, a_sh, 1.0)
    return a, b


def _block_scan_cplx(br, bi, tab_ref, kt, base, sl):
    for lvl, d in enumerate((1, 2, 4)):
        lr = tab_ref[kt, base + 2 * lvl, :, sl]
        li = tab_ref[kt, base + 2 * lvl + 1, :, sl]
        sr = pltpu.roll(br, d, 1)
        si = pltpu.roll(bi, d, 1)
        br, bi = br + (lr * sr - li * si), bi + (lr * si + li * sr)
    return br, bi


def _spread_rows(scr, nblk, bb):
    return jnp.concatenate(
        [jnp.broadcast_to(scr[:, k:k + 1, :], (bb, SUBLANES, LANES)) for k in range(nblk)],
        axis=1)


def _mixer_a_kernel(ab_ref, ac_ref, ax_ref, ag_ref, w_ref, b_ref, st_ref,
                    o_ref, st_out_ref, scr, *, tt):
    keep = CONV_A - 1

    @pl.when(pl.program_id(2) == 0)
    def _():
        scr[:, HALO - keep:HALO, :] = st_ref[...]

    conv_in = ac_ref[...] * ax_ref[...]
    y = b_ref[...] + _causal_taps(scr, conv_in, w_ref, CONV_A, tt)
    out = (ab_ref[...] * y) * jax.nn.silu(ag_ref[...])
    o_ref[...] = out.reshape(o_ref.shape).astype(o_ref.dtype)
    tail = scr[:, HALO + tt - keep:HALO + tt, :]
    st_out_ref[...] = tail
    scr[:, HALO - keep:HALO, :] = tail


def _mixer_a(proj3, w_conv, b_conv, state, bb, tt):
    b, t, _ = proj3.shape
    cw = CH_TILE
    nc = W_A // cw
    nt = t // tt
    blk = (bb, tt, cw)

    def col(off):
        return pl.BlockSpec(blk, lambda i, c, j, off=off: (i, j, off // cw + c))

    vmem = 2 * (4 * _nbytes(blk, F32) + _nbytes(blk, BF16)) \
        + _nbytes((bb, HALO + tt, cw), F32) + 6 * _nbytes(blk, F32)
    return pl.pallas_call(
        functools.partial(_mixer_a_kernel, tt=tt),
        grid=(b // bb, nc, nt),
        in_specs=[
            col(0), col(W_A), col(2 * W_A), col(3 * W_A),
            pl.BlockSpec((CONV_A, cw), lambda i, c, j: (0, c)),
            pl.BlockSpec((1, cw), lambda i, c, j: (0, c)),
            pl.BlockSpec((bb, CONV_A - 1, cw), lambda i, c, j: (i, 0, c)),
        ],
        out_specs=[
            pl.BlockSpec((bb * tt, cw), lambda i, c, j: (i * nt + j, c)),
            pl.BlockSpec((bb, CONV_A - 1, cw), lambda i, c, j: (i, 0, c)),
        ],
        out_shape=[
            jax.ShapeDtypeStruct((b * t, W_A), BF16),
            jax.ShapeDtypeStruct((b, CONV_A - 1, W_A), F32),
        ],
        scratch_shapes=[pltpu.VMEM((bb, HALO + tt, cw), F32)],
        compiler_params=_params(("arbitrary",) * 3, vmem),
        name="mixer_a",
    )(proj3, proj3, proj3, proj3, w_conv, b_conv.reshape(1, W_A), state)


MIXER_B_INS = 10
MIXER_B_OUTS = 3


def _mixer_b_init(cst_ref, hst_ref, scr, carry, first_time_tile):
    @pl.when(first_time_tile)
    def _():
        scr[:, HALO - (CONV_B - 1):HALO, :] = cst_ref[...]
        carry[...] = hst_ref[...]


def _mixer_b_kernel(*refs, bb, tt):
    ins = refs[:MIXER_B_INS]
    outs = refs[MIXER_B_INS:MIXER_B_INS + MIXER_B_OUTS]
    scr, carry, *scan_scr = refs[MIXER_B_INS + MIXER_B_OUTS:]
    _mixer_b_init(ins[8], ins[9], scr, carry, pl.program_id(2) == 0)
    _mixer_b_step(*ins[:8], *outs, scr, carry, scan_scr, bb=bb, tt=tt)


def _mixer_b_step(bx_ref, bg_ref, w_ref, b_ref, wg_ref, brg_ref, big_ref, lam_ref,
                  o_ref, cst_out_ref, hst_out_ref, scr, carry, scan_scr, *, bb, tt,
                  side_matmul=None):
    keep = CONV_B - 1
    cw = CH_TILE
    rows = bb * tt
    nblk = tt // SUBLANES
    nv = nblk // SUBLANES

    scr[:, HALO:HALO + tt, :] = bx_ref[...]
    cst_out_ref[...] = scr[:, HALO + tt - keep:HALO + tt, :]
    softplus_neg_lam = jax.nn.softplus(-lam_ref[...])
    cin_all = carry[...]
    h_parts, last_parts = [], []
    assert LRU_HD == LANES
    n_heads = cw // LRU_HD
    n_chunks = 1 if side_matmul is None else rows // GATE_CHUNK_ROWS
    rc = rows // n_chunks
    for s in range(n_heads):
        sl = slice(s * LANES, (s + 1) * LANES)
        w_head = jnp.concatenate([wg_ref[0, sl, sl], wg_ref[0, sl, cw + s * LANES:cw + (s + 1) * LANES]],
                                 axis=1)
        a_chunks, b_chunks = [], []
        for k in range(n_chunks):
            acc = None
            for j in range(CONV_B):
                lo = HALO - keep + j
                if tt >= rc:
                    b0, t0 = divmod(k * rc, tt)
                    win = scr[b0:b0 + 1, lo + t0:lo + t0 + rc, sl]
                else:
                    nb = rc // tt
                    win = scr[k * nb:(k + 1) * nb, lo:lo + tt, sl]
                term = w_ref[j:j + 1, sl] * win.reshape(rc, LANES)
                acc = term if acc is None else acc + term
            xb = b_ref[:, sl] + acc
            gates = jnp.dot(xb.astype(BF16), w_head, preferred_element_type=F32)
            if side_matmul is not None:
                side_matmul(s * n_chunks + k, n_heads * n_chunks)
            r = jax.nn.sigmoid(gates[:, :LANES] + brg_ref[:, sl])
            ig = jax.nn.sigmoid(gates[:, LANES:] + big_ref[:, sl])
            log_a = (-LRU_C * r) * softplus_neg_lam[:, sl]
            a = jnp.exp(log_a)
            beta = jnp.sqrt(-jnp.tanh(log_a) * (a * a + 1.0))
            bt = (beta * ig) * xb
            a_k, b_k = _block_scan_real(a.reshape(rc // SUBLANES, SUBLANES, LANES),
                                        bt.reshape(rc // SUBLANES, SUBLANES, LANES))
            a_chunks.append(a_k)
            b_chunks.append(b_k)
        a_blk = jnp.concatenate(a_chunks, axis=0)
        b_blk = jnp.concatenate(b_chunks, axis=0)
        cin = cin_all[:, :, sl]
        if nblk == 1:
            h = b_blk + a_blk * cin
            last = h[:, SUBLANES - 1:SUBLANES, :]
        else:
            acum, bcum, cscr = scan_scr
            acum[s] = a_blk.reshape(bb, tt, LANES)
            bcum[s] = b_blk.reshape(bb, tt, LANES)
            a2 = acum[s, :, pl.ds(SUBLANES - 1, nblk, stride=SUBLANES), :]
            b2 = bcum[s, :, pl.ds(SUBLANES - 1, nblk, stride=SUBLANES), :]
            a2, b2 = _block_scan_real(a2.reshape(bb * nv, SUBLANES, LANES),
                                      b2.reshape(bb * nv, SUBLANES, LANES))
            a2 = a2.reshape(bb, nblk, LANES)
            b2 = b2.reshape(bb, nblk, LANES)
            prev = cin
            ends = []
            for j in range(nv):
                seg = slice(j * SUBLANES, (j + 1) * SUBLANES)
                e = b2[:, seg, :] + a2[:, seg, :] * prev
                ends.append(e)
                prev = e[:, SUBLANES - 1:SUBLANES, :]
            last = prev
            cscr[s, :, 0:1, :] = cin
            cscr[s, :, 1:1 + nblk, :] = jnp.concatenate(ends, axis=1)
            spread = _spread_rows(cscr.at[s], nblk, bb)
            h = b_blk + a_blk * spread.reshape(bb * nblk, SUBLANES, LANES)
        h_parts.append(h.reshape(bb, tt, LANES))
        last_parts.append(last)
    h = jnp.concatenate(h_parts, axis=-1)
    last = jnp.concatenate(last_parts, axis=-1)
    carry[...] = last
    hst_out_ref[...] = last
    out = h * jax.nn.silu(bg_ref[...])
    o_ref[...] = out.reshape(o_ref.shape).astype(o_ref.dtype)
    scr[:, HALO - keep:HALO, :] = scr[:, HALO + tt - keep:HALO + tt, :]


def _mixer_b_operands(proj3, w_conv, b_conv, w_gates, b_rg, b_ig, lam, conv_state, h_state,
                      bb, tt, idx):
    b, t, _ = proj3.shape
    cw = CH_TILE
    nt = t // tt
    nblk = tt // SUBLANES
    blk = (bb, tt, cw)

    def spec(shape, fn):
        return pl.BlockSpec(shape, lambda *g: fn(*idx(*g)))

    vec = spec((1, cw), lambda i, c, j: (0, c))
    scratch = [pltpu.VMEM((bb, HALO + tt, cw), F32), pltpu.VMEM((bb, 1, cw), F32)]
    if nblk > 1:
        assert nblk % SUBLANES == 0
        slabs = cw // LANES
        scratch += [pltpu.VMEM((slabs, bb, tt, LANES), F32),
                    pltpu.VMEM((slabs, bb, tt, LANES), F32),
                    pltpu.VMEM((slabs, bb, HALO + nblk, LANES), F32)]
    return dict(
        args=[proj3, proj3, w_conv, b_conv.reshape(1, W_B), w_gates,
              b_rg.reshape(1, W_B), b_ig.reshape(1, W_B), lam.reshape(1, W_B),
              conv_state, h_state],
        in_specs=[
            spec(blk, lambda i, c, j: (i, j, c)),
            spec(blk, lambda i, c, j: (i, j, W_B // cw + c)),
            spec((CONV_B, cw), lambda i, c, j: (0, c)),
            vec,
            spec((1, cw, 2 * cw), lambda i, c, j: (c, 0, 0)),
            vec, vec, vec,
            spec((bb, CONV_B - 1, cw), lambda i, c, j: (i, 0, c)),
            spec((bb, 1, cw), lambda i, c, j: (i, 0, c)),
        ],
        out_specs=[
            spec((bb * tt, cw), lambda i, c, j: (i * nt + j, c)),
            spec((bb, CONV_B - 1, cw), lambda i, c, j: (i, 0, c)),
            spec((bb, 1, cw), lambda i, c, j: (i, 0, c)),
        ],
        out_shape=[
            jax.ShapeDtypeStruct((b * t, W_B), BF16),
            jax.ShapeDtypeStruct((b, CONV_B - 1, W_B), F32),
            jax.ShapeDtypeStruct((b, 1, W_B), F32),
        ],
        scratch=scratch,
        vmem=2 * (2 * _nbytes(blk, F32) + _nbytes(blk, BF16) + _nbytes((cw, 2 * cw), BF16))
        + _nbytes((bb, HALO + tt, cw), F32) + 16 * _nbytes(blk, F32),
        grid=(b // bb, W_B // cw, nt),
    )


def _mixer_b(proj3, *params, bb, tt):
    ops = _mixer_b_operands(proj3, *params, bb, tt, lambda i, c, j: (i, c, j))
    return pl.pallas_call(
        functools.partial(_mixer_b_kernel, bb=bb, tt=tt),
        grid=ops["grid"],
        in_specs=ops["in_specs"],
        out_specs=ops["out_specs"],
        out_shape=ops["out_shape"],
        scratch_shapes=ops["scratch"],
        compiler_params=_params(("arbitrary",) * 3, ops["vmem"]),
        name="mixer_b",
    )(*ops["args"])


def _fused_proj_kernel(*refs, mixers, ni):
    a_ref, w_ref = refs[:2]
    mixers = [dict(mx) for mx in mixers]
    pos = 2
    for mx in mixers:
        mx["ins"] = refs[pos:pos + mx["n_ins"]]
        pos += mx["n_ins"]
    proj_ref = refs[pos]
    pos += 1
    for mx in mixers:
        mx["outs"] = refs[pos:pos + mx["n_outs"]]
        pos += mx["n_outs"]
    wb_ref = refs[pos]
    pos += 1
    for mx in mixers:
        mx["scr"] = refs[pos:pos + mx["n_scr"]]
        pos += mx["n_scr"]
    s = pl.program_id(0)
    _cast_weight_tile(w_ref, wb_ref, lax.rem(s, ni) == 0)
    for mx in mixers:
        mx["init"](mx["ins"], mx["scr"], s)
    k_total = a_ref.shape[1]
    acc = []

    def projection_part(part, n_parts):
        kc = MXU_DIM
        per_part = k_total // (kc * n_parts)
        assert per_part * kc * n_parts == k_total
        for sub in range(part * per_part, (part + 1) * per_part):
            ks = slice(sub * kc, (sub + 1) * kc)
            term = jnp.dot(a_ref[:, ks], wb_ref[ks, :], preferred_element_type=F32)
            acc[:] = [term if not acc else acc[0] + term]
        if part == n_parts - 1:
            proj_ref[...] = acc[0]

    for mx in mixers[:-1]:
        mx["step"](mx["ins"], mx["outs"], mx["scr"], s, None)
    mx = mixers[-1]
    mx["step"](mx["ins"], mx["outs"], mx["scr"], s, projection_part)


def _fused_proj_call(name, h2d, w_in, layer, col0, ncols, mixers):
    steps = None
    for ops, _, _ in mixers:
        n = 1
        for g in ops["grid"]:
            n *= g
        assert steps in (None, n)
        steps = n
    nj = ncols // PROJ_TN
    ni = steps // nj
    tm = h2d.shape[0] // ni
    assert nj * ni == steps and tm * ni == h2d.shape[0] and tm % SUBLANES == 0
    mm = _in_proj_operands(h2d, w_in, layer, col0, ncols, tm, PROJ_TN,
                           lambda s: (s // ni, s % ni))
    specs = [dict(n_ins=len(ops["args"]), n_outs=len(ops["out_shape"]),
                  n_scr=len(ops["scratch"]), init=init, step=step)
             for ops, init, step in mixers]
    cat = lambda key: [x for ops, _, _ in mixers for x in ops[key]]
    return pl.pallas_call(
        functools.partial(_fused_proj_kernel, mixers=specs, ni=ni),
        grid=(steps,),
        in_specs=mm["in_specs"] + cat("in_specs"),
        out_specs=mm["out_specs"] + cat("out_specs"),
        out_shape=mm["out_shape"] + cat("out_shape"),
        scratch_shapes=mm["scratch"] + cat("scratch"),
        compiler_params=_params(("arbitrary",),
                                mm["vmem"] + sum(ops["vmem"] for ops, _, _ in mixers)),
        name=name,
    )(*mm["args"], *cat("args"))


def _flat_index(dims):
    def idx(s):
        out = []
        for k, d in enumerate(dims):
            stride = 1
            for e in dims[k + 1:]:
                stride *= e
            out.append((s // stride) % d if k else s // stride)
        return tuple(out)
    return idx


def _flat_mixer_b(proj3, *params, bb, tt):
    b, t, _ = proj3.shape
    dims = (b // bb, W_B // CH_TILE, t // tt)
    ops = _mixer_b_operands(proj3, *params, bb, tt, _flat_index(dims))
    nt = dims[2]

    def init(ins, scr, s):
        _mixer_b_init(ins[8], ins[9], scr[0], scr[1], lax.rem(s, nt) == 0)

    def step(ins, outs, scr, s, side_matmul):
        _mixer_b_step(*ins[:8], *outs, scr[0], scr[1], scr[2:], bb=bb, tt=tt,
                      side_matmul=side_matmul)

    return ops, init, step


def _s5_disc_kernel(lre_ref, lim_ref, ldt_ref, btre_ref, btim_ref,
                    abre_ref, abim_ref, bbre_ref, bbim_ref):
    lre = lre_ref[0]
    lim = lim_ref[0]
    dt = jnp.exp(ldt_ref[0])
    mag = jnp.exp(lre * dt)
    ab_re = mag * jnp.cos(lim * dt)
    ab_im = mag * jnp.sin(lim * dt)
    nr = ab_re - 1.0
    den = lre * lre + lim * lim
    fr = (nr * lre + ab_im * lim) / den
    fi = (ab_im * lre - nr * lim) / den
    bre = btre_ref[0]
    bim = btim_ref[0]
    bbre_ref[0] = fr * bre - fi * bim
    bbim_ref[0] = fr * bim + fi * bre
    abre_ref[0] = ab_re
    abim_ref[0] = ab_im


def _s5_discretize(lam_re, lam_im, log_dt, b_re, b_im):
    depth = lam_re.shape[0]
    gp = (depth, S5_G, 1, S5_P)
    ldt = jnp.broadcast_to(log_dt[:, :, None, None], gp)
    bt_shape = (depth, S5_G, S5_GROUP, S5_P)
    small = pl.BlockSpec((1, S5_G, 1, S5_P), lambda l: (l, 0, 0, 0))
    big = pl.BlockSpec((1, S5_G, S5_GROUP, S5_P), lambda l: (l, 0, 0, 0))
    ab_re, ab_im, bb_re, bb_im = pl.pallas_call(
        _s5_disc_kernel,
        grid=(depth,),
        in_specs=[small, small, small, big, big],
        out_specs=[small, small, big, big],
        out_shape=[jax.ShapeDtypeStruct(gp, F32), jax.ShapeDtypeStruct(gp, F32),
                   jax.ShapeDtypeStruct(bt_shape, F32), jax.ShapeDtypeStruct(bt_shape, F32)],
        name="s5_discretize",
    )(lam_re.reshape(gp), lam_im.reshape(gp), ldt,
      jnp.swapaxes(b_re, 2, 3), jnp.swapaxes(b_im, 2, 3))
    return (ab_re.reshape(depth, 1, S5_STATES), ab_im.reshape(depth, 1, S5_STATES),
            bb_re, bb_im)


def _s5_pow_kernel(ar_ref, ai_ref, tab_ref):
    def cmul(x, y):
        return x[0] * y[0] - x[1] * y[1], x[0] * y[1] + x[1] * y[0]

    def powers(b1):
        b2 = cmul(b1, b1)
        b3 = cmul(b2, b1)
        b4 = cmul(b2, b2)
        return (b1, b2, b3, b4, cmul(b4, b1), cmul(b4, b2), cmul(b4, b3), cmul(b4, b4))

    shape = (SUBLANES, S5_STATES)
    row = lax.broadcasted_iota(jnp.int32, shape, 0)
    zero = jnp.zeros(shape, F32)
    pw_a = powers((ar_ref[0], ai_ref[0]))
    pw_a8 = powers(pw_a[SUBLANES - 1])
    for base, pw in ((0, pw_a), (N_TABLES // 2, pw_a8)):
        for part in (0, 1):
            tab_ref[0, base + part] = jnp.where(row >= 1, pw[0][part], zero)
            tab_ref[0, base + 2 + part] = jnp.where(row >= 2, pw[1][part], zero)
            tab_ref[0, base + 4 + part] = jnp.where(row >= 4, pw[3][part], zero)
            p = zero
            for r, v in enumerate(pw):
                p = jnp.where(row == r, v[part], p)
            tab_ref[0, base + 6 + part] = p


def _s5_power_tables(ab_re, ab_im):
    depth = ab_re.shape[0]
    vec = pl.BlockSpec((1, 1, S5_STATES), lambda l: (l, 0, 0))
    tables = pl.pallas_call(
        _s5_pow_kernel,
        grid=(depth,),
        in_specs=[vec, vec],
        out_specs=pl.BlockSpec((1, N_TABLES, SUBLANES, S5_STATES), lambda l: (l, 0, 0, 0)),
        out_shape=jax.ShapeDtypeStruct((depth, N_TABLES, SUBLANES, S5_STATES), F32),
        name="s5_power_tables",
    )(ab_re, ab_im)
    kslab = S5_STATES // S5_KT
    tables = tables.reshape(depth, N_TABLES, SUBLANES, S5_KT, kslab)
    return jnp.transpose(tables, (0, 3, 1, 2, 4)).reshape(
        depth * S5_KT, N_TABLES, SUBLANES, kslab)


def _s5_block_matrices(bb_re, bb_im, c_re, c_im):
    depth = bb_re.shape[0]
    gpt = CH_TILE // S5_GROUP
    chan = lax.broadcasted_iota(jnp.int32, (S5_Q, CH_TILE, CH_TILE), 1)
    state = lax.broadcasted_iota(jnp.int32, (S5_Q, CH_TILE, CH_TILE), 2)
    slab = lax.broadcasted_iota(jnp.int32, (S5_Q, CH_TILE, CH_TILE), 0)
    gps = LANES // S5_P
    keep = (chan // S5_GROUP) == gps * slab + (state % LANES) // S5_P

    def dense_b(x):
        return x.reshape(depth * S5_KT, gpt * S5_GROUP, S5_P)

    db = jnp.concatenate([dense_b(bb_re)] * gps + [dense_b(bb_im)] * gps, axis=-1)
    bm = jnp.where(keep[None], db[:, None], 0.0).astype(BF16)

    def dense_c(x):
        x = x.reshape(depth * S5_KT, gpt, S5_GROUP, S5_P)
        return jnp.transpose(x, (0, 3, 1, 2)).reshape(depth * S5_KT, S5_P, gpt * S5_GROUP)

    dc = jnp.concatenate([dense_c(c_re)] * gps + [dense_c(-c_im)] * gps, axis=1)
    keep_c = jnp.swapaxes(keep, 1, 2)
    cm = jnp.where(keep_c[None], dc[:, None], 0.0).astype(BF16)
    return bm, cm


MIXER_C_INS = 10
MIXER_C_OUTS = 3


def _mixer_c_init(sre_ref, sim_ref, car_r, car_i, kt, first_time_tile):
    @pl.when(first_time_tile)
    def _():
        car_r[kt] = sre_ref[...]
        car_i[kt] = sim_ref[...]


def _mixer_c_kernel(*refs, bb, tt):
    ins = refs[:MIXER_C_INS]
    outs = refs[MIXER_C_INS:MIXER_C_INS + MIXER_C_OUTS]
    ystore, car_r, car_i, *scan_scr = refs[MIXER_C_INS + MIXER_C_OUTS:]
    kt = pl.program_id(2)
    _mixer_c_init(ins[8], ins[9], car_r, car_i, kt, pl.program_id(1) == 0)
    _mixer_c_step(*ins[:8], *outs, ystore, car_r, car_i, scan_scr, kt, bb=bb, tt=tt)


def _mixer_c_step(cu_ref, cg_ref, bm_ref, cm_ref, tab_ref, d_ref, wglu_ref, bglu_ref,
                  o_ref, sre_out_ref, sim_out_ref,
                  ystore, car_r, car_i, scan_scr, kt, *, bb, tt, side_matmul=None):
    cw = CH_TILE
    rows = bb * tt
    nblk = tt // SUBLANES
    nv = nblk // SUBLANES
    n = bb * nblk
    base8 = N_TABLES // 2

    u = cu_ref[...].reshape(rows, cw)
    ub = u.astype(BF16)
    y = d_ref[...] * u
    cr_all = car_r[kt]
    ci_all = car_i[kt]
    new_r, new_i = [], []
    for q in range(S5_Q):
        sl = slice(LANES * q, LANES * (q + 1))
        bu = jnp.dot(ub, bm_ref[kt, q], preferred_element_type=F32)
        if side_matmul is not None:
            side_matmul(q, S5_Q)
        br, bi = _block_scan_cplx(bu[:, :LANES].reshape(n, SUBLANES, LANES),
                                  bu[:, LANES:].reshape(n, SUBLANES, LANES), tab_ref, kt, 0, sl)
        p_r = tab_ref[kt, 6, :, sl]
        p_i = tab_ref[kt, 7, :, sl]
        cr = cr_all[:, :, sl]
        ci = ci_all[:, :, sl]
        if nblk == 1:
            kr, ki = cr, ci
        else:
            hsr, hsi, csr, csi = scan_scr
            hsr[...] = br.reshape(bb, tt, LANES)
            hsi[...] = bi.reshape(bb, tt, LANES)
            xr = hsr[:, pl.ds(SUBLANES - 1, nblk, stride=SUBLANES), :]
            xi = hsi[:, pl.ds(SUBLANES - 1, nblk, stride=SUBLANES), :]
            xr, xi = _block_scan_cplx(xr.reshape(bb * nv, SUBLANES, LANES),
                                      xi.reshape(bb * nv, SUBLANES, LANES), tab_ref, kt, base8,
                                      sl)
            xr = xr.reshape(bb, nblk, LANES)
            xi = xi.reshape(bb, nblk, LANES)
            q_r = tab_ref[kt, base8 + 6, :, sl]
            q_i = tab_ref[kt, base8 + 7, :, sl]
            prev_r, prev_i = cr, ci
            ends_r, ends_i = [], []
            for j in range(nv):
                seg = slice(j * SUBLANES, (j + 1) * SUBLANES)
                er = xr[:, seg, :] + (q_r * prev_r - q_i * prev_i)
                ei = xi[:, seg, :] + (q_r * prev_i + q_i * prev_r)
                ends_r.append(er)
                ends_i.append(ei)
                prev_r = er[:, SUBLANES - 1:SUBLANES, :]
                prev_i = ei[:, SUBLANES - 1:SUBLANES, :]
            csr[:, 0:1, :] = cr
            csi[:, 0:1, :] = ci
            csr[:, 1:1 + nblk, :] = jnp.concatenate(ends_r, axis=1)
            csi[:, 1:1 + nblk, :] = jnp.concatenate(ends_i, axis=1)
            kr = _spread_rows(csr, nblk, bb).reshape(n, SUBLANES, LANES)
            ki = _spread_rows(csi, nblk, bb).reshape(n, SUBLANES, LANES)
        hr = br + (p_r * kr - p_i * ki)
        hi = bi + (p_r * ki + p_i * kr)
        hr = hr.reshape(bb, tt, LANES)
        hi = hi.reshape(bb, tt, LANES)
        new_r.append(hr[:, tt - 1:tt, :])
        new_i.append(hi[:, tt - 1:tt, :])
        hcat = jnp.concatenate([hr.reshape(rows, LANES), hi.reshape(rows, LANES)], axis=-1)
        y = y + jnp.dot(hcat.astype(BF16), cm_ref[kt, q], preferred_element_type=F32)

    ncr_all = jnp.concatenate(new_r, axis=-1)
    nci_all = jnp.concatenate(new_i, axis=-1)
    car_r[kt] = ncr_all
    car_i[kt] = nci_all
    sre_out_ref[:, kt] = ncr_all
    sim_out_ref[:, kt] = nci_all
    ystore[kt] = y

    @pl.when(kt == S5_KT - 1)
    def _():
        y_all = jnp.concatenate([ystore[k] for k in range(S5_KT)], axis=-1)
        yg = jax.nn.gelu(y_all)
        z = jnp.dot(yg.astype(BF16), wglu_ref[0].astype(BF16),
                    preferred_element_type=F32) + bglu_ref[...]
        yy = yg * jax.nn.sigmoid(z)
        out = yy * jax.nn.silu(cg_ref[...].reshape(rows, W_C))
        o_ref[...] = out.astype(o_ref.dtype)


def _mixer_c_operands(proj3, bmat, cmat, tables, d_skip, w_glu, b_glu, s_re, s_im, layer,
                      bb, tt, idx):
    b, t, _ = proj3.shape
    u_col = 0
    cw = CH_TILE
    nt = t // tt
    nblk = tt // SUBLANES
    kslab = S5_Q * LANES

    def spec(shape, fn):
        return pl.BlockSpec(shape, lambda *g: fn(*idx(*g)))

    mat = spec((S5_KT, S5_Q, cw, cw), lambda i, j, k: (layer, 0, 0, 0))
    st = spec((bb, 1, kslab), lambda i, j, k: (i, 0, k))
    st_out = spec((bb, S5_KT, 1, kslab), lambda i, j, k: (i, 0, 0, 0))
    rows = bb * tt
    scratch = [pltpu.VMEM((S5_KT, rows, cw), F32),
               pltpu.VMEM((S5_KT, bb, 1, kslab), F32),
               pltpu.VMEM((S5_KT, bb, 1, kslab), F32)]
    if nblk > 1:
        assert nblk % SUBLANES == 0 and nblk // SUBLANES <= SUBLANES
        scratch += [pltpu.VMEM((bb, tt, LANES), F32), pltpu.VMEM((bb, tt, LANES), F32),
                    pltpu.VMEM((bb, HALO + nblk, LANES), F32),
                    pltpu.VMEM((bb, HALO + nblk, LANES), F32)]
    vmem = 2 * (_nbytes((bb, tt, cw), F32) + _nbytes((bb, tt, W_C), F32)
                + 2 * _nbytes((S5_KT, S5_Q, cw, cw), BF16)
                + _nbytes((S5_KT, N_TABLES, SUBLANES, kslab), F32)
                + _nbytes((W_C, W_C), F32) + _nbytes((bb, tt, W_C), BF16)
                + 4 * _nbytes((bb, SUBLANES, kslab), F32)) \
        + _nbytes((S5_KT, rows, cw), F32) + 2 * _nbytes((bb, tt, LANES), F32) \
        + 2 * _nbytes((S5_KT, bb, SUBLANES, kslab), F32) + 10 * _nbytes((rows, W_C), F32)
    return dict(
        args=[proj3, proj3, bmat, cmat, tables, d_skip.reshape(1, W_C), w_glu,
              b_glu.reshape(1, W_C), s_re, s_im],
        in_specs=[
            spec((bb, tt, cw), lambda i, j, k: (i, j, u_col // cw + k)),
            spec((bb, tt, W_C), lambda i, j, k: (i, j, u_col // W_C + 1)),
            mat, mat,
            spec((S5_KT, N_TABLES, SUBLANES, kslab), lambda i, j, k: (layer, 0, 0, 0)),
            spec((1, cw), lambda i, j, k: (0, k)),
            spec((1, W_C, W_C), lambda i, j, k: (layer, 0, 0)),
            spec((1, W_C), lambda i, j, k: (0, 0)),
            st, st,
        ],
        out_specs=[
            spec((rows, W_C), lambda i, j, k: (i * nt + j, 0)),
            st_out, st_out,
        ],
        out_shape=[
            jax.ShapeDtypeStruct((b * t, W_C), BF16),
            jax.ShapeDtypeStruct((b, S5_KT, 1, kslab), F32),
            jax.ShapeDtypeStruct((b, S5_KT, 1, kslab), F32),
        ],
        scratch=scratch,
        vmem=vmem,
        grid=(b // bb, nt, S5_KT),
    )


def _mixer_c(proj3, *params, bb, tt):
    ops = _mixer_c_operands(proj3, *params, bb, tt, lambda i, j, k: (i, j, k))
    return pl.pallas_call(
        functools.partial(_mixer_c_kernel, bb=bb, tt=tt),
        grid=ops["grid"],
        in_specs=ops["in_specs"],
        out_specs=ops["out_specs"],
        out_shape=ops["out_shape"],
        scratch_shapes=ops["scratch"],
        compiler_params=_params(("arbitrary",) * 3, ops["vmem"]),
        name="mixer_c",
    )(*ops["args"])


def _flat_mixer_c(proj3, *params, bb, tt):
    b, t, _ = proj3.shape
    dims = (b // bb, t // tt, S5_KT)
    ops = _mixer_c_operands(proj3, *params, bb, tt, _flat_index(dims))
    nt = dims[1]

    def init(ins, scr, s):
        _mixer_c_init(ins[8], ins[9], scr[1], scr[2], lax.rem(s, S5_KT),
                      lax.rem(s // S5_KT, nt) == 0)

    def step(ins, outs, scr, s, side_matmul):
        _mixer_c_step(*ins[:8], *outs, scr[0], scr[1], scr[2], scr[3:],
                      lax.rem(s, S5_KT), bb=bb, tt=tt, side_matmul=side_matmul)

    return ops, init, step


OUT_PROJ_TN = 512


def _out_proj_kernel(ma_ref, mb_ref, mc_ref, w_ref, x_ref, gate_ref, o_ref, *, bb, tt):
    w = w_ref[0].astype(BF16)
    acc = jnp.dot(ma_ref[...], w[0:W_A, :], preferred_element_type=F32)
    acc = acc + jnp.dot(mb_ref[...], w[W_A:W_A + W_B, :], preferred_element_type=F32)
    acc = acc + jnp.dot(mc_ref[...], w[W_A + W_B:, :], preferred_element_type=F32)
    o_ref[...] = x_ref[...] + gate_ref[...] * acc.reshape(bb, tt, acc.shape[-1])


def _out_projection(out_a, out_b, out_c, w_out, x, gate, layer, bb, tt):
    b, t, d = x.shape
    tm = bb * tt
    tn = OUT_PROJ_TN
    nt = t // tt
    m = b * t

    def rows(width):
        return pl.BlockSpec((tm, width), lambda i, j: (i, 0))

    vmem = 2 * (_nbytes((tm, d), BF16) + _nbytes((d, tn), F32)
                + 2 * _nbytes((tm, tn), F32)) + _nbytes((d, tn), BF16) \
        + 3 * _nbytes((tm, tn), F32)
    return pl.pallas_call(
        functools.partial(_out_proj_kernel, bb=bb, tt=tt),
        grid=(m // tm, d // tn),
        in_specs=[
            rows(W_A), rows(W_B), rows(W_C),
            pl.BlockSpec((1, d, tn), lambda i, j: (layer, 0, j)),
            pl.BlockSpec((bb, tt, tn), lambda i, j: (i // nt, i % nt, j)),
            pl.BlockSpec((bb, 1, tn), lambda i, j: (i // nt, 0, j)),
        ],
        out_specs=pl.BlockSpec((bb, tt, tn), lambda i, j: (i // nt, i % nt, j)),
        out_shape=jax.ShapeDtypeStruct((b, t, d), F32),
        compiler_params=_params(("arbitrary", "arbitrary"), vmem + (4 << 20)),
        name="out_projection",
    )(out_a, out_b, out_c, w_out, x, gate)


def _gate_weights(w_rg, w_ig):
    hpt = CH_TILE // LRU_HD
    pairs = LRU_HEADS // hpt
    w = jnp.stack([w_rg, w_ig], axis=2)
    w = w.reshape(pairs, hpt, LRU_HD, 2, LRU_HD)
    eye = jnp.eye(hpt, dtype=bool)
    out = jnp.where(eye[None, :, None, None, :, None],
                    w[:, :, :, :, None, :], 0.0)
    return out.reshape(pairs, CH_TILE, 2 * CH_TILE).astype(BF16)


def _layer(x, mod, states, wts, layer, tiles):
    b, t, d = x.shape
    conv_a, conv_b, lru_h, s5_re, s5_im = states
    shift, scale, gate = (mod[:, None, k * d:(k + 1) * d] for k in range(3))

    h = _norm_modulate(x, scale, shift, wts["g_norm"], *tiles["norm"])
    w_in = wts["w_in"]
    col_a, col_b, col_c = 0, 4 * W_A, 4 * W_A + 2 * W_B
    b_params = (wts["w_conv_b"], wts["b_conv_b"], wts["w_gates"], wts["b_rg"], wts["b_ig"],
                wts["lru_lambda"], conv_b, lru_h.reshape(b, 1, W_B))
    c_params = (wts["bmat"], wts["cmat"], wts["tables"], wts["s5_d"], wts["w_glu"],
                wts["b_glu"], s5_re.reshape(b, 1, S5_STATES), s5_im.reshape(b, 1, S5_STATES),
                layer)
    bb_b, tt_b = tiles["b"]
    bb_c, tt_c = tiles["c"]

    proj_c = _in_projection(h, w_in, layer, col_c, 2 * W_C).reshape(b, t, 2 * W_C)
    if tiles["fuse"]:
        proj_b, out_c, re_new, im_new = _fused_proj_call(
            "proj_b_mixer_c", h, w_in, layer, col_b, 2 * W_B,
            [_flat_mixer_c(proj_c, *c_params, bb=bb_c, tt=tt_c)])
        proj_b = proj_b.reshape(b, t, 2 * W_B)
        proj_a, out_b, conv_b_new, lru_new = _fused_proj_call(
            "proj_a_mixer_b", h, w_in, layer, col_a, 4 * W_A,
            [_flat_mixer_b(proj_b, *b_params, bb=bb_b, tt=tt_b)])
    else:
        out_c, re_new, im_new = _mixer_c(proj_c, *c_params, bb=bb_c, tt=tt_c)
        proj_b = _in_projection(h, w_in, layer, col_b, 2 * W_B).reshape(b, t, 2 * W_B)
        out_b, conv_b_new, lru_new = _mixer_b(proj_b, *b_params, bb=bb_b, tt=tt_b)
        proj_a = _in_projection(h, w_in, layer, col_a, 4 * W_A)
    proj_a = proj_a.reshape(b, t, 4 * W_A)
    out_a, conv_a_new = _mixer_a(proj_a, wts["w_conv_a"], wts["b_conv_a"], conv_a,
                                 *tiles["a"])

    x_new = _out_projection(out_a, out_b, out_c, wts["w_out"], x, gate, layer, *tiles["out"])
    return (x_new, conv_a_new, conv_b_new, lru_new.reshape(b, W_B),
            re_new.reshape(b, S5_G, S5_P), im_new.reshape(b, S5_G, S5_P))


PROMPT_TILES = {"norm": (1, 512), "a": (1, 2048), "b": (1, 1024), "c": (1, 256),
                "out": (1, 1024), "final": (1, 512), "fuse": True}
SAMPLE_TILES = {"norm": (32, 8), "a": (128, 8), "b": (64, 8), "c": (32, 8),
                "out": (128, 8), "final": (32, 8), "fuse": False}


def kernel(x_prompt, x_sample, c_prompt, c_sample, state_conv_a, state_conv_b, state_lru_h,
           state_s5_re, state_s5_im, g_norm, w_ada, b_ada, w_in, w_conv_a, b_conv_a,
           w_conv_b, b_conv_b, w_rg, b_rg, w_ig, b_ig, lru_lambda, s5_lambda_re,
           s5_lambda_im, s5_log_dt, s5_b_re, s5_b_im, s5_c_re, s5_c_im, s5_d, w_glu,
           b_glu, w_out, g_final):
    depth = w_in.shape[0]
    bp = x_prompt.shape[0]
    bs = x_sample.shape[0]

    c_all = jnp.concatenate([c_prompt, c_sample], axis=0)
    pad = (-c_all.shape[0]) % SUBLANES
    c_all = jnp.pad(c_all, ((0, pad), (0, 0)))
    mod = _modulation(c_all, w_ada, b_ada)

    ab_re, ab_im, bb_re, bb_im = _s5_discretize(s5_lambda_re, s5_lambda_im, s5_log_dt,
                                               s5_b_re, s5_b_im)
    tables = _s5_power_tables(ab_re, ab_im)
    bmat, cmat = _s5_block_matrices(bb_re, bb_im, s5_c_re, s5_c_im)
    w_glu_bf16 = w_glu.astype(BF16)

    xp, xs = x_prompt, x_sample
    zeros_p = (jnp.zeros((bp, CONV_A - 1, W_A), F32), jnp.zeros((bp, CONV_B - 1, W_B), F32),
               jnp.zeros((bp, W_B), F32), jnp.zeros((bp, S5_G, S5_P), F32),
               jnp.zeros((bp, S5_G, S5_P), F32))
    outs_p, outs_s = [], []
    for l in range(depth):
        wts = {
            "g_norm": g_norm[l], "w_in": w_in,
            "w_conv_a": w_conv_a[l], "b_conv_a": b_conv_a[l],
            "w_conv_b": w_conv_b[l], "b_conv_b": b_conv_b[l],
            "w_gates": _gate_weights(w_rg[l], w_ig[l]), "b_rg": b_rg[l], "b_ig": b_ig[l],
            "lru_lambda": lru_lambda[l], "bmat": bmat, "cmat": cmat,
            "tables": tables, "s5_d": s5_d[l],
            "w_glu": w_glu_bf16, "b_glu": b_glu[l], "w_out": w_out,
        }
        res_p = _layer(xp, mod[l, :bp], zeros_p, wts, l, PROMPT_TILES)
        xp = res_p[0]
        outs_p.append(res_p[1:])
        st_s = (state_conv_a[l], state_conv_b[l], state_lru_h[l], state_s5_re[l],
                state_s5_im[l])
        res_s = _layer(xs, mod[l, bp:bp + bs], st_s, wts, l, SAMPLE_TILES)
        xs = res_s[0]
        outs_s.append(res_s[1:])

    y_prompt = _final_norm(xp, g_final, *PROMPT_TILES["final"])
    y_sample = _final_norm(xs, g_final, *SAMPLE_TILES["final"])
    stack = lambda outs, k: jnp.stack([o[k] for o in outs])
    return (y_prompt, y_sample,
            *(stack(outs_p, k) for k in range(5)),
            *(stack(outs_s, k) for k in range(5)))
```

```python
import functools

import jax
import jax.numpy as jnp
from jax import lax
from jax.experimental import pallas as pl
from jax.experimental.pallas import tpu as pltpu

F32 = jnp.float32
BF16 = jnp.bfloat16

D_MODEL = 4096
W_A = D_MODEL // 4
W_B = D_MODEL // 2
W_C = D_MODEL // 4
IN_COLS = 4 * W_A + 2 * W_B + 2 * W_C
CONV_A = 3
CONV_B = 4
LRU_HEADS = 16
LRU_HD = W_B // LRU_HEADS
LRU_C = 8.0
S5_GROUP = 16
S5_G = W_C // S5_GROUP
S5_P = 64
S5_STATES = S5_G * S5_P
EPS = 1e-6

SUBLANES = 8
LANES = 128
MXU_DIM = 256
VMEM_LIMIT_CAP = 60000 * 1024
CH_TILE = MXU_DIM
S5_KT = W_C // CH_TILE
S5_Q = (CH_TILE // S5_GROUP) * S5_P // LANES
HALO = SUBLANES
N_TABLES = 16
GATE_CHUNK_ROWS = 128


def _params(sem, vmem_bytes):
    return pltpu.CompilerParams(
        dimension_semantics=sem,
        vmem_limit_bytes=int(min(VMEM_LIMIT_CAP, vmem_bytes)))


def _nbytes(shape, dtype):
    n = 1
    for s in shape:
        n *= s
    return n * jnp.dtype(dtype).itemsize


def _mod_kernel(c_ref, w_ref, b_ref, o_ref):
    c = c_ref[...].astype(BF16)
    w = w_ref[0].astype(BF16)
    o_ref[0] = jnp.dot(c, w, preferred_element_type=F32) + b_ref[0]


def _modulation(c_all, w_ada, b_ada):
    depth, d, n = w_ada.shape
    rows = c_all.shape[0]
    tn = 512
    vmem = 2 * (_nbytes((rows, d), F32) + _nbytes((d, tn), F32)
                + _nbytes((rows, tn), F32)) + _nbytes((d, tn), F32)
    return pl.pallas_call(
        _mod_kernel,
        grid=(depth, n // tn),
        in_specs=[
            pl.BlockSpec((rows, d), lambda l, j: (0, 0)),
            pl.BlockSpec((1, d, tn), lambda l, j: (l, 0, j)),
            pl.BlockSpec((1, 1, tn), lambda l, j: (l, 0, j)),
        ],
        out_specs=pl.BlockSpec((1, rows, tn), lambda l, j: (l, 0, j)),
        out_shape=jax.ShapeDtypeStruct((depth, rows, n), F32),
        compiler_params=_params(("arbitrary", "arbitrary"), vmem + (8 << 20)),
        name="adaln_modulation",
    )(c_all, w_ada, b_ada.reshape(depth, 1, n))


def _norm_mod_kernel(x_ref, scale_ref, shift_ref, g_ref, o_ref):
    x = x_ref[...]
    ms = jnp.mean(x * x, axis=-1, keepdims=True)
    y = (x * lax.rsqrt(ms + EPS)) * g_ref[...]
    h = y * (1.0 + scale_ref[...]) + shift_ref[...]
    o_ref[...] = h.reshape(o_ref.shape).astype(o_ref.dtype)


def _norm_modulate(x, scale, shift, g, bb, tt):
    b, t, d = x.shape
    nt = t // tt
    vmem = 2 * (_nbytes((bb, tt, d), F32) + _nbytes((bb, tt, d), BF16)) \
        + 4 * _nbytes((bb, tt, d), F32)
    return pl.pallas_call(
        _norm_mod_kernel,
        grid=(b // bb, nt),
        in_specs=[
            pl.BlockSpec((bb, tt, d), lambda i, j: (i, j, 0)),
            pl.BlockSpec((bb, 1, d), lambda i, j: (i, 0, 0)),
            pl.BlockSpec((bb, 1, d), lambda i, j: (i, 0, 0)),
            pl.BlockSpec((1, 1, d), lambda i, j: (0, 0, 0)),
        ],
        out_specs=pl.BlockSpec((bb * tt, d), lambda i, j: (i * nt + j, 0)),
        out_shape=jax.ShapeDtypeStruct((b * t, d), BF16),
        compiler_params=_params(("arbitrary", "arbitrary"), vmem),
        name="norm_modulate",
    )(x, scale, shift, g.reshape(1, 1, d))


def _final_norm_kernel(x_ref, g_ref, o_ref):
    x = x_ref[...]
    ms = jnp.mean(x * x, axis=-1, keepdims=True)
    o_ref[...] = (x * lax.rsqrt(ms + EPS)) * g_ref[...]


def _final_norm(x, g, bb, tt):
    b, t, d = x.shape
    vmem = 8 * _nbytes((bb, tt, d), F32)
    return pl.pallas_call(
        _final_norm_kernel,
        grid=(b // bb, t // tt),
        in_specs=[
            pl.BlockSpec((bb, tt, d), lambda i, j: (i, j, 0)),
            pl.BlockSpec((1, 1, d), lambda i, j: (0, 0, 0)),
        ],
        out_specs=pl.BlockSpec((bb, tt, d), lambda i, j: (i, j, 0)),
        out_shape=jax.ShapeDtypeStruct((b, t, d), F32),
        compiler_params=_params(("arbitrary", "arbitrary"), vmem),
        name="final_norm",
    )(x, g.reshape(1, 1, d))


PROJ_TN = 512
PROJ_ALONE_TILE = (1024, 512)


def _cast_weight_tile(w_ref, wb_ref, first_row_tile):
    @pl.when(first_row_tile)
    def _():
        wb_ref[...] = w_ref[0].astype(BF16)


def _in_proj_kernel(a_ref, w_ref, o_ref, wb_ref):
    _cast_weight_tile(w_ref, wb_ref, pl.program_id(1) == 0)
    o_ref[...] = jnp.dot(a_ref[...], wb_ref[...], preferred_element_type=F32)


def _in_proj_operands(h2d, w_in, layer, col0, ncols, tm, tn, idx):
    m, k = h2d.shape
    assert col0 % tn == 0 and ncols % tn == 0 and m % tm == 0
    return dict(
        args=[h2d, w_in],
        in_specs=[
            pl.BlockSpec((tm, k), lambda *g: (idx(*g)[1], 0)),
            pl.BlockSpec((1, k, tn), lambda *g: (layer, 0, col0 // tn + idx(*g)[0])),
        ],
        out_specs=[pl.BlockSpec((tm, tn), lambda *g: (idx(*g)[1], idx(*g)[0]))],
        out_shape=[jax.ShapeDtypeStruct((m, ncols), F32)],
        scratch=[pltpu.VMEM((k, tn), BF16)],
        vmem=2 * (_nbytes((tm, k), BF16) + _nbytes((k, tn), F32) + _nbytes((tm, tn), F32))
        + _nbytes((k, tn), BF16) + _nbytes((tm, tn), F32),
    )


def _in_projection(h2d, w_in, layer, col0, ncols):
    m = h2d.shape[0]
    tm, tn = PROJ_ALONE_TILE
    ops = _in_proj_operands(h2d, w_in, layer, col0, ncols, tm, tn, lambda j, i: (j, i))
    return pl.pallas_call(
        _in_proj_kernel,
        grid=(ncols // tn, m // tm),
        in_specs=ops["in_specs"],
        out_specs=ops["out_specs"][0],
        out_shape=ops["out_shape"][0],
        scratch_shapes=ops["scratch"],
        compiler_params=_params(("arbitrary", "arbitrary"), ops["vmem"] + (4 << 20)),
        name="in_projection",
    )(*ops["args"])


def _causal_taps(scr, v, w_ref, taps, tt):
    scr[:, HALO:HALO + tt, :] = v
    acc = None
    for k in range(taps):
        src = v if k == taps - 1 else scr[:, HALO - (taps - 1) + k:HALO - (taps - 1) + k + tt, :]
        term = w_ref[k:k + 1, :] * src
        acc = term if acc is None else acc + term
    return acc


def _block_scan_real(a, b):
    row = lax.broadcasted_iota(jnp.int32, (1, SUBLANES, a.shape[-1]), 1)
    for d in (1, 2, 4):
        keep = row >= d
        a_sh = pltpu.roll(a, d, 1)
        b_sh = pltpu.roll(b, d, 1)
        b = b + a * jnp.where(keep, b_sh, 0.0)
        a = a * jnp.where(keep, a_sh, 1.0)
    return a, b


def _block_scan_cplx(br, bi, tab_ref, kt, base, sl):
    for lvl, d in enumerate((1, 2, 4)):
        lr = tab_ref[kt, base + 2 * lvl, :, sl]
        li = tab_ref[kt, base + 2 * lvl + 1, :, sl]
        sr = pltpu.roll(br, d, 1)
        si = pltpu.roll(bi, d, 1)
        br, bi = br + (lr * sr - li * si), bi + (lr * si + li * sr)
    return br, bi


def _spread_rows(scr, nblk, bb):
    return jnp.concatenate(
        [jnp.broadcast_to(scr[:, k:k + 1, :], (bb, SUBLANES, LANES)) for k in range(nblk)],
        axis=1)


def _mixer_a_kernel(ab_ref, ac_ref, ax_ref, ag_ref, w_ref, b_ref, st_ref,
                    o_ref, st_out_ref, scr, *, tt):
    keep = CONV_A - 1

    @pl.when(pl.program_id(2) == 0)
    def _():
        scr[:, HALO - keep:HALO, :] = st_ref[...]

    conv_in = ac_ref[...] * ax_ref[...]
    y = b_ref[...] + _causal_taps(scr, conv_in, w_ref, CONV_A, tt)
    out = (ab_ref[...] * y) * jax.nn.silu(ag_ref[...])
    o_ref[...] = out.reshape(o_ref.shape).astype(o_ref.dtype)
    tail = scr[:, HALO + tt - keep:HALO + tt, :]
    st_out_ref[...] = tail
    scr[:, HALO - keep:HALO, :] = tail


def _mixer_a(proj3, w_conv, b_conv, state, bb, tt):
    b, t, _ = proj3.shape
    cw = CH_TILE
    nc = W_A // cw
    nt = t // tt
    blk = (bb, tt, cw)

    def col(off):
        return pl.BlockSpec(blk, lambda i, c, j, off=off: (i, j, off // cw + c))

    vmem = 2 * (4 * _nbytes(blk, F32) + _nbytes(blk, BF16)) \
        + _nbytes((bb, HALO + tt, cw), F32) + 6 * _nbytes(blk, F32)
    return pl.pallas_call(
        functools.partial(_mixer_a_kernel, tt=tt),
        grid=(b // bb, nc, nt),
        in_specs=[
            col(0), col(W_A), col(2 * W_A), col(3 * W_A),
            pl.BlockSpec((CONV_A, cw), lambda i, c, j: (0, c)),
            pl.BlockSpec((1, cw), lambda i, c, j: (0, c)),
            pl.BlockSpec((bb, CONV_A - 1, cw), lambda i, c, j: (i, 0, c)),
        ],
        out_specs=[
            pl.BlockSpec((bb * tt, cw), lambda i, c, j: (i * nt + j, c)),
            pl.BlockSpec((bb, CONV_A - 1, cw), lambda i, c, j: (i, 0, c)),
        ],
        out_shape=[
            jax.ShapeDtypeStruct((b * t, W_A), BF16),
            jax.ShapeDtypeStruct((b, CONV_A - 1, W_A), F32),
        ],
        scratch_shapes=[pltpu.VMEM((bb, HALO + tt, cw), F32)],
        compiler_params=_params(("arbitrary",) * 3, vmem),
        name="mixer_a",
    )(proj3, proj3, proj3, proj3, w_conv, b_conv.reshape(1, W_A), state)


MIXER_B_INS = 10
MIXER_B_OUTS = 3


def _mixer_b_init(cst_ref, hst_ref, scr, carry, first_time_tile):
    @pl.when(first_time_tile)
    def _():
        scr[:, HALO - (CONV_B - 1):HALO, :] = cst_ref[...]
        carry[...] = hst_ref[...]


def _mixer_b_kernel(*refs, bb, tt):
    ins = refs[:MIXER_B_INS]
    outs = refs[MIXER_B_INS:MIXER_B_INS + MIXER_B_OUTS]
    scr, carry, *scan_scr = refs[MIXER_B_INS + MIXER_B_OUTS:]
    _mixer_b_init(ins[8], ins[9], scr, carry, pl.program_id(2) == 0)
    _mixer_b_step(*ins[:8], *outs, scr, carry, scan_scr, bb=bb, tt=tt)


def _mixer_b_step(bx_ref, bg_ref, w_ref, b_ref, wg_ref, brg_ref, big_ref, lam_ref,
                  o_ref, cst_out_ref, hst_out_ref, scr, carry, scan_scr, *, bb, tt,
                  side_matmul=None):
    keep = CONV_B - 1
    cw = CH_TILE
    rows = bb * tt
    nblk = tt // SUBLANES
    nv = nblk // SUBLANES

    scr[:, HALO:HALO + tt, :] = bx_ref[...]
    cst_out_ref[...] = scr[:, HALO + tt - keep:HALO + tt, :]
    softplus_neg_lam = jax.nn.softplus(-lam_ref[...])
    cin_all = carry[...]
    h_parts, last_parts = [], []
    assert LRU_HD == LANES
    n_heads = cw // LRU_HD
    n_chunks = 1 if side_matmul is None else rows // GATE_CHUNK_ROWS
    rc = rows // n_chunks
    for s in range(n_heads):
        sl = slice(s * LANES, (s + 1) * LANES)
        w_head = jnp.concatenate([wg_ref[0, sl, sl], wg_ref[0, sl, cw + s * LANES:cw + (s + 1) * LANES]],
                                 axis=1)
        a_chunks, b_chunks = [], []
        for k in range(n_chunks):
            acc = None
            for j in range(CONV_B):
                lo = HALO - keep + j
                if tt >= rc:
                    b0, t0 = divmod(k * rc, tt)
                    win = scr[b0:b0 + 1, lo + t0:lo + t0 + rc, sl]
                else:
                    nb = rc // tt
                    win = scr[k * nb:(k + 1) * nb, lo:lo + tt, sl]
                term = w_ref[j:j + 1, sl] * win.reshape(rc, LANES)
                acc = term if acc is None else acc + term
            xb = b_ref[:, sl] + acc
            gates = jnp.dot(xb.astype(BF16), w_head, preferred_element_type=F32)
            if side_matmul is not None:
                side_matmul(s * n_chunks + k, n_heads * n_chunks)
            r = jax.nn.sigmoid(gates[:, :LANES] + brg_ref[:, sl])
            ig = jax.nn.sigmoid(gates[:, LANES:] + big_ref[:, sl])
            log_a = (-LRU_C * r) * softplus_neg_lam[:, sl]
            a = jnp.exp(log_a)
            beta = jnp.sqrt(-jnp.tanh(log_a) * (a * a + 1.0))
            bt = (beta * ig) * xb
            a_k, b_k = _block_scan_real(a.reshape(rc // SUBLANES, SUBLANES, LANES),
                                        bt.reshape(rc // SUBLANES, SUBLANES, LANES))
            a_chunks.append(a_k)
            b_chunks.append(b_k)
        a_blk = jnp.concatenate(a_chunks, axis=0)
        b_blk = jnp.concatenate(b_chunks, axis=0)
        cin = cin_all[:, :, sl]
        if nblk == 1:
            h = b_blk + a_blk * cin
            last = h[:, SUBLANES - 1:SUBLANES, :]
        else:
            acum, bcum, cscr = scan_scr
            acum[s] = a_blk.reshape(bb, tt, LANES)
            bcum[s] = b_blk.reshape(bb, tt, LANES)
            a2 = acum[s, :, pl.ds(SUBLANES - 1, nblk, stride=SUBLANES), :]
            b2 = bcum[s, :, pl.ds(SUBLANES - 1, nblk, stride=SUBLANES), :]
            a2, b2 = _block_scan_real(a2.reshape(bb * nv, SUBLANES, LANES),
                                      b2.reshape(bb * nv, SUBLANES, LANES))
            a2 = a2.reshape(bb, nblk, LANES)
            b2 = b2.reshape(bb, nblk, LANES)
            prev = cin
            ends = []
            for j in range(nv):
                seg = slice(j * SUBLANES, (j + 1) * SUBLANES)
                e = b2[:, seg, :] + a2[:, seg, :] * prev
                ends.append(e)
                prev = e[:, SUBLANES - 1:SUBLANES, :]
            last = prev
            cscr[s, :, 0:1, :] = cin
            cscr[s, :, 1:1 + nblk, :] = jnp.concatenate(ends, axis=1)
            spread = _spread_rows(cscr.at[s], nblk, bb)
            h = b_blk + a_blk * spread.reshape(bb * nblk, SUBLANES, LANES)
        h_parts.append(h.reshape(bb, tt, LANES))
        last_parts.append(last)
    h = jnp.concatenate(h_parts, axis=-1)
    last = jnp.concatenate(last_parts, axis=-1)
    carry[...] = last
    hst_out_ref[...] = last
    out = h * jax.nn.silu(bg_ref[...])
    o_ref[...] = out.reshape(o_ref.shape).astype(o_ref.dtype)
    scr[:, HALO - keep:HALO, :] = scr[:, HALO + tt - keep:HALO + tt, :]


def _mixer_b_operands(proj3, w_conv, b_conv, w_gates, b_rg, b_ig, lam, conv_state, h_state,
                      bb, tt, idx):
    b, t, _ = proj3.shape
    cw = CH_TILE
    nt = t // tt
    nblk = tt // SUBLANES
    blk = (bb, tt, cw)

    def spec(shape, fn):
        return pl.BlockSpec(shape, lambda *g: fn(*idx(*g)))

    vec = spec((1, cw), lambda i, c, j: (0, c))
    scratch = [pltpu.VMEM((bb, HALO + tt, cw), F32), pltpu.VMEM((bb, 1, cw), F32)]
    if nblk > 1:
        assert nblk % SUBLANES == 0
        slabs = cw // LANES
        scratch += [pltpu.VMEM((slabs, bb, tt, LANES), F32),
                    pltpu.VMEM((slabs, bb, tt, LANES), F32),
                    pltpu.VMEM((slabs, bb, HALO + nblk, LANES), F32)]
    return dict(
        args=[proj3, proj3, w_conv, b_conv.reshape(1, W_B), w_gates,
              b_rg.reshape(1, W_B), b_ig.reshape(1, W_B), lam.reshape(1, W_B),
              conv_state, h_state],
        in_specs=[
            spec(blk, lambda i, c, j: (i, j, c)),
            spec(blk, lambda i, c, j: (i, j, W_B // cw + c)),
            spec((CONV_B, cw), lambda i, c, j: (0, c)),
            vec,
            spec((1, cw, 2 * cw), lambda i, c, j: (c, 0, 0)),
            vec, vec, vec,
            spec((bb, CONV_B - 1, cw), lambda i, c, j: (i, 0, c)),
            spec((bb, 1, cw), lambda i, c, j: (i, 0, c)),
        ],
        out_specs=[
            spec((bb * tt, cw), lambda i, c, j: (i * nt + j, c)),
            spec((bb, CONV_B - 1, cw), lambda i, c, j: (i, 0, c)),
            spec((bb, 1, cw), lambda i, c, j: (i, 0, c)),
        ],
        out_shape=[
            jax.ShapeDtypeStruct((b * t, W_B), BF16),
            jax.ShapeDtypeStruct((b, CONV_B - 1, W_B), F32),
            jax.ShapeDtypeStruct((b, 1, W_B), F32),
        ],
        scratch=scratch,
        vmem=2 * (2 * _nbytes(blk, F32) + _nbytes(blk, BF16) + _nbytes((cw, 2 * cw), BF16))
        + _nbytes((bb, HALO + tt, cw), F32) + 16 * _nbytes(blk, F32),
        grid=(b // bb, W_B // cw, nt),
    )


def _mixer_b(proj3, *params, bb, tt):
    ops = _mixer_b_operands(proj3, *params, bb, tt, lambda i, c, j: (i, c, j))
    return pl.pallas_call(
        functools.partial(_mixer_b_kernel, bb=bb, tt=tt),
        grid=ops["grid"],
        in_specs=ops["in_specs"],
        out_specs=ops["out_specs"],
        out_shape=ops["out_shape"],
        scratch_shapes=ops["scratch"],
        compiler_params=_params(("arbitrary",) * 3, ops["vmem"]),
        name="mixer_b",
    )(*ops["args"])


def _fused_proj_kernel(*refs, mixers, ni):
    a_ref, w_ref = refs[:2]
    mixers = [dict(mx) for mx in mixers]
    pos = 2
    for mx in mixers:
        mx["ins"] = refs[pos:pos + mx["n_ins"]]
        pos += mx["n_ins"]
    proj_ref = refs[pos]
    pos += 1
    for mx in mixers:
        mx["outs"] = refs[pos:pos + mx["n_outs"]]
        pos += mx["n_outs"]
    wb_ref = refs[pos]
    pos += 1
    for mx in mixers:
        mx["scr"] = refs[pos:pos + mx["n_scr"]]
        pos += mx["n_scr"]
    s = pl.program_id(0)
    _cast_weight_tile(w_ref, wb_ref, lax.rem(s, ni) == 0)
    for mx in mixers:
        mx["init"](mx["ins"], mx["scr"], s)
    k_total = a_ref.shape[1]
    acc = []

    def projection_part(part, n_parts):
        kc = MXU_DIM
        per_part = k_total // (kc * n_parts)
        assert per_part * kc * n_parts == k_total
        for sub in range(part * per_part, (part + 1) * per_part):
            ks = slice(sub * kc, (sub + 1) * kc)
            term = jnp.dot(a_ref[:, ks], wb_ref[ks, :], preferred_element_type=F32)
            acc[:] = [term if not acc else acc[0] + term]
        if part == n_parts - 1:
            proj_ref[...] = acc[0]

    for mx in mixers[:-1]:
        mx["step"](mx["ins"], mx["outs"], mx["scr"], s, None)
    mx = mixers[-1]
    mx["step"](mx["ins"], mx["outs"], mx["scr"], s, projection_part)


def _fused_proj_call(name, h2d, w_in, layer, col0, ncols, mixers):
    steps = None
    for ops, _, _ in mixers:
        n = 1
        for g in ops["grid"]:
            n *= g
        assert steps in (None, n)
        steps = n
    nj = ncols // PROJ_TN
    ni = steps // nj
    tm = h2d.shape[0] // ni
    assert nj * ni == steps and tm * ni == h2d.shape[0] and tm % SUBLANES == 0
    mm = _in_proj_operands(h2d, w_in, layer, col0, ncols, tm, PROJ_TN,
                           lambda s: (s // ni, s % ni))
    specs = [dict(n_ins=len(ops["args"]), n_outs=len(ops["out_shape"]),
                  n_scr=len(ops["scratch"]), init=init, step=step)
             for ops, init, step in mixers]
    cat = lambda key: [x for ops, _, _ in mixers for x in ops[key]]
    return pl.pallas_call(
        functools.partial(_fused_proj_kernel, mixers=specs, ni=ni),
        grid=(steps,),
        in_specs=mm["in_specs"] + cat("in_specs"),
        out_specs=mm["out_specs"] + cat("out_specs"),
        out_shape=mm["out_shape"] + cat("out_shape"),
        scratch_shapes=mm["scratch"] + cat("scratch"),
        compiler_params=_params(("arbitrary",),
                                mm["vmem"] + sum(ops["vmem"] for ops, _, _ in mixers)),
        name=name,
    )(*mm["args"], *cat("args"))


def _flat_index(dims):
    def idx(s):
        out = []
        for k, d in enumerate(dims):
            stride = 1
            for e in dims[k + 1:]:
                stride *= e
            out.append((s // stride) % d if k else s // stride)
        return tuple(out)
    return idx


def _flat_mixer_b(proj3, *params, bb, tt):
    b, t, _ = proj3.shape
    dims = (b // bb, W_B // CH_TILE, t // tt)
    ops = _mixer_b_operands(proj3, *params, bb, tt, _flat_index(dims))
    nt = dims[2]

    def init(ins, scr, s):
        _mixer_b_init(ins[8], ins[9], scr[0], scr[1], lax.rem(s, nt) == 0)

    def step(ins, outs, scr, s, side_matmul):
        _mixer_b_step(*ins[:8], *outs, scr[0], scr[1], scr[2:], bb=bb, tt=tt,
                      side_matmul=side_matmul)

    return ops, init, step


def _s5_disc_kernel(lre_ref, lim_ref, ldt_ref, btre_ref, btim_ref,
                    abre_ref, abim_ref, bbre_ref, bbim_ref):
    lre = lre_ref[0]
    lim = lim_ref[0]
    dt = jnp.exp(ldt_ref[0])
    mag = jnp.exp(lre * dt)
    ab_re = mag * jnp.cos(lim * dt)
    ab_im = mag * jnp.sin(lim * dt)
    nr = ab_re - 1.0
    den = lre * lre + lim * lim
    fr = (nr * lre + ab_im * lim) / den
    fi = (ab_im * lre - nr * lim) / den
    bre = btre_ref[0]
    bim = btim_ref[0]
    bbre_ref[0] = fr * bre - fi * bim
    bbim_ref[0] = fr * bim + fi * bre
    abre_ref[0] = ab_re
    abim_ref[0] = ab_im


def _s5_discretize(lam_re, lam_im, log_dt, b_re, b_im):
    depth = lam_re.shape[0]
    gp = (depth, S5_G, 1, S5_P)
    ldt = jnp.broadcast_to(log_dt[:, :, None, None], gp)
    bt_shape = (depth, S5_G, S5_GROUP, S5_P)
    small = pl.BlockSpec((1, S5_G, 1, S5_P), lambda l: (l, 0, 0, 0))
    big = pl.BlockSpec((1, S5_G, S5_GROUP, S5_P), lambda l: (l, 0, 0, 0))
    ab_re, ab_im, bb_re, bb_im = pl.pallas_call(
        _s5_disc_kernel,
        grid=(depth,),
        in_specs=[small, small, small, big, big],
        out_specs=[small, small, big, big],
        out_shape=[jax.ShapeDtypeStruct(gp, F32), jax.ShapeDtypeStruct(gp, F32),
                   jax.ShapeDtypeStruct(bt_shape, F32), jax.ShapeDtypeStruct(bt_shape, F32)],
        name="s5_discretize",
    )(lam_re.reshape(gp), lam_im.reshape(gp), ldt,
      jnp.swapaxes(b_re, 2, 3), jnp.swapaxes(b_im, 2, 3))
    return (ab_re.reshape(depth, 1, S5_STATES), ab_im.reshape(depth, 1, S5_STATES),
            bb_re, bb_im)


def _s5_pow_kernel(ar_ref, ai_ref, tab_ref):
    def cmul(x, y):
        return x[0] * y[0] - x[1] * y[1], x[0] * y[1] + x[1] * y[0]

    def powers(b1):
        b2 = cmul(b1, b1)
        b3 = cmul(b2, b1)
        b4 = cmul(b2, b2)
        return (b1, b2, b3, b4, cmul(b4, b1), cmul(b4, b2), cmul(b4, b3), cmul(b4, b4))

    shape = (SUBLANES, S5_STATES)
    row = lax.broadcasted_iota(jnp.int32, shape, 0)
    zero = jnp.zeros(shape, F32)
    pw_a = powers((ar_ref[0], ai_ref[0]))
    pw_a8 = powers(pw_a[SUBLANES - 1])
    for base, pw in ((0, pw_a), (N_TABLES // 2, pw_a8)):
        for part in (0, 1):
            tab_ref[0, base + part] = jnp.where(row >= 1, pw[0][part], zero)
            tab_ref[0, base + 2 + part] = jnp.where(row >= 2, pw[1][part], zero)
            tab_ref[0, base + 4 + part] = jnp.where(row >= 4, pw[3][part], zero)
            p = zero
            for r, v in enumerate(pw):
                p = jnp.where(row == r, v[part], p)
            tab_ref[0, base + 6 + part] = p


def _s5_power_tables(ab_re, ab_im):
    depth = ab_re.shape[0]
    vec = pl.BlockSpec((1, 1, S5_STATES), lambda l: (l, 0, 0))
    tables = pl.pallas_call(
        _s5_pow_kernel,
        grid=(depth,),
        in_specs=[vec, vec],
        out_specs=pl.BlockSpec((1, N_TABLES, SUBLANES, S5_STATES), lambda l: (l, 0, 0, 0)),
        out_shape=jax.ShapeDtypeStruct((depth, N_TABLES, SUBLANES, S5_STATES), F32),
        name="s5_power_tables",
    )(ab_re, ab_im)
    kslab = S5_STATES // S5_KT
    tables = tables.reshape(depth, N_TABLES, SUBLANES, S5_KT, kslab)
    return jnp.transpose(tables, (0, 3, 1, 2, 4)).reshape(
        depth * S5_KT, N_TABLES, SUBLANES, kslab)


def _s5_block_matrices(bb_re, bb_im, c_re, c_im):
    depth = bb_re.shape[0]
    gpt = CH_TILE // S5_GROUP
    chan = lax.broadcasted_iota(jnp.int32, (S5_Q, CH_TILE, CH_TILE), 1)
    state = lax.broadcasted_iota(jnp.int32, (S5_Q, CH_TILE, CH_TILE), 2)
    slab = lax.broadcasted_iota(jnp.int32, (S5_Q, CH_TILE, CH_TILE), 0)
    gps = LANES // S5_P
    keep = (chan // S5_GROUP) == gps * slab + (state % LANES) // S5_P

    def dense_b(x):
        return x.reshape(depth * S5_KT, gpt * S5_GROUP, S5_P)

    db = jnp.concatenate([dense_b(bb_re)] * gps + [dense_b(bb_im)] * gps, axis=-1)
    bm = jnp.where(keep[None], db[:, None], 0.0).astype(BF16)

    def dense_c(x):
        x = x.reshape(depth * S5_KT, gpt, S5_GROUP, S5_P)
        return jnp.transpose(x, (0, 3, 1, 2)).reshape(depth * S5_KT, S5_P, gpt * S5_GROUP)

    dc = jnp.concatenate([dense_c(c_re)] * gps + [dense_c(-c_im)] * gps, axis=1)
    keep_c = jnp.swapaxes(keep, 1, 2)
    cm = jnp.where(keep_c[None], dc[:, None], 0.0).astype(BF16)
    return bm, cm


MIXER_C_INS = 10
MIXER_C_OUTS = 3


def _mixer_c_init(sre_ref, sim_ref, car_r, car_i, kt, first_time_tile):
    @pl.when(first_time_tile)
    def _():
        car_r[kt] = sre_ref[...]
        car_i[kt] = sim_ref[...]


def _mixer_c_kernel(*refs, bb, tt):
    ins = refs[:MIXER_C_INS]
    outs = refs[MIXER_C_INS:MIXER_C_INS + MIXER_C_OUTS]
    ystore, car_r, car_i, *scan_scr = refs[MIXER_C_INS + MIXER_C_OUTS:]
    kt = pl.program_id(2)
    _mixer_c_init(ins[8], ins[9], car_r, car_i, kt, pl.program_id(1) == 0)
    _mixer_c_step(*ins[:8], *outs, ystore, car_r, car_i, scan_scr, kt, bb=bb, tt=tt)


def _mixer_c_step(cu_ref, cg_ref, bm_ref, cm_ref, tab_ref, d_ref, wglu_ref, bglu_ref,
                  o_ref, sre_out_ref, sim_out_ref,
                  ystore, car_r, car_i, scan_scr, kt, *, bb, tt, side_matmul=None):
    cw = CH_TILE
    rows = bb * tt
    nblk = tt // SUBLANES
    nv = nblk // SUBLANES
    n = bb * nblk
    base8 = N_TABLES // 2

    u = cu_ref[...].reshape(rows, cw)
    ub = u.astype(BF16)
    y = d_ref[...] * u
    cr_all = car_r[kt]
    ci_all = car_i[kt]
    new_r, new_i = [], []
    for q in range(S5_Q):
        sl = slice(LANES * q, LANES * (q + 1))
        bu = jnp.dot(ub, bm_ref[kt, q], preferred_element_type=F32)
        if side_matmul is not None:
            side_matmul(q, S5_Q)
        br, bi = _block_scan_cplx(bu[:, :LANES].reshape(n, SUBLANES, LANES),
                                  bu[:, LANES:].reshape(n, SUBLANES, LANES), tab_ref, kt, 0, sl)
        p_r = tab_ref[kt, 6, :, sl]
        p_i = tab_ref[kt, 7, :, sl]
        cr = cr_all[:, :, sl]
        ci = ci_all[:, :, sl]
        if nblk == 1:
            kr, ki = cr, ci
        else:
            hsr, hsi, csr, csi = scan_scr
            hsr[...] = br.reshape(bb, tt, LANES)
            hsi[...] = bi.reshape(bb, tt, LANES)
            xr = hsr[:, pl.ds(SUBLANES - 1, nblk, stride=SUBLANES), :]
            xi = hsi[:, pl.ds(SUBLANES - 1, nblk, stride=SUBLANES), :]
            xr, xi = _block_scan_cplx(xr.reshape(bb * nv, SUBLANES, LANES),
                                      xi.reshape(bb * nv, SUBLANES, LANES), tab_ref, kt, base8,
                                      sl)
            xr = xr.reshape(bb, nblk, LANES)
            xi = xi.reshape(bb, nblk, LANES)
            q_r = tab_ref[kt, base8 + 6, :, sl]
            q_i = tab_ref[kt, base8 + 7, :, sl]
            prev_r, prev_i = cr, ci
            ends_r, ends_i = [], []
            for j in range(nv):
                seg = slice(j * SUBLANES, (j + 1) * SUBLANES)
                er = xr[:, seg, :] + (q_r * prev_r - q_i * prev_i)
                ei = xi[:, seg, :] + (q_r * prev_i + q_i * prev_r)
                ends_r.append(er)
                ends_i.append(ei)
                prev_r = er[:, SUBLANES - 1:SUBLANES, :]
                prev_i = ei[:, SUBLANES - 1:SUBLANES, :]
            csr[:, 0:1, :] = cr
            csi[:, 0:1, :] = ci
            csr[:, 1:1 + nblk, :] = jnp.concatenate(ends_r, axis=1)
            csi[:, 1:1 + nblk, :] = jnp.concatenate(ends_i, axis=1)
            kr = _spread_rows(csr, nblk, bb).reshape(n, SUBLANES, LANES)
            ki = _spread_rows(csi, nblk, bb).reshape(n, SUBLANES, LANES)
        hr = br + (p_r * kr - p_i * ki)
        hi = bi + (p_r * ki + p_i * kr)
        hr = hr.reshape(bb, tt, LANES)
        hi = hi.reshape(bb, tt, LANES)
        new_r.append(hr[:, tt - 1:tt, :])
        new_i.append(hi[:, tt - 1:tt, :])
        hcat = jnp.concatenate([hr.reshape(rows, LANES), hi.reshape(rows, LANES)], axis=-1)
        y = y + jnp.dot(hcat.astype(BF16), cm_ref[kt, q], preferred_element_type=F32)

    ncr_all = jnp.concatenate(new_r, axis=-1)
    nci_all = jnp.concatenate(new_i, axis=-1)
    car_r[kt] = ncr_all
    car_i[kt] = nci_all
    sre_out_ref[:, kt] = ncr_all
    sim_out_ref[:, kt] = nci_all
    ystore[kt] = y

    @pl.when(kt == S5_KT - 1)
    def _():
        y_all = jnp.concatenate([ystore[k] for k in range(S5_KT)], axis=-1)
        yg = jax.nn.gelu(y_all)
        z = jnp.dot(yg.astype(BF16), wglu_ref[0].astype(BF16),
                    preferred_element_type=F32) + bglu_ref[...]
        yy = yg * jax.nn.sigmoid(z)
        out = yy * jax.nn.silu(cg_ref[...].reshape(rows, W_C))
        o_ref[...] = out.astype(o_ref.dtype)


def _mixer_c_operands(proj3, bmat, cmat, tables, d_skip, w_glu, b_glu, s_re, s_im, layer,
                      bb, tt, idx):
    b, t, _ = proj3.shape
    u_col = 0
    cw = CH_TILE
    nt = t // tt
    nblk = tt // SUBLANES
    kslab = S5_Q * LANES

    def spec(shape, fn):
        return pl.BlockSpec(shape, lambda *g: fn(*idx(*g)))

    mat = spec((S5_KT, S5_Q, cw, cw), lambda i, j, k: (layer, 0, 0, 0))
    st = spec((bb, 1, kslab), lambda i, j, k: (i, 0, k))
    st_out = spec((bb, S5_KT, 1, kslab), lambda i, j, k: (i, 0, 0, 0))
    rows = bb * tt
    scratch = [pltpu.VMEM((S5_KT, rows, cw), F32),
               pltpu.VMEM((S5_KT, bb, 1, kslab), F32),
               pltpu.VMEM((S5_KT, bb, 1, kslab), F32)]
    if nblk > 1:
        assert nblk % SUBLANES == 0 and nblk // SUBLANES <= SUBLANES
        scratch += [pltpu.VMEM((bb, tt, LANES), F32), pltpu.VMEM((bb, tt, LANES), F32),
                    pltpu.VMEM((bb, HALO + nblk, LANES), F32),
                    pltpu.VMEM((bb, HALO + nblk, LANES), F32)]
    vmem = 2 * (_nbytes((bb, tt, cw), F32) + _nbytes((bb, tt, W_C), F32)
                + 2 * _nbytes((S5_KT, S5_Q, cw, cw), BF16)
                + _nbytes((S5_KT, N_TABLES, SUBLANES, kslab), F32)
                + _nbytes((W_C, W_C), F32) + _nbytes((bb, tt, W_C), BF16)
                + 4 * _nbytes((bb, SUBLANES, kslab), F32)) \
        + _nbytes((S5_KT, rows, cw), F32) + 2 * _nbytes((bb, tt, LANES), F32) \
        + 2 * _nbytes((S5_KT, bb, SUBLANES, kslab), F32) + 10 * _nbytes((rows, W_C), F32)
    return dict(
        args=[proj3, proj3, bmat, cmat, tables, d_skip.reshape(1, W_C), w_glu,
              b_glu.reshape(1, W_C), s_re, s_im],
        in_specs=[
            spec((bb, tt, cw), lambda i, j, k: (i, j, u_col // cw + k)),
            spec((bb, tt, W_C), lambda i, j, k: (i, j, u_col // W_C + 1)),
            mat, mat,
            spec((S5_KT, N_TABLES, SUBLANES, kslab), lambda i, j, k: (layer, 0, 0, 0)),
            spec((1, cw), lambda i, j, k: (0, k)),
            spec((1, W_C, W_C), lambda i, j, k: (layer, 0, 0)),
            spec((1, W_C), lambda i, j, k: (0, 0)),
            st, st,
        ],
        out_specs=[
            spec((rows, W_C), lambda i, j, k: (i * nt + j, 0)),
            st_out, st_out,
        ],
        out_shape=[
            jax.ShapeDtypeStruct((b * t, W_C), BF16),
            jax.ShapeDtypeStruct((b, S5_KT, 1, kslab), F32),
            jax.ShapeDtypeStruct((b, S5_KT, 1, kslab), F32),
        ],
        scratch=scratch,
        vmem=vmem,
        grid=(b // bb, nt, S5_KT),
    )


def _mixer_c(proj3, *params, bb, tt):
    ops = _mixer_c_operands(proj3, *params, bb, tt, lambda i, j, k: (i, j, k))
    return pl.pallas_call(
        functools.partial(_mixer_c_kernel, bb=bb, tt=tt),
        grid=ops["grid"],
        in_specs=ops["in_specs"],
        out_specs=ops["out_specs"],
        out_shape=ops["out_shape"],
        scratch_shapes=ops["scratch"],
        compiler_params=_params(("arbitrary",) * 3, ops["vmem"]),
        name="mixer_c",
    )(*ops["args"])


def _flat_mixer_c(proj3, *params, bb, tt):
    b, t, _ = proj3.shape
    dims = (b // bb, t // tt, S5_KT)
    ops = _mixer_c_operands(proj3, *params, bb, tt, _flat_index(dims))
    nt = dims[1]

    def init(ins, scr, s):
        _mixer_c_init(ins[8], ins[9], scr[1], scr[2], lax.rem(s, S5_KT),
                      lax.rem(s // S5_KT, nt) == 0)

    def step(ins, outs, scr, s, side_matmul):
        _mixer_c_step(*ins[:8], *outs, scr[0], scr[1], scr[2], scr[3:],
                      lax.rem(s, S5_KT), bb=bb, tt=tt, side_matmul=side_matmul)

    return ops, init, step


OUT_PROJ_TN = 512


def _out_proj_kernel(ma_ref, mb_ref, mc_ref, w_ref, x_ref, gate_ref, o_ref, *, bb, tt):
    w = w_ref[0].astype(BF16)
    acc = jnp.dot(ma_ref[...], w[0:W_A, :], preferred_element_type=F32)
    acc = acc + jnp.dot(mb_ref[...], w[W_A:W_A + W_B, :], preferred_element_type=F32)
    acc = acc + jnp.dot(mc_ref[...], w[W_A + W_B:, :], preferred_element_type=F32)
    o_ref[...] = x_ref[...] + gate_ref[...] * acc.reshape(bb, tt, acc.shape[-1])


def _out_projection(out_a, out_b, out_c, w_out, x, gate, layer, bb, tt):
    b, t, d = x.shape
    tm = bb * tt
    tn = OUT_PROJ_TN
    nt = t // tt
    m = b * t

    def rows(width):
        return pl.BlockSpec((tm, width), lambda i, j: (i, 0))

    vmem = 2 * (_nbytes((tm, d), BF16) + _nbytes((d, tn), F32)
                + 2 * _nbytes((tm, tn), F32)) + _nbytes((d, tn), BF16) \
        + 3 * _nbytes((tm, tn), F32)
    return pl.pallas_call(
        functools.partial(_out_proj_kernel, bb=bb, tt=tt),
        grid=(m // tm, d // tn),
        in_specs=[
            rows(W_A), rows(W_B), rows(W_C),
            pl.BlockSpec((1, d, tn), lambda i, j: (layer, 0, j)),
            pl.BlockSpec((bb, tt, tn), lambda i, j: (i // nt, i % nt, j)),
            pl.BlockSpec((bb, 1, tn), lambda i, j: (i // nt, 0, j)),
        ],
        out_specs=pl.BlockSpec((bb, tt, tn), lambda i, j: (i // nt, i % nt, j)),
        out_shape=jax.ShapeDtypeStruct((b, t, d), F32),
        compiler_params=_params(("arbitrary", "arbitrary"), vmem + (4 << 20)),
        name="out_projection",
    )(out_a, out_b, out_c, w_out, x, gate)


def _gate_weights(w_rg, w_ig):
    hpt = CH_TILE // LRU_HD
    pairs = LRU_HEADS // hpt
    w = jnp.stack([w_rg, w_ig], axis=2)
    w = w.reshape(pairs, hpt, LRU_HD, 2, LRU_HD)
    eye = jnp.eye(hpt, dtype=bool)
    out = jnp.where(eye[None, :, None, None, :, None],
                    w[:, :, :, :, None, :], 0.0)
    return out.reshape(pairs, CH_TILE, 2 * CH_TILE).astype(BF16)


def _layer(x, mod, states, wts, layer, tiles):
    b, t, d = x.shape
    conv_a, conv_b, lru_h, s5_re, s5_im = states
    shift, scale, gate = (mod[:, None, k * d:(k + 1) * d] for k in range(3))

    h = _norm_modulate(x, scale, shift, wts["g_norm"], *tiles["norm"])
    w_in = wts["w_in"]
    col_a, col_b, col_c = 0, 4 * W_A, 4 * W_A + 2 * W_B
    b_params = (wts["w_conv_b"], wts["b_conv_b"], wts["w_gates"], wts["b_rg"], wts["b_ig"],
                wts["lru_lambda"], conv_b, lru_h.reshape(b, 1, W_B))
    c_params = (wts["bmat"], wts["cmat"], wts["tables"], wts["s5_d"], wts["w_glu"],
                wts["b_glu"], s5_re.reshape(b, 1, S5_STATES), s5_im.reshape(b, 1, S5_STATES),
                layer)
    bb_b, tt_b = tiles["b"]
    bb_c, tt_c = tiles["c"]

    proj_c = _in_projection(h, w_in, layer, col_c, 2 * W_C).reshape(b, t, 2 * W_C)
    if tiles["fuse"]:
        proj_b, out_c, re_new, im_new = _fused_proj_call(
            "proj_b_mixer_c", h, w_in, layer, col_b, 2 * W_B,
            [_flat_mixer_c(proj_c, *c_params, bb=bb_c, tt=tt_c)])
        proj_b = proj_b.reshape(b, t, 2 * W_B)
        proj_a, out_b, conv_b_new, lru_new = _fused_proj_call(
            "proj_a_mixer_b", h, w_in, layer, col_a, 4 * W_A,
            [_flat_mixer_b(proj_b, *b_params, bb=bb_b, tt=tt_b)])
    else:
        out_c, re_new, im_new = _mixer_c(proj_c, *c_params, bb=bb_c, tt=tt_c)
        proj_b = _in_projection(h, w_in, layer, col_b, 2 * W_B).reshape(b, t, 2 * W_B)
        out_b, conv_b_new, lru_new = _mixer_b(proj_b, *b_params, bb=bb_b, tt=tt_b)
        proj_a = _in_projection(h, w_in, layer, col_a, 4 * W_A)
    proj_a = proj_a.reshape(b, t, 4 * W_A)
    out_a, conv_a_new = _mixer_a(proj_a, wts["w_conv_a"], wts["b_conv_a"], conv_a,
                                 *tiles["a"])

    x_new = _out_projection(out_a, out_b, out_c, wts["w_out"], x, gate, layer, *tiles["out"])
    return (x_new, conv_a_new, conv_b_new, lru_new.reshape(b, W_B),
            re_new.reshape(b, S5_G, S5_P), im_new.reshape(b, S5_G, S5_P))


PROMPT_TILES = {"norm": (1, 512), "a": (1, 2048), "b": (1, 1024), "c": (1, 256),
                "out": (1, 1024), "final": (1, 512), "fuse": True}
SAMPLE_TILES = {"norm": (32, 8), "a": (128, 8), "b": (64, 8), "c": (32, 8),
                "out": (128, 8), "final": (32, 8), "fuse": False}


def kernel(x_prompt, x_sample, c_prompt, c_sample, state_conv_a, state_conv_b, state_lru_h,
           state_s5_re, state_s5_im, g_norm, w_ada, b_ada, w_in, w_conv_a, b_conv_a,
           w_conv_b, b_conv_b, w_rg, b_rg, w_ig, b_ig, lru_lambda, s5_lambda_re,
           s5_lambda_im, s5_log_dt, s5_b_re, s5_b_im, s5_c_re, s5_c_im, s5_d, w_glu,
           b_glu, w_out, g_final):
    depth = w_in.shape[0]
    bp = x_prompt.shape[0]
    bs = x_sample.shape[0]

    c_all = jnp.concatenate([c_prompt, c_sample], axis=0)
    pad = (-c_all.shape[0]) % SUBLANES
    c_all = jnp.pad(c_all, ((0, pad), (0, 0)))
    mod = _modulation(c_all, w_ada, b_ada)

    ab_re, ab_im, bb_re, bb_im = _s5_discretize(s5_lambda_re, s5_lambda_im, s5_log_dt,
                                               s5_b_re, s5_b_im)
    tables = _s5_power_tables(ab_re, ab_im)
    bmat, cmat = _s5_block_matrices(bb_re, bb_im, s5_c_re, s5_c_im)
    w_glu_bf16 = w_glu.astype(BF16)

    xp, xs = x_prompt, x_sample
    zeros_p = (jnp.zeros((bp, CONV_A - 1, W_A), F32), jnp.zeros((bp, CONV_B - 1, W_B), F32),
               jnp.zeros((bp, W_B), F32), jnp.zeros((bp, S5_G, S5_P), F32),
               jnp.zeros((bp, S5_G, S5_P), F32))
    outs_p, outs_s = [], []
    for l in range(depth):
        wts = {
            "g_norm": g_norm[l], "w_in": w_in,
            "w_conv_a": w_conv_a[l], "b_conv_a": b_conv_a[l],
            "w_conv_b": w_conv_b[l], "b_conv_b": b_conv_b[l],
            "w_gates": _gate_weights(w_rg[l], w_ig[l]), "b_rg": b_rg[l], "b_ig": b_ig[l],
            "lru_lambda": lru_lambda[l], "bmat": bmat, "cmat": cmat,
            "tables": tables, "s5_d": s5_d[l],
            "w_glu": w_glu_bf16, "b_glu": b_glu[l], "w_out": w_out,
        }
        res_p = _layer(xp, mod[l, :bp], zeros_p, wts, l, PROMPT_TILES)
        xp = res_p[0]
        outs_p.append(res_p[1:])
        st_s = (state_conv_a[l], state_conv_b[l], state_lru_h[l], state_s5_re[l],
                state_s5_im[l])
        res_s = _layer(xs, mod[l, bp:bp + bs], st_s, wts, l, SAMPLE_TILES)
        xs = res_s[0]
        outs_s.append(res_s[1:])

    y_prompt = _final_norm(xp, g_final, *PROMPT_TILES["final"])
    y_sample = _final_norm(xs, g_final, *SAMPLE_TILES["final"])
    stack = lambda outs, k: jnp.stack([o[k] for o in outs])
    return (y_prompt, y_sample,
            *(stack(outs_p, k) for k in range(5)),
            *(stack(outs_s, k) for k in range(5)))
```

```python
import functools

import jax
import jax.numpy as jnp
from jax import lax
from jax.experimental import pallas as pl
from jax.experimental.pallas import tpu as pltpu

F32 = jnp.float32
BF16 = jnp.bfloat16

D_MODEL = 4096
W_A = D_MODEL // 4
W_B = D_MODEL // 2
W_C = D_MODEL // 4
IN_COLS = 4 * W_A + 2 * W_B + 2 * W_C
CONV_A = 3
CONV_B = 4
LRU_HEADS = 16
LRU_HD = W_B // LRU_HEADS
LRU_C = 8.0
S5_GROUP = 16
S5_G = W_C // S5_GROUP
S5_P = 64
S5_STATES = S5_G * S5_P
EPS = 1e-6

SUBLANES = 8
LANES = 128
MXU_DIM = 256
VMEM_LIMIT_CAP = 60000 * 1024
CH_TILE = MXU_DIM
S5_KT = W_C // CH_TILE
S5_Q = (CH_TILE // S5_GROUP) * S5_P // LANES
HALO = SUBLANES
N_TABLES = 16
GATE_CHUNK_ROWS = 128


def _params(sem, vmem_bytes):
    return pltpu.CompilerParams(
        dimension_semantics=sem,
        vmem_limit_bytes=int(min(VMEM_LIMIT_CAP, vmem_bytes)))


def _nbytes(shape, dtype):
    n = 1
    for s in shape:
        n *= s
    return n * jnp.dtype(dtype).itemsize


def _mod_kernel(c_ref, w_ref, b_ref, o_ref):
    c = c_ref[...].astype(BF16)
    w = w_ref[0].astype(BF16)
    o_ref[0] = jnp.dot(c, w, preferred_element_type=F32) + b_ref[0]


def _modulation(c_all, w_ada, b_ada):
    depth, d, n = w_ada.shape
    rows = c_all.shape[0]
    tn = 512
    vmem = 2 * (_nbytes((rows, d), F32) + _nbytes((d, tn), F32)
                + _nbytes((rows, tn), F32)) + _nbytes((d, tn), F32)
    return pl.pallas_call(
        _mod_kernel,
        grid=(depth, n // tn),
        in_specs=[
            pl.BlockSpec((rows, d), lambda l, j: (0, 0)),
            pl.BlockSpec((1, d, tn), lambda l, j: (l, 0, j)),
            pl.BlockSpec((1, 1, tn), lambda l, j: (l, 0, j)),
        ],
        out_specs=pl.BlockSpec((1, rows, tn), lambda l, j: (l, 0, j)),
        out_shape=jax.ShapeDtypeStruct((depth, rows, n), F32),
        compiler_params=_params(("arbitrary", "arbitrary"), vmem + (8 << 20)),
        name="adaln_modulation",
    )(c_all, w_ada, b_ada.reshape(depth, 1, n))


def _norm_mod_kernel(x_ref, scale_ref, shift_ref, g_ref, o_ref):
    x = x_ref[...]
    ms = jnp.mean(x * x, axis=-1, keepdims=True)
    y = (x * lax.rsqrt(ms + EPS)) * g_ref[...]
    h = y * (1.0 + scale_ref[...]) + shift_ref[...]
    o_ref[...] = h.reshape(o_ref.shape).astype(o_ref.dtype)


def _norm_modulate(x, scale, shift, g, bb, tt):
    b, t, d = x.shape
    nt = t // tt
    vmem = 2 * (_nbytes((bb, tt, d), F32) + _nbytes((bb, tt, d), BF16)) \
        + 4 * _nbytes((bb, tt, d), F32)
    return pl.pallas_call(
        _norm_mod_kernel,
        grid=(b // bb, nt),
        in_specs=[
            pl.BlockSpec((bb, tt, d), lambda i, j: (i, j, 0)),
            pl.BlockSpec((bb, 1, d), lambda i, j: (i, 0, 0)),
            pl.BlockSpec((bb, 1, d), lambda i, j: (i, 0, 0)),
            pl.BlockSpec((1, 1, d), lambda i, j: (0, 0, 0)),
        ],
        out_specs=pl.BlockSpec((bb * tt, d), lambda i, j: (i * nt + j, 0)),
        out_shape=jax.ShapeDtypeStruct((b * t, d), BF16),
        compiler_params=_params(("arbitrary", "arbitrary"), vmem),
        name="norm_modulate",
    )(x, scale, shift, g.reshape(1, 1, d))


def _final_norm_kernel(x_ref, g_ref, o_ref):
    x = x_ref[...]
    ms = jnp.mean(x * x, axis=-1, keepdims=True)
    o_ref[...] = (x * lax.rsqrt(ms + EPS)) * g_ref[...]


def _final_norm(x, g, bb, tt):
    b, t, d = x.shape
    vmem = 8 * _nbytes((bb, tt, d), F32)
    return pl.pallas_call(
        _final_norm_kernel,
        grid=(b // bb, t // tt),
        in_specs=[
            pl.BlockSpec((bb, tt, d), lambda i, j: (i, j, 0)),
            pl.BlockSpec((1, 1, d), lambda i, j: (0, 0, 0)),
        ],
        out_specs=pl.BlockSpec((bb, tt, d), lambda i, j: (i, j, 0)),
        out_shape=jax.ShapeDtypeStruct((b, t, d), F32),
        compiler_params=_params(("arbitrary", "arbitrary"), vmem),
        name="final_norm",
    )(x, g.reshape(1, 1, d))


PROJ_TN = 512
PROJ_ALONE_TILE = (1024, 512)


def _cast_weight_tile(w_ref, wb_ref, first_row_tile):
    @pl.when(first_row_tile)
    def _():
        wb_ref[...] = w_ref[0].astype(BF16)


def _in_proj_kernel(a_ref, w_ref, o_ref, wb_ref):
    _cast_weight_tile(w_ref, wb_ref, pl.program_id(1) == 0)
    o_ref[...] = jnp.dot(a_ref[...], wb_ref[...], preferred_element_type=F32)


def _in_proj_operands(h2d, w_in, layer, col0, ncols, tm, tn, idx):
    m, k = h2d.shape
    assert col0 % tn == 0 and ncols % tn == 0 and m % tm == 0
    return dict(
        args=[h2d, w_in],
        in_specs=[
            pl.BlockSpec((tm, k), lambda *g: (idx(*g)[1], 0)),
            pl.BlockSpec((1, k, tn), lambda *g: (layer, 0, col0 // tn + idx(*g)[0])),
        ],
        out_specs=[pl.BlockSpec((tm, tn), lambda *g: (idx(*g)[1], idx(*g)[0]))],
        out_shape=[jax.ShapeDtypeStruct((m, ncols), F32)],
        scratch=[pltpu.VMEM((k, tn), BF16)],
        vmem=2 * (_nbytes((tm, k), BF16) + _nbytes((k, tn), F32) + _nbytes((tm, tn), F32))
        + _nbytes((k, tn), BF16) + _nbytes((tm, tn), F32),
    )


def _in_projection(h2d, w_in, layer, col0, ncols):
    m = h2d.shape[0]
    tm, tn = PROJ_ALONE_TILE
    ops = _in_proj_operands(h2d, w_in, layer, col0, ncols, tm, tn, lambda j, i: (j, i))
    return pl.pallas_call(
        _in_proj_kernel,
        grid=(ncols // tn, m // tm),
        in_specs=ops["in_specs"],
        out_specs=ops["out_specs"][0],
        out_shape=ops["out_shape"][0],
        scratch_shapes=ops["scratch"],
        compiler_params=_params(("arbitrary", "arbitrary"), ops["vmem"] + (4 << 20)),
        name="in_projection",
    )(*ops["args"])


def _causal_taps(scr, v, w_ref, taps, tt):
    scr[:, HALO:HALO + tt, :] = v
    acc = None
    for k in range(taps):
        src = v if k == taps - 1 else scr[:, HALO - (taps - 1) + k:HALO - (taps - 1) + k + tt, :]
        term = w_ref[k:k + 1, :] * src
        acc = term if acc is None else acc + term
    return acc


def _block_scan_real(a, b):
    row = lax.broadcasted_iota(jnp.int32, (1, SUBLANES, a.shape[-1]), 1)
    for d in (1, 2, 4):
        keep = row >= d
        a_sh = pltpu.roll(a, d, 1)
        b_sh = pltpu.roll(b, d, 1)
        b = b + a * jnp.where(keep, b_sh, 0.0)
        a = a * jnp.where(keep, a_sh, 1.0)
    return a, b


def _block_scan_cplx(br, bi, tab_ref, kt, base, sl):
    for lvl, d in enumerate((1, 2, 4)):
        lr = tab_ref[kt, base + 2 * lvl, :, sl]
        li = tab_ref[kt, base + 2 * lvl + 1, :, sl]
        sr = pltpu.roll(br, d, 1)
        si = pltpu.roll(bi, d, 1)
        br, bi = br + (lr * sr - li * si), bi + (lr * si + li * sr)
    return br, bi


def _spread_rows(scr, nblk, bb):
    return jnp.concatenate(
        [jnp.broadcast_to(scr[:, k:k + 1, :], (bb, SUBLANES, LANES)) for k in range(nblk)],
        axis=1)


def _mixer_a_kernel(ab_ref, ac_ref, ax_ref, ag_ref, w_ref, b_ref, st_ref,
                    o_ref, st_out_ref, scr, *, tt):
    keep = CONV_A - 1

    @pl.when(pl.program_id(2) == 0)
    def _():
        scr[:, HALO - keep:HALO, :] = st_ref[...]

    conv_in = ac_ref[...] * ax_ref[...]
    y = b_ref[...] + _causal_taps(scr, conv_in, w_ref, CONV_A, tt)
    out = (ab_ref[...] * y) * jax.nn.silu(ag_ref[...])
    o_ref[...] = out.reshape(o_ref.shape).astype(o_ref.dtype)
    tail = scr[:, HALO + tt - keep:HALO + tt, :]
    st_out_ref[...] = tail
    scr[:, HALO - keep:HALO, :] = tail


def _mixer_a(proj3, w_conv, b_conv, state, bb, tt):
    b, t, _ = proj3.shape
    cw = CH_TILE
    nc = W_A // cw
    nt = t // tt
    blk = (bb, tt, cw)

    def col(off):
        return pl.BlockSpec(blk, lambda i, c, j, off=off: (i, j, off // cw + c))

    vmem = 2 * (4 * _nbytes(blk, F32) + _nbytes(blk, BF16)) \
        + _nbytes((bb, HALO + tt, cw), F32) + 6 * _nbytes(blk, F32)
    return pl.pallas_call(
        functools.partial(_mixer_a_kernel, tt=tt),
        grid=(b // bb, nc, nt),
        in_specs=[
            col(0), col(W_A), col(2 * W_A), col(3 * W_A),
            pl.BlockSpec((CONV_A, cw), lambda i, c, j: (0, c)),
            pl.BlockSpec((1, cw), lambda i, c, j: (0, c)),
            pl.BlockSpec((bb, CONV_A - 1, cw), lambda i, c, j: (i, 0, c)),
        ],
        out_specs=[
            pl.BlockSpec((bb * tt, cw), lambda i, c, j: (i * nt + j, c)),
            pl.BlockSpec((bb, CONV_A - 1, cw), lambda i, c, j: (i, 0, c)),
        ],
        out_shape=[
            jax.ShapeDtypeStruct((b * t, W_A), BF16),
            jax.ShapeDtypeStruct((b, CONV_A - 1, W_A), F32),
        ],
        scratch_shapes=[pltpu.VMEM((bb, HALO + tt, cw), F32)],
        compiler_params=_params(("arbitrary",) * 3, vmem),
        name="mixer_a",
    )(proj3, proj3, proj3, proj3, w_conv, b_conv.reshape(1, W_A), state)


MIXER_B_INS = 10
MIXER_B_OUTS = 3


def _mixer_b_init(cst_ref, hst_ref, scr, carry, first_time_tile):
    @pl.when(first_time_tile)
    def _():
        scr[:, HALO - (CONV_B - 1):HALO, :] = cst_ref[...]
        carry[...] = hst_ref[...]


def _mixer_b_kernel(*refs, bb, tt):
    ins = refs[:MIXER_B_INS]
    outs = refs[MIXER_B_INS:MIXER_B_INS + MIXER_B_OUTS]
    scr, carry, *scan_scr = refs[MIXER_B_INS + MIXER_B_OUTS:]
    _mixer_b_init(ins[8], ins[9], scr, carry, pl.program_id(2) == 0)
    _mixer_b_step(*ins[:8], *outs, scr, carry, scan_scr, bb=bb, tt=tt)


def _mixer_b_step(bx_ref, bg_ref, w_ref, b_ref, wg_ref, brg_ref, big_ref, lam_ref,
                  o_ref, cst_out_ref, hst_out_ref, scr, carry, scan_scr, *, bb, tt,
                  side_matmul=None):
    keep = CONV_B - 1
    cw = CH_TILE
    rows = bb * tt
    nblk = tt // SUBLANES
    nv = nblk // SUBLANES

    scr[:, HALO:HALO + tt, :] = bx_ref[...]
    cst_out_ref[...] = scr[:, HALO + tt - keep:HALO + tt, :]
    softplus_neg_lam = jax.nn.softplus(-lam_ref[...])
    cin_all = carry[...]
    h_parts, last_parts = [], []
    assert LRU_HD == LANES
    n_heads = cw // LRU_HD
    n_chunks = 1 if side_matmul is None else rows // GATE_CHUNK_ROWS
    rc = rows // n_chunks
    for s in range(n_heads):
        sl = slice(s * LANES, (s + 1) * LANES)
        w_head = jnp.concatenate([wg_ref[0, sl, sl], wg_ref[0, sl, cw + s * LANES:cw + (s + 1) * LANES]],
                                 axis=1)
        a_chunks, b_chunks = [], []
        for k in range(n_chunks):
            acc = None
            for j in range(CONV_B):
                lo = HALO - keep + j
                if tt >= rc:
                    b0, t0 = divmod(k * rc, tt)
                    win = scr[b0:b0 + 1, lo + t0:lo + t0 + rc, sl]
                else:
                    nb = rc // tt
                    win = scr[k * nb:(k + 1) * nb, lo:lo + tt, sl]
                term = w_ref[j:j + 1, sl] * win.reshape(rc, LANES)
                acc = term if acc is None else acc + term
            xb = b_ref[:, sl] + acc
            gates = jnp.dot(xb.astype(BF16), w_head, preferred_element_type=F32)
            if side_matmul is not None:
                side_matmul(s * n_chunks + k, n_heads * n_chunks)
            r = jax.nn.sigmoid(gates[:, :LANES] + brg_ref[:, sl])
            ig = jax.nn.sigmoid(gates[:, LANES:] + big_ref[:, sl])
            log_a = (-LRU_C * r) * softplus_neg_lam[:, sl]
            a = jnp.exp(log_a)
            beta = jnp.sqrt(-jnp.tanh(log_a) * (a * a + 1.0))
            bt = (beta * ig) * xb
            a_k, b_k = _block_scan_real(a.reshape(rc // SUBLANES, SUBLANES, LANES),
                                        bt.reshape(rc // SUBLANES, SUBLANES, LANES))
            a_chunks.append(a_k)
            b_chunks.append(b_k)
        a_blk = jnp.concatenate(a_chunks, axis=0)
        b_blk = jnp.concatenate(b_chunks, axis=0)
        cin = cin_all[:, :, sl]
        if nblk == 1:
            h = b_blk + a_blk * cin
            last = h[:, SUBLANES - 1:SUBLANES, :]
        else:
            acum, bcum, cscr = scan_scr
            acum[s] = a_blk.reshape(bb, tt, LANES)
            bcum[s] = b_blk.reshape(bb, tt, LANES)
            a2 = acum[s, :, pl.ds(SUBLANES - 1, nblk, stride=SUBLANES), :]
            b2 = bcum[s, :, pl.ds(SUBLANES - 1, nblk, stride=SUBLANES), :]
            a2, b2 = _block_scan_real(a2.reshape(bb * nv, SUBLANES, LANES),
                                      b2.reshape(bb * nv, SUBLANES, LANES))
            a2 = a2.reshape(bb, nblk, LANES)
            b2 = b2.reshape(bb, nblk, LANES)
            prev = cin
            ends = []
            for j in range(nv):
                seg = slice(j * SUBLANES, (j + 1) * SUBLANES)
                e = b2[:, seg, :] + a2[:, seg, :] * prev
                ends.append(e)
                prev = e[:, SUBLANES - 1:SUBLANES, :]
            last = prev
            cscr[s, :, 0:1, :] = cin
            cscr[s, :, 1:1 + nblk, :] = jnp.concatenate(ends, axis=1)
            spread = _spread_rows(cscr.at[s], nblk, bb)
            h = b_blk + a_blk * spread.reshape(bb * nblk, SUBLANES, LANES)
        h_parts.append(h.reshape(bb, tt, LANES))
        last_parts.append(last)
    h = jnp.concatenate(h_parts, axis=-1)
    last = jnp.concatenate(last_parts, axis=-1)
    carry[...] = last
    hst_out_ref[...] = last
    out = h * jax.nn.silu(bg_ref[...])
    o_ref[...] = out.reshape(o_ref.shape).astype(o_ref.dtype)
    scr[:, HALO - keep:HALO, :] = scr[:, HALO + tt - keep:HALO + tt, :]


def _mixer_b_operands(proj3, w_conv, b_conv, w_gates, b_rg, b_ig, lam, conv_state, h_state,
                      bb, tt, idx):
    b, t, _ = proj3.shape
    cw = CH_TILE
    nt = t // tt
    nblk = tt // SUBLANES
    blk = (bb, tt, cw)

    def spec(shape, fn):
        return pl.BlockSpec(shape, lambda *g: fn(*idx(*g)))

    vec = spec((1, cw), lambda i, c, j: (0, c))
    scratch = [pltpu.VMEM((bb, HALO + tt, cw), F32), pltpu.VMEM((bb, 1, cw), F32)]
    if nblk > 1:
        assert nblk % SUBLANES == 0
        slabs = cw // LANES
        scratch += [pltpu.VMEM((slabs, bb, tt, LANES), F32),
                    pltpu.VMEM((slabs, bb, tt, LANES), F32),
                    pltpu.VMEM((slabs, bb, HALO + nblk, LANES), F32)]
    return dict(
        args=[proj3, proj3, w_conv, b_conv.reshape(1, W_B), w_gates,
              b_rg.reshape(1, W_B), b_ig.reshape(1, W_B), lam.reshape(1, W_B),
              conv_state, h_state],
        in_specs=[
            spec(blk, lambda i, c, j: (i, j, c)),
            spec(blk, lambda i, c, j: (i, j, W_B // cw + c)),
            spec((CONV_B, cw), lambda i, c, j: (0, c)),
            vec,
            spec((1, cw, 2 * cw), lambda i, c, j: (c, 0, 0)),
            vec, vec, vec,
            spec((bb, CONV_B - 1, cw), lambda i, c, j: (i, 0, c)),
            spec((bb, 1, cw), lambda i, c, j: (i, 0, c)),
        ],
        out_specs=[
            spec((bb * tt, cw), lambda i, c, j: (i * nt + j, c)),
            spec((bb, CONV_B - 1, cw), lambda i, c, j: (i, 0, c)),
            spec((bb, 1, cw), lambda i, c, j: (i, 0, c)),
        ],
        out_shape=[
            jax.ShapeDtypeStruct((b * t, W_B), BF16),
            jax.ShapeDtypeStruct((b, CONV_B - 1, W_B), F32),
            jax.ShapeDtypeStruct((b, 1, W_B), F32),
        ],
        scratch=scratch,
        vmem=2 * (2 * _nbytes(blk, F32) + _nbytes(blk, BF16) + _nbytes((cw, 2 * cw), BF16))
        + _nbytes((bb, HALO + tt, cw), F32) + 16 * _nbytes(blk, F32),
        grid=(b // bb, W_B // cw, nt),
    )


def _mixer_b(proj3, *params, bb, tt):
    ops = _mixer_b_operands(proj3, *params, bb, tt, lambda i, c, j: (i, c, j))
    return pl.pallas_call(
        functools.partial(_mixer_b_kernel, bb=bb, tt=tt),
        grid=ops["grid"],
        in_specs=ops["in_specs"],
        out_specs=ops["out_specs"],
        out_shape=ops["out_shape"],
        scratch_shapes=ops["scratch"],
        compiler_params=_params(("arbitrary",) * 3, ops["vmem"]),
        name="mixer_b",
    )(*ops["args"])


def _fused_proj_kernel(*refs, mixers, ni):
    a_ref, w_ref = refs[:2]
    mixers = [dict(mx) for mx in mixers]
    pos = 2
    for mx in mixers:
        mx["ins"] = refs[pos:pos + mx["n_ins"]]
        pos += mx["n_ins"]
    proj_ref = refs[pos]
    pos += 1
    for mx in mixers:
        mx["outs"] = refs[pos:pos + mx["n_outs"]]
        pos += mx["n_outs"]
    wb_ref = refs[pos]
    pos += 1
    for mx in mixers:
        mx["scr"] = refs[pos:pos + mx["n_scr"]]
        pos += mx["n_scr"]
    s = pl.program_id(0)
    _cast_weight_tile(w_ref, wb_ref, lax.rem(s, ni) == 0)
    for mx in mixers:
        mx["init"](mx["ins"], mx["scr"], s)
    k_total = a_ref.shape[1]
    acc = []

    def projection_part(part, n_parts):
        kc = MXU_DIM
        per_part = k_total // (kc * n_parts)
        assert per_part * kc * n_parts == k_total
        for sub in range(part * per_part, (part + 1) * per_part):
            ks = slice(sub * kc, (sub + 1) * kc)
            term = jnp.dot(a_ref[:, ks], wb_ref[ks, :], preferred_element_type=F32)
            acc[:] = [term if not acc else acc[0] + term]
        if part == n_parts - 1:
            proj_ref[...] = acc[0]

    for mx in mixers[:-1]:
        mx["step"](mx["ins"], mx["outs"], mx["scr"], s, None)
    mx = mixers[-1]
    mx["step"](mx["ins"], mx["outs"], mx["scr"], s, projection_part)


def _fused_proj_call(name, h2d, w_in, layer, col0, ncols, mixers):
    steps = None
    for ops, _, _ in mixers:
        n = 1
        for g in ops["grid"]:
            n *= g
        assert steps in (None, n)
        steps = n
    nj = ncols // PROJ_TN
    ni = steps // nj
    tm = h2d.shape[0] // ni
    assert nj * ni == steps and tm * ni == h2d.shape[0] and tm % SUBLANES == 0
    mm = _in_proj_operands(h2d, w_in, layer, col0, ncols, tm, PROJ_TN,
                           lambda s: (s // ni, s % ni))
    specs = [dict(n_ins=len(ops["args"]), n_outs=len(ops["out_shape"]),
                  n_scr=len(ops["scratch"]), init=init, step=step)
             for ops, init, step in mixers]
    cat = lambda key: [x for ops, _, _ in mixers for x in ops[key]]
    return pl.pallas_call(
        functools.partial(_fused_proj_kernel, mixers=specs, ni=ni),
        grid=(steps,),
        in_specs=mm["in_specs"] + cat("in_specs"),
        out_specs=mm["out_specs"] + cat("out_specs"),
        out_shape=mm["out_shape"] + cat("out_shape"),
        scratch_shapes=mm["scratch"] + cat("scratch"),
        compiler_params=_params(("arbitrary",),
                                mm["vmem"] + sum(ops["vmem"] for ops, _, _ in mixers)),
        name=name,
    )(*mm["args"], *cat("args"))


def _flat_index(dims):
    def idx(s):
        out = []
        for k, d in enumerate(dims):
            stride = 1
            for e in dims[k + 1:]:
                stride *= e
            out.append((s // stride) % d if k else s // stride)
        return tuple(out)
    return idx


def _flat_mixer_b(proj3, *params, bb, tt):
    b, t, _ = proj3.shape
    dims = (b // bb, W_B // CH_TILE, t // tt)
    ops = _mixer_b_operands(proj3, *params, bb, tt, _flat_index(dims))
    nt = dims[2]

    def init(ins, scr, s):
        _mixer_b_init(ins[8], ins[9], scr[0], scr[1], lax.rem(s, nt) == 0)

    def step(ins, outs, scr, s, side_matmul):
        _mixer_b_step(*ins[:8], *outs, scr[0], scr[1], scr[2:], bb=bb, tt=tt,
                      side_matmul=side_matmul)

    return ops, init, step


def _s5_disc_kernel(lre_ref, lim_ref, ldt_ref, btre_ref, btim_ref,
                    abre_ref, abim_ref, bbre_ref, bbim_ref):
    lre = lre_ref[0]
    lim = lim_ref[0]
    dt = jnp.exp(ldt_ref[0])
    mag = jnp.exp(lre * dt)
    ab_re = mag * jnp.cos(lim * dt)
    ab_im = mag * jnp.sin(lim * dt)
    nr = ab_re - 1.0
    den = lre * lre + lim * lim
    fr = (nr * lre + ab_im * lim) / den
    fi = (ab_im * lre - nr * lim) / den
    bre = btre_ref[0]
    bim = btim_ref[0]
    bbre_ref[0] = fr * bre - fi * bim
    bbim_ref[0] = fr * bim + fi * bre
    abre_ref[0] = ab_re
    abim_ref[0] = ab_im


def _s5_discretize(lam_re, lam_im, log_dt, b_re, b_im):
    depth = lam_re.shape[0]
    gp = (depth, S5_G, 1, S5_P)
    ldt = jnp.broadcast_to(log_dt[:, :, None, None], gp)
    bt_shape = (depth, S5_G, S5_GROUP, S5_P)
    small = pl.BlockSpec((1, S5_G, 1, S5_P), lambda l: (l, 0, 0, 0))
    big = pl.BlockSpec((1, S5_G, S5_GROUP, S5_P), lambda l: (l, 0, 0, 0))
    ab_re, ab_im, bb_re, bb_im = pl.pallas_call(
        _s5_disc_kernel,
        grid=(depth,),
        in_specs=[small, small, small, big, big],
        out_specs=[small, small, big, big],
        out_shape=[jax.ShapeDtypeStruct(gp, F32), jax.ShapeDtypeStruct(gp, F32),
                   jax.ShapeDtypeStruct(bt_shape, F32), jax.ShapeDtypeStruct(bt_shape, F32)],
        name="s5_discretize",
    )(lam_re.reshape(gp), lam_im.reshape(gp), ldt,
      jnp.swapaxes(b_re, 2, 3), jnp.swapaxes(b_im, 2, 3))
    return (ab_re.reshape(depth, 1, S5_STATES), ab_im.reshape(depth, 1, S5_STATES),
            bb_re, bb_im)


def _s5_pow_kernel(ar_ref, ai_ref, tab_ref):
    def cmul(x, y):
        return x[0] * y[0] - x[1] * y[1], x[0] * y[1] + x[1] * y[0]

    def powers(b1):
        b2 = cmul(b1, b1)
        b3 = cmul(b2, b1)
        b4 = cmul(b2, b2)
        return (b1, b2, b3, b4, cmul(b4, b1), cmul(b4, b2), cmul(b4, b3), cmul(b4, b4))

    shape = (SUBLANES, S5_STATES)
    row = lax.broadcasted_iota(jnp.int32, shape, 0)
    zero = jnp.zeros(shape, F32)
    pw_a = powers((ar_ref[0], ai_ref[0]))
    pw_a8 = powers(pw_a[SUBLANES - 1])
    for base, pw in ((0, pw_a), (N_TABLES // 2, pw_a8)):
        for part in (0, 1):
            tab_ref[0, base + part] = jnp.where(row >= 1, pw[0][part], zero)
            tab_ref[0, base + 2 + part] = jnp.where(row >= 2, pw[1][part], zero)
            tab_ref[0, base + 4 + part] = jnp.where(row >= 4, pw[3][part], zero)
            p = zero
            for r, v in enumerate(pw):
                p = jnp.where(row == r, v[part], p)
            tab_ref[0, base + 6 + part] = p


def _s5_power_tables(ab_re, ab_im):
    depth = ab_re.shape[0]
    vec = pl.BlockSpec((1, 1, S5_STATES), lambda l: (l, 0, 0))
    tables = pl.pallas_call(
        _s5_pow_kernel,
        grid=(depth,),
        in_specs=[vec, vec],
        out_specs=pl.BlockSpec((1, N_TABLES, SUBLANES, S5_STATES), lambda l: (l, 0, 0, 0)),
        out_shape=jax.ShapeDtypeStruct((depth, N_TABLES, SUBLANES, S5_STATES), F32),
        name="s5_power_tables",
    )(ab_re, ab_im)
    kslab = S5_STATES // S5_KT
    tables = tables.reshape(depth, N_TABLES, SUBLANES, S5_KT, kslab)
    return jnp.transpose(tables, (0, 3, 1, 2, 4)).reshape(
        depth * S5_KT, N_TABLES, SUBLANES, kslab)


def _s5_block_matrices(bb_re, bb_im, c_re, c_im):
    depth = bb_re.shape[0]
    gpt = CH_TILE // S5_GROUP
    chan = lax.broadcasted_iota(jnp.int32, (S5_Q, CH_TILE, CH_TILE), 1)
    state = lax.broadcasted_iota(jnp.int32, (S5_Q, CH_TILE, CH_TILE), 2)
    slab = lax.broadcasted_iota(jnp.int32, (S5_Q, CH_TILE, CH_TILE), 0)
    gps = LANES // S5_P
    keep = (chan // S5_GROUP) == gps * slab + (state % LANES) // S5_P

    def dense_b(x):
        return x.reshape(depth * S5_KT, gpt * S5_GROUP, S5_P)

    db = jnp.concatenate([dense_b(bb_re)] * gps + [dense_b(bb_im)] * gps, axis=-1)
    bm = jnp.where(keep[None], db[:, None], 0.0).astype(BF16)

    def dense_c(x):
        x = x.reshape(depth * S5_KT, gpt, S5_GROUP, S5_P)
        return jnp.transpose(x, (0, 3, 1, 2)).reshape(depth * S5_KT, S5_P, gpt * S5_GROUP)

    dc = jnp.concatenate([dense_c(c_re)] * gps + [dense_c(-c_im)] * gps, axis=1)
    keep_c = jnp.swapaxes(keep, 1, 2)
    cm = jnp.where(keep_c[None], dc[:, None], 0.0).astype(BF16)
    return bm, cm


MIXER_C_INS = 10
MIXER_C_OUTS = 3


def _mixer_c_init(sre_ref, sim_ref, car_r, car_i, kt, first_time_tile):
    @pl.when(first_time_tile)
    def _():
        car_r[kt] = sre_ref[...]
        car_i[kt] = sim_ref[...]


def _mixer_c_kernel(*refs, bb, tt):
    ins = refs[:MIXER_C_INS]
    outs = refs[MIXER_C_INS:MIXER_C_INS + MIXER_C_OUTS]
    ystore, car_r, car_i, *scan_scr = refs[MIXER_C_INS + MIXER_C_OUTS:]
    kt = pl.program_id(2)
    _mixer_c_init(ins[8], ins[9], car_r, car_i, kt, pl.program_id(1) == 0)
    _mixer_c_step(*ins[:8], *outs, ystore, car_r, car_i, scan_scr, kt, bb=bb, tt=tt)


def _mixer_c_step(cu_ref, cg_ref, bm_ref, cm_ref, tab_ref, d_ref, wglu_ref, bglu_ref,
                  o_ref, sre_out_ref, sim_out_ref,
                  ystore, car_r, car_i, scan_scr, kt, *, bb, tt, side_matmul=None):
    cw = CH_TILE
    rows = bb * tt
    nblk = tt // SUBLANES
    nv = nblk // SUBLANES
    n = bb * nblk
    base8 = N_TABLES // 2

    u = cu_ref[...].reshape(rows, cw)
    ub = u.astype(BF16)
    y = d_ref[...] * u
    cr_all = car_r[kt]
    ci_all = car_i[kt]
    new_r, new_i = [], []
    for q in range(S5_Q):
        sl = slice(LANES * q, LANES * (q + 1))
        bu = jnp.dot(ub, bm_ref[kt, q], preferred_element_type=F32)
        if side_matmul is not None:
            side_matmul(q, S5_Q)
        br, bi = _block_scan_cplx(bu[:, :LANES].reshape(n, SUBLANES, LANES),
                                  bu[:, LANES:].reshape(n, SUBLANES, LANES), tab_ref, kt, 0, sl)
        p_r = tab_ref[kt, 6, :, sl]
        p_i = tab_ref[kt, 7, :, sl]
        cr = cr_all[:, :, sl]
        ci = ci_all[:, :, sl]
        if nblk == 1:
            kr, ki = cr, ci
        else:
            hsr, hsi, csr, csi = scan_scr
            hsr[...] = br.reshape(bb, tt, LANES)
            hsi[...] = bi.reshape(bb, tt, LANES)
            xr = hsr[:, pl.ds(SUBLANES - 1, nblk, stride=SUBLANES), :]
            xi = hsi[:, pl.ds(SUBLANES - 1, nblk, stride=SUBLANES), :]
            xr, xi = _block_scan_cplx(xr.reshape(bb * nv, SUBLANES, LANES),
                                      xi.reshape(bb * nv, SUBLANES, LANES), tab_ref, kt, base8,
                                      sl)
            xr = xr.reshape(bb, nblk, LANES)
            xi = xi.reshape(bb, nblk, LANES)
            q_r = tab_ref[kt, base8 + 6, :, sl]
            q_i = tab_ref[kt, base8 + 7, :, sl]
            prev_r, prev_i = cr, ci
            ends_r, ends_i = [], []
            for j in range(nv):
                seg = slice(j * SUBLANES, (j + 1) * SUBLANES)
                er = xr[:, seg, :] + (q_r * prev_r - q_i * prev_i)
                ei = xi[:, seg, :] + (q_r * prev_i + q_i * prev_r)
                ends_r.append(er)
                ends_i.append(ei)
                prev_r = er[:, SUBLANES - 1:SUBLANES, :]
                prev_i = ei[:, SUBLANES - 1:SUBLANES, :]
            csr[:, 0:1, :] = cr
            csi[:, 0:1, :] = ci
            csr[:, 1:1 + nblk, :] = jnp.concatenate(ends_r, axis=1)
            csi[:, 1:1 + nblk, :] = jnp.concatenate(ends_i, axis=1)
            kr = _spread_rows(csr, nblk, bb).reshape(n, SUBLANES, LANES)
            ki = _spread_rows(csi, nblk, bb).reshape(n, SUBLANES, LANES)
        hr = br + (p_r * kr - p_i * ki)
        hi = bi + (p_r * ki + p_i * kr)
        hr = hr.reshape(bb, tt, LANES)
        hi = hi.reshape(bb, tt, LANES)
        new_r.append(hr[:, tt - 1:tt, :])
        new_i.append(hi[:, tt - 1:tt, :])
        hcat = jnp.concatenate([hr.reshape(rows, LANES), hi.reshape(rows, LANES)], axis=-1)
        y = y + jnp.dot(hcat.astype(BF16), cm_ref[kt, q], preferred_element_type=F32)

    ncr_all = jnp.concatenate(new_r, axis=-1)
    nci_all = jnp.concatenate(new_i, axis=-1)
    car_r[kt] = ncr_all
    car_i[kt] = nci_all
    sre_out_ref[:, kt] = ncr_all
    sim_out_ref[:, kt] = nci_all
    ystore[kt] = y

    @pl.when(kt == S5_KT - 1)
    def _():
        y_all = jnp.concatenate([ystore[k] for k in range(S5_KT)], axis=-1)
        yg = jax.nn.gelu(y_all)
        z = jnp.dot(yg.astype(BF16), wglu_ref[0].astype(BF16),
                    preferred_element_type=F32) + bglu_ref[...]
        yy = yg * jax.nn.sigmoid(z)
        out = yy * jax.nn.silu(cg_ref[...].reshape(rows, W_C))
        o_ref[...] = out.astype(o_ref.dtype)


def _mixer_c_operands(proj3, bmat, cmat, tables, d_skip, w_glu, b_glu, s_re, s_im, layer,
                      bb, tt, idx):
    b, t, _ = proj3.shape
    u_col = 0
    cw = CH_TILE
    nt = t // tt
    nblk = tt // SUBLANES
    kslab = S5_Q * LANES

    def spec(shape, fn):
        return pl.BlockSpec(shape, lambda *g: fn(*idx(*g)))

    mat = spec((S5_KT, S5_Q, cw, cw), lambda i, j, k: (layer, 0, 0, 0))
    st = spec((bb, 1, kslab), lambda i, j, k: (i, 0, k))
    st_out = spec((bb, S5_KT, 1, kslab), lambda i, j, k: (i, 0, 0, 0))
    rows = bb * tt
    scratch = [pltpu.VMEM((S5_KT, rows, cw), F32),
               pltpu.VMEM((S5_KT, bb, 1, kslab), F32),
               pltpu.VMEM((S5_KT, bb, 1, kslab), F32)]
    if nblk > 1:
        assert nblk % SUBLANES == 0 and nblk // SUBLANES <= SUBLANES
        scratch += [pltpu.VMEM((bb, tt, LANES), F32), pltpu.VMEM((bb, tt, LANES), F32),
                    pltpu.VMEM((bb, HALO + nblk, LANES), F32),
                    pltpu.VMEM((bb, HALO + nblk, LANES), F32)]
    vmem = 2 * (_nbytes((bb, tt, cw), F32) + _nbytes((bb, tt, W_C), F32)
                + 2 * _nbytes((S5_KT, S5_Q, cw, cw), BF16)
                + _nbytes((S5_KT, N_TABLES, SUBLANES, kslab), F32)
                + _nbytes((W_C, W_C), F32) + _nbytes((bb, tt, W_C), BF16)
                + 4 * _nbytes((bb, SUBLANES, kslab), F32)) \
        + _nbytes((S5_KT, rows, cw), F32) + 2 * _nbytes((bb, tt, LANES), F32) \
        + 2 * _nbytes((S5_KT, bb, SUBLANES, kslab), F32) + 10 * _nbytes((rows, W_C), F32)
    return dict(
        args=[proj3, proj3, bmat, cmat, tables, d_skip.reshape(1, W_C), w_glu,
              b_glu.reshape(1, W_C), s_re, s_im],
        in_specs=[
            spec((bb, tt, cw), lambda i, j, k: (i, j, u_col // cw + k)),
            spec((bb, tt, W_C), lambda i, j, k: (i, j, u_col // W_C + 1)),
            mat, mat,
            spec((S5_KT, N_TABLES, SUBLANES, kslab), lambda i, j, k: (layer, 0, 0, 0)),
            spec((1, cw), lambda i, j, k: (0, k)),
            spec((1, W_C, W_C), lambda i, j, k: (layer, 0, 0)),
            spec((1, W_C), lambda i, j, k: (0, 0)),
            st, st,
        ],
        out_specs=[
            spec((rows, W_C), lambda i, j, k: (i * nt + j, 0)),
            st_out, st_out,
        ],
        out_shape=[
            jax.ShapeDtypeStruct((b * t, W_C), BF16),
            jax.ShapeDtypeStruct((b, S5_KT, 1, kslab), F32),
            jax.ShapeDtypeStruct((b, S5_KT, 1, kslab), F32),
        ],
        scratch=scratch,
        vmem=vmem,
        grid=(b // bb, nt, S5_KT),
    )


def _mixer_c(proj3, *params, bb, tt):
    ops = _mixer_c_operands(proj3, *params, bb, tt, lambda i, j, k: (i, j, k))
    return pl.pallas_call(
        functools.partial(_mixer_c_kernel, bb=bb, tt=tt),
        grid=ops["grid"],
        in_specs=ops["in_specs"],
        out_specs=ops["out_specs"],
        out_shape=ops["out_shape"],
        scratch_shapes=ops["scratch"],
        compiler_params=_params(("arbitrary",) * 3, ops["vmem"]),
        name="mixer_c",
    )(*ops["args"])


def _flat_mixer_c(proj3, *params, bb, tt):
    b, t, _ = proj3.shape
    dims = (b // bb, t // tt, S5_KT)
    ops = _mixer_c_operands(proj3, *params, bb, tt, _flat_index(dims))
    nt = dims[1]

    def init(ins, scr, s):
        _mixer_c_init(ins[8], ins[9], scr[1], scr[2], lax.rem(s, S5_KT),
                      lax.rem(s // S5_KT, nt) == 0)

    def step(ins, outs, scr, s, side_matmul):
        _mixer_c_step(*ins[:8], *outs, scr[0], scr[1], scr[2], scr[3:],
                      lax.rem(s, S5_KT), bb=bb, tt=tt, side_matmul=side_matmul)

    return ops, init, step


OUT_PROJ_TN = 512


def _out_proj_kernel(ma_ref, mb_ref, mc_ref, w_ref, x_ref, gate_ref, o_ref, wb_ref, *, bb, tt):
    _cast_weight_tile(w_ref, wb_ref, pl.program_id(1) == 0)
    acc = jnp.dot(ma_ref[...], wb_ref[0:W_A, :], preferred_element_type=F32)
    acc = acc + jnp.dot(mb_ref[...], wb_ref[W_A:W_A + W_B, :], preferred_element_type=F32)
    acc = acc + jnp.dot(mc_ref[...], wb_ref[W_A + W_B:, :], preferred_element_type=F32)
    o_ref[...] = x_ref[...] + gate_ref[...] * acc.reshape(bb, tt, acc.shape[-1])


def _out_projection(out_a, out_b, out_c, w_out, x, gate, layer, bb, tt):
    b, t, d = x.shape
    tm = bb * tt
    tn = OUT_PROJ_TN
    nt = t // tt
    m = b * t

    def rows(width):
        return pl.BlockSpec((tm, width), lambda j, i: (i, 0))

    vmem = 2 * (_nbytes((tm, d), BF16) + _nbytes((d, tn), F32)
                + 2 * _nbytes((tm, tn), F32)) + _nbytes((d, tn), BF16) \
        + 3 * _nbytes((tm, tn), F32)
    return pl.pallas_call(
        functools.partial(_out_proj_kernel, bb=bb, tt=tt),
        grid=(d // tn, m // tm),
        in_specs=[
            rows(W_A), rows(W_B), rows(W_C),
            pl.BlockSpec((1, d, tn), lambda j, i: (layer, 0, j)),
            pl.BlockSpec((bb, tt, tn), lambda j, i: (i // nt, i % nt, j)),
            pl.BlockSpec((bb, 1, tn), lambda j, i: (i // nt, 0, j)),
        ],
        out_specs=pl.BlockSpec((bb, tt, tn), lambda j, i: (i // nt, i % nt, j)),
        out_shape=jax.ShapeDtypeStruct((b, t, d), F32),
        scratch_shapes=[pltpu.VMEM((d, tn), BF16)],
        compiler_params=_params(("arbitrary", "arbitrary"), vmem + (4 << 20)),
        name="out_projection",
    )(out_a, out_b, out_c, w_out, x, gate)


def _gate_weights(w_rg, w_ig):
    hpt = CH_TILE // LRU_HD
    pairs = LRU_HEADS // hpt
    w = jnp.stack([w_rg, w_ig], axis=2)
    w = w.reshape(pairs, hpt, LRU_HD, 2, LRU_HD)
    eye = jnp.eye(hpt, dtype=bool)
    out = jnp.where(eye[None, :, None, None, :, None],
                    w[:, :, :, :, None, :], 0.0)
    return out.reshape(pairs, CH_TILE, 2 * CH_TILE).astype(BF16)


def _layer(x, mod, states, wts, layer, tiles):
    b, t, d = x.shape
    conv_a, conv_b, lru_h, s5_re, s5_im = states
    shift, scale, gate = (mod[:, None, k * d:(k + 1) * d] for k in range(3))

    h = _norm_modulate(x, scale, shift, wts["g_norm"], *tiles["norm"])
    w_in = wts["w_in"]
    col_a, col_b, col_c = 0, 4 * W_A, 4 * W_A + 2 * W_B
    b_params = (wts["w_conv_b"], wts["b_conv_b"], wts["w_gates"], wts["b_rg"], wts["b_ig"],
                wts["lru_lambda"], conv_b, lru_h.reshape(b, 1, W_B))
    c_params = (wts["bmat"], wts["cmat"], wts["tables"], wts["s5_d"], wts["w_glu"],
                wts["b_glu"], s5_re.reshape(b, 1, S5_STATES), s5_im.reshape(b, 1, S5_STATES),
                layer)
    bb_b, tt_b = tiles["b"]
    bb_c, tt_c = tiles["c"]

    proj_c = _in_projection(h, w_in, layer, col_c, 2 * W_C).reshape(b, t, 2 * W_C)
    if tiles["fuse"]:
        proj_b, out_c, re_new, im_new = _fused_proj_call(
            "proj_b_mixer_c", h, w_in, layer, col_b, 2 * W_B,
            [_flat_mixer_c(proj_c, *c_params, bb=bb_c, tt=tt_c)])
        proj_b = proj_b.reshape(b, t, 2 * W_B)
        proj_a, out_b, conv_b_new, lru_new = _fused_proj_call(
            "proj_a_mixer_b", h, w_in, layer, col_a, 4 * W_A,
            [_flat_mixer_b(proj_b, *b_params, bb=bb_b, tt=tt_b)])
    else:
        out_c, re_new, im_new = _mixer_c(proj_c, *c_params, bb=bb_c, tt=tt_c)
        proj_b = _in_projection(h, w_in, layer, col_b, 2 * W_B).reshape(b, t, 2 * W_B)
        out_b, conv_b_new, lru_new = _mixer_b(proj_b, *b_params, bb=bb_b, tt=tt_b)
        proj_a = _in_projection(h, w_in, layer, col_a, 4 * W_A)
    proj_a = proj_a.reshape(b, t, 4 * W_A)
    out_a, conv_a_new = _mixer_a(proj_a, wts["w_conv_a"], wts["b_conv_a"], conv_a,
                                 *tiles["a"])

    x_new = _out_projection(out_a, out_b, out_c, wts["w_out"], x, gate, layer, *tiles["out"])
    return (x_new, conv_a_new, conv_b_new, lru_new.reshape(b, W_B),
            re_new.reshape(b, S5_G, S5_P), im_new.reshape(b, S5_G, S5_P))


PROMPT_TILES = {"norm": (1, 512), "a": (1, 2048), "b": (1, 1024), "c": (1, 256),
                "out": (1, 1024), "final": (1, 512), "fuse": True}
SAMPLE_TILES = {"norm": (32, 8), "a": (128, 8), "b": (64, 8), "c": (32, 8),
                "out": (128, 8), "final": (32, 8), "fuse": False}


def kernel(x_prompt, x_sample, c_prompt, c_sample, state_conv_a, state_conv_b, state_lru_h,
           state_s5_re, state_s5_im, g_norm, w_ada, b_ada, w_in, w_conv_a, b_conv_a,
           w_conv_b, b_conv_b, w_rg, b_rg, w_ig, b_ig, lru_lambda, s5_lambda_re,
           s5_lambda_im, s5_log_dt, s5_b_re, s5_b_im, s5_c_re, s5_c_im, s5_d, w_glu,
           b_glu, w_out, g_final):
    depth = w_in.shape[0]
    bp = x_prompt.shape[0]
    bs = x_sample.shape[0]

    c_all = jnp.concatenate([c_prompt, c_sample], axis=0)
    pad = (-c_all.shape[0]) % SUBLANES
    c_all = jnp.pad(c_all, ((0, pad), (0, 0)))
    mod = _modulation(c_all, w_ada, b_ada)

    ab_re, ab_im, bb_re, bb_im = _s5_discretize(s5_lambda_re, s5_lambda_im, s5_log_dt,
                                               s5_b_re, s5_b_im)
    tables = _s5_power_tables(ab_re, ab_im)
    bmat, cmat = _s5_block_matrices(bb_re, bb_im, s5_c_re, s5_c_im)
    w_glu_bf16 = w_glu.astype(BF16)

    xp, xs = x_prompt, x_sample
    zeros_p = (jnp.zeros((bp, CONV_A - 1, W_A), F32), jnp.zeros((bp, CONV_B - 1, W_B), F32),
               jnp.zeros((bp, W_B), F32), jnp.zeros((bp, S5_G, S5_P), F32),
               jnp.zeros((bp, S5_G, S5_P), F32))
    outs_p, outs_s = [], []
    for l in range(depth):
        wts = {
            "g_norm": g_norm[l], "w_in": w_in,
            "w_conv_a": w_conv_a[l], "b_conv_a": b_conv_a[l],
            "w_conv_b": w_conv_b[l], "b_conv_b": b_conv_b[l],
            "w_gates": _gate_weights(w_rg[l], w_ig[l]), "b_rg": b_rg[l], "b_ig": b_ig[l],
            "lru_lambda": lru_lambda[l], "bmat": bmat, "cmat": cmat,
            "tables": tables, "s5_d": s5_d[l],
            "w_glu": w_glu_bf16, "b_glu": b_glu[l], "w_out": w_out,
        }
        res_p = _layer(xp, mod[l, :bp], zeros_p, wts, l, PROMPT_TILES)
        xp = res_p[0]
        outs_p.append(res_p[1:])
        st_s = (state_conv_a[l], state_conv_b[l], state_lru_h[l], state_s5_re[l],
                state_s5_im[l])
        res_s = _layer(xs, mod[l, bp:bp + bs], st_s, wts, l, SAMPLE_TILES)
        xs = res_s[0]
        outs_s.append(res_s[1:])

    y_prompt = _final_norm(xp, g_final, *PROMPT_TILES["final"])
    y_sample = _final_norm(xs, g_final, *SAMPLE_TILES["final"])
    stack = lambda outs, k: jnp.stack([o[k] for o in outs])
    return (y_prompt, y_sample,
            *(stack(outs_p, k) for k in range(5)),
            *(stack(outs_s, k) for k in range(5)))
```

```python
import functools

import jax
import jax.numpy as jnp
from jax import lax
from jax.experimental import pallas as pl
from jax.experimental.pallas import tpu as pltpu

F32 = jnp.float32
BF16 = jnp.bfloat16

D_MODEL = 4096
W_A = D_MODEL // 4
W_B = D_MODEL // 2
W_C = D_MODEL // 4
IN_COLS = 4 * W_A + 2 * W_B + 2 * W_C
CONV_A = 3
CONV_B = 4
LRU_HEADS = 16
LRU_HD = W_B // LRU_HEADS
LRU_C = 8.0
S5_GROUP = 16
S5_G = W_C // S5_GROUP
S5_P = 64
S5_STATES = S5_G * S5_P
EPS = 1e-6

SUBLANES = 8
LANES = 128
MXU_DIM = 256
VMEM_LIMIT_CAP = 60000 * 1024
CH_TILE = MXU_DIM
S5_KT = W_C // CH_TILE
S5_Q = (CH_TILE // S5_GROUP) * S5_P // LANES
HALO = SUBLANES
N_TABLES = 16
GATE_CHUNK_ROWS = 128


def _params(sem, vmem_bytes):
    return pltpu.CompilerParams(
        dimension_semantics=sem,
        vmem_limit_bytes=int(min(VMEM_LIMIT_CAP, vmem_bytes)))


def _nbytes(shape, dtype):
    n = 1
    for s in shape:
        n *= s
    return n * jnp.dtype(dtype).itemsize


def _mod_kernel(c_ref, w_ref, b_ref, o_ref):
    c = c_ref[...].astype(BF16)
    w = w_ref[0].astype(BF16)
    o_ref[0] = jnp.dot(c, w, preferred_element_type=F32) + b_ref[0]


def _modulation(c_all, w_ada, b_ada):
    depth, d, n = w_ada.shape
    rows = c_all.shape[0]
    tn = 512
    vmem = 2 * (_nbytes((rows, d), F32) + _nbytes((d, tn), F32)
                + _nbytes((rows, tn), F32)) + _nbytes((d, tn), F32)
    return pl.pallas_call(
        _mod_kernel,
        grid=(depth, n // tn),
        in_specs=[
            pl.BlockSpec((rows, d), lambda l, j: (0, 0)),
            pl.BlockSpec((1, d, tn), lambda l, j: (l, 0, j)),
            pl.BlockSpec((1, 1, tn), lambda l, j: (l, 0, j)),
        ],
        out_specs=pl.BlockSpec((1, rows, tn), lambda l, j: (l, 0, j)),
        out_shape=jax.ShapeDtypeStruct((depth, rows, n), F32),
        compiler_params=_params(("arbitrary", "arbitrary"), vmem + (8 << 20)),
        name="adaln_modulation",
    )(c_all, w_ada, b_ada.reshape(depth, 1, n))


def _norm_mod_kernel(x_ref, scale_ref, shift_ref, g_ref, o_ref):
    x = x_ref[...]
    ms = jnp.mean(x * x, axis=-1, keepdims=True)
    y = (x * lax.rsqrt(ms + EPS)) * g_ref[...]
    h = y * (1.0 + scale_ref[...]) + shift_ref[...]
    o_ref[...] = h.reshape(o_ref.shape).astype(o_ref.dtype)


def _norm_modulate(x, scale, shift, g, bb, tt):
    b, t, d = x.shape
    nt = t // tt
    vmem = 2 * (_nbytes((bb, tt, d), F32) + _nbytes((bb, tt, d), BF16)) \
        + 4 * _nbytes((bb, tt, d), F32)
    return pl.pallas_call(
        _norm_mod_kernel,
        grid=(b // bb, nt),
        in_specs=[
            pl.BlockSpec((bb, tt, d), lambda i, j: (i, j, 0)),
            pl.BlockSpec((bb, 1, d), lambda i, j: (i, 0, 0)),
            pl.BlockSpec((bb, 1, d), lambda i, j: (i, 0, 0)),
            pl.BlockSpec((1, 1, d), lambda i, j: (0, 0, 0)),
        ],
        out_specs=pl.BlockSpec((bb * tt, d), lambda i, j: (i * nt + j, 0)),
        out_shape=jax.ShapeDtypeStruct((b * t, d), BF16),
        compiler_params=_params(("arbitrary", "arbitrary"), vmem),
        name="norm_modulate",
    )(x, scale, shift, g.reshape(1, 1, d))


def _final_norm_kernel(x_ref, g_ref, o_ref):
    x = x_ref[...]
    ms = jnp.mean(x * x, axis=-1, keepdims=True)
    o_ref[...] = (x * lax.rsqrt(ms + EPS)) * g_ref[...]


def _final_norm(x, g, bb, tt):
    b, t, d = x.shape
    vmem = 8 * _nbytes((bb, tt, d), F32)
    return pl.pallas_call(
        _final_norm_kernel,
        grid=(b // bb, t // tt),
        in_specs=[
            pl.BlockSpec((bb, tt, d), lambda i, j: (i, j, 0)),
            pl.BlockSpec((1, 1, d), lambda i, j: (0, 0, 0)),
        ],
        out_specs=pl.BlockSpec((bb, tt, d), lambda i, j: (i, j, 0)),
        out_shape=jax.ShapeDtypeStruct((b, t, d), F32),
        compiler_params=_params(("arbitrary", "arbitrary"), vmem),
        name="final_norm",
    )(x, g.reshape(1, 1, d))


PROJ_TN = 512
PROJ_ALONE_TILE = (1024, 512)


def _cast_weight_tile(w_ref, wb_ref, first_row_tile):
    @pl.when(first_row_tile)
    def _():
        wb_ref[...] = w_ref[0].astype(BF16)


def _in_proj_kernel(a_ref, w_ref, o_ref, wb_ref):
    _cast_weight_tile(w_ref, wb_ref, pl.program_id(1) == 0)
    o_ref[...] = jnp.dot(a_ref[...], wb_ref[...], preferred_element_type=F32)


def _in_proj_operands(h2d, w_in, layer, col0, ncols, tm, tn, idx):
    m, k = h2d.shape
    assert col0 % tn == 0 and ncols % tn == 0 and m % tm == 0
    return dict(
        args=[h2d, w_in],
        in_specs=[
            pl.BlockSpec((tm, k), lambda *g: (idx(*g)[1], 0)),
            pl.BlockSpec((1, k, tn), lambda *g: (layer, 0, col0 // tn + idx(*g)[0])),
        ],
        out_specs=[pl.BlockSpec((tm, tn), lambda *g: (idx(*g)[1], idx(*g)[0]))],
        out_shape=[jax.ShapeDtypeStruct((m, ncols), F32)],
        scratch=[pltpu.VMEM((k, tn), BF16)],
        vmem=2 * (_nbytes((tm, k), BF16) + _nbytes((k, tn), F32) + _nbytes((tm, tn), F32))
        + _nbytes((k, tn), BF16) + _nbytes((tm, tn), F32),
    )


def _in_projection(h2d, w_in, layer, col0, ncols):
    m = h2d.shape[0]
    tm, tn = PROJ_ALONE_TILE
    ops = _in_proj_operands(h2d, w_in, layer, col0, ncols, tm, tn, lambda j, i: (j, i))
    return pl.pallas_call(
        _in_proj_kernel,
        grid=(ncols // tn, m // tm),
        in_specs=ops["in_specs"],
        out_specs=ops["out_specs"][0],
        out_shape=ops["out_shape"][0],
        scratch_shapes=ops["scratch"],
        compiler_params=_params(("arbitrary", "arbitrary"), ops["vmem"] + (4 << 20)),
        name="in_projection",
    )(*ops["args"])


def _causal_taps(scr, v, w_ref, taps, tt):
    scr[:, HALO:HALO + tt, :] = v
    acc = None
    for k in range(taps):
        src = v if k == taps - 1 else scr[:, HALO - (taps - 1) + k:HALO - (taps - 1) + k + tt, :]
        term = w_ref[k:k + 1, :] * src
        acc = term if acc is None else acc + term
    return acc


def _block_scan_real(a, b):
    row = lax.broadcasted_iota(jnp.int32, (1, SUBLANES, a.shape[-1]), 1)
    for d in (1, 2, 4):
        keep = row >= d
        a_sh = pltpu.roll(a, d, 1)
        b_sh = pltpu.roll(b, d, 1)
        b = b + a * jnp.where(keep, b_sh, 0.0)
        a = a * jnp.where(keep, a_sh, 1.0)
    return a, b


def _block_scan_cplx(br, bi, tab_ref, kt, base, sl):
    for lvl, d in enumerate((1, 2, 4)):
        lr = tab_ref[kt, base + 2 * lvl, :, sl]
        li = tab_ref[kt, base + 2 * lvl + 1, :, sl]
        sr = pltpu.roll(br, d, 1)
        si = pltpu.roll(bi, d, 1)
        br, bi = br + (lr * sr - li * si), bi + (lr * si + li * sr)
    return br, bi


def _spread_rows(scr, nblk, bb):
    return jnp.concatenate(
        [jnp.broadcast_to(scr[:, k:k + 1, :], (bb, SUBLANES, LANES)) for k in range(nblk)],
        axis=1)


def _mixer_a_kernel(ab_ref, ac_ref, ax_ref, ag_ref, w_ref, b_ref, st_ref,
                    o_ref, st_out_ref, scr, *, tt):
    keep = CONV_A - 1

    @pl.when(pl.program_id(2) == 0)
    def _():
        scr[:, HALO - keep:HALO, :] = st_ref[...]

    conv_in = ac_ref[...] * ax_ref[...]
    y = b_ref[...] + _causal_taps(scr, conv_in, w_ref, CONV_A, tt)
    out = (ab_ref[...] * y) * jax.nn.silu(ag_ref[...])
    o_ref[...] = out.reshape(o_ref.shape).astype(o_ref.dtype)
    tail = scr[:, HALO + tt - keep:HALO + tt, :]
    st_out_ref[...] = tail
    scr[:, HALO - keep:HALO, :] = tail


def _mixer_a(proj3, w_conv, b_conv, state, bb, tt):
    b, t, _ = proj3.shape
    cw = CH_TILE
    nc = W_A // cw
    nt = t // tt
    blk = (bb, tt, cw)

    def col(off):
        return pl.BlockSpec(blk, lambda i, c, j, off=off: (i, j, off // cw + c))

    vmem = 2 * (4 * _nbytes(blk, F32) + _nbytes(blk, BF16)) \
        + _nbytes((bb, HALO + tt, cw), F32) + 6 * _nbytes(blk, F32)
    return pl.pallas_call(
        functools.partial(_mixer_a_kernel, tt=tt),
        grid=(b // bb, nc, nt),
        in_specs=[
            col(0), col(W_A), col(2 * W_A), col(3 * W_A),
            pl.BlockSpec((CONV_A, cw), lambda i, c, j: (0, c)),
            pl.BlockSpec((1, cw), lambda i, c, j: (0, c)),
            pl.BlockSpec((bb, CONV_A - 1, cw), lambda i, c, j: (i, 0, c)),
        ],
        out_specs=[
            pl.BlockSpec((bb * tt, cw), lambda i, c, j: (i * nt + j, c)),
            pl.BlockSpec((bb, CONV_A - 1, cw), lambda i, c, j: (i, 0, c)),
        ],
        out_shape=[
            jax.ShapeDtypeStruct((b * t, W_A), BF16),
            jax.ShapeDtypeStruct((b, CONV_A - 1, W_A), F32),
        ],
        scratch_shapes=[pltpu.VMEM((bb, HALO + tt, cw), F32)],
        compiler_params=_params(("arbitrary",) * 3, vmem),
        name="mixer_a",
    )(proj3, proj3, proj3, proj3, w_conv, b_conv.reshape(1, W_A), state)


MIXER_B_INS = 10
MIXER_B_OUTS = 3


def _mixer_b_init(cst_ref, hst_ref, scr, carry, first_time_tile):
    @pl.when(first_time_tile)
    def _():
        scr[:, HALO - (CONV_B - 1):HALO, :] = cst_ref[...]
        carry[...] = hst_ref[...]


def _mixer_b_kernel(*refs, bb, tt):
    ins = refs[:MIXER_B_INS]
    outs = refs[MIXER_B_INS:MIXER_B_INS + MIXER_B_OUTS]
    scr, carry, *scan_scr = refs[MIXER_B_INS + MIXER_B_OUTS:]
    _mixer_b_init(ins[8], ins[9], scr, carry, pl.program_id(2) == 0)
    _mixer_b_step(*ins[:8], *outs, scr, carry, scan_scr, bb=bb, tt=tt)


def _mixer_b_step(bx_ref, bg_ref, w_ref, b_ref, wg_ref, brg_ref, big_ref, lam_ref,
                  o_ref, cst_out_ref, hst_out_ref, scr, carry, scan_scr, *, bb, tt,
                  side_matmul=None):
    keep = CONV_B - 1
    cw = CH_TILE
    rows = bb * tt
    nblk = tt // SUBLANES
    nv = nblk // SUBLANES

    scr[:, HALO:HALO + tt, :] = bx_ref[...]
    cst_out_ref[...] = scr[:, HALO + tt - keep:HALO + tt, :]
    softplus_neg_lam = jax.nn.softplus(-lam_ref[...])
    cin_all = carry[...]
    h_parts, last_parts = [], []
    assert LRU_HD == LANES
    n_heads = cw // LRU_HD
    n_chunks = 1 if side_matmul is None else rows // GATE_CHUNK_ROWS
    rc = rows // n_chunks
    for s in range(n_heads):
        sl = slice(s * LANES, (s + 1) * LANES)
        w_head = jnp.concatenate([wg_ref[0, sl, sl], wg_ref[0, sl, cw + s * LANES:cw + (s + 1) * LANES]],
                                 axis=1)
        a_chunks, b_chunks = [], []
        for k in range(n_chunks):
            acc = None
            for j in range(CONV_B):
                lo = HALO - keep + j
                if tt >= rc:
                    b0, t0 = divmod(k * rc, tt)
                    win = scr[b0:b0 + 1, lo + t0:lo + t0 + rc, sl]
                else:
                    nb = rc // tt
                    win = scr[k * nb:(k + 1) * nb, lo:lo + tt, sl]
                term = w_ref[j:j + 1, sl] * win.reshape(rc, LANES)
                acc = term if acc is None else acc + term
            xb = b_ref[:, sl] + acc
            gates = jnp.dot(xb.astype(BF16), w_head, preferred_element_type=F32)
            if side_matmul is not None:
                side_matmul(s * n_chunks + k, n_heads * n_chunks)
            r = jax.nn.sigmoid(gates[:, :LANES] + brg_ref[:, sl])
            ig = jax.nn.sigmoid(gates[:, LANES:] + big_ref[:, sl])
            log_a = (-LRU_C * r) * softplus_neg_lam[:, sl]
            a = jnp.exp(log_a)
            beta = jnp.sqrt(-jnp.tanh(log_a) * (a * a + 1.0))
            bt = (beta * ig) * xb
            a_k, b_k = _block_scan_real(a.reshape(rc // SUBLANES, SUBLANES, LANES),
                                        bt.reshape(rc // SUBLANES, SUBLANES, LANES))
            a_chunks.append(a_k)
            b_chunks.append(b_k)
        a_blk = jnp.concatenate(a_chunks, axis=0)
        b_blk = jnp.concatenate(b_chunks, axis=0)
        cin = cin_all[:, :, sl]
        if nblk == 1:
            h = b_blk + a_blk * cin
            last = h[:, SUBLANES - 1:SUBLANES, :]
        else:
            acum, bcum, cscr = scan_scr
            acum[s] = a_blk.reshape(bb, tt, LANES)
            bcum[s] = b_blk.reshape(bb, tt, LANES)
            a2 = acum[s, :, pl.ds(SUBLANES - 1, nblk, stride=SUBLANES), :]
            b2 = bcum[s, :, pl.ds(SUBLANES - 1, nblk, stride=SUBLANES), :]
            a2, b2 = _block_scan_real(a2.reshape(bb * nv, SUBLANES, LANES),
                                      b2.reshape(bb * nv, SUBLANES, LANES))
            a2 = a2.reshape(bb, nblk, LANES)
            b2 = b2.reshape(bb, nblk, LANES)
            prev = cin
            ends = []
            for j in range(nv):
                seg = slice(j * SUBLANES, (j + 1) * SUBLANES)
                e = b2[:, seg, :] + a2[:, seg, :] * prev
                ends.append(e)
                prev = e[:, SUBLANES - 1:SUBLANES, :]
            last = prev
            cscr[s, :, 0:1, :] = cin
            cscr[s, :, 1:1 + nblk, :] = jnp.concatenate(ends, axis=1)
            spread = _spread_rows(cscr.at[s], nblk, bb)
            h = b_blk + a_blk * spread.reshape(bb * nblk, SUBLANES, LANES)
        h_parts.append(h.reshape(bb, tt, LANES))
        last_parts.append(last)
    h = jnp.concatenate(h_parts, axis=-1)
    last = jnp.concatenate(last_parts, axis=-1)
    carry[...] = last
    hst_out_ref[...] = last
    out = h * jax.nn.silu(bg_ref[...])
    o_ref[...] = out.reshape(o_ref.shape).astype(o_ref.dtype)
    scr[:, HALO - keep:HALO, :] = scr[:, HALO + tt - keep:HALO + tt, :]


def _mixer_b_operands(proj3, w_conv, b_conv, w_gates, b_rg, b_ig, lam, conv_state, h_state,
                      bb, tt, idx):
    b, t, _ = proj3.shape
    cw = CH_TILE
    nt = t // tt
    nblk = tt // SUBLANES
    blk = (bb, tt, cw)

    def spec(shape, fn):
        return pl.BlockSpec(shape, lambda *g: fn(*idx(*g)))

    vec = spec((1, cw), lambda i, c, j: (0, c))
    scratch = [pltpu.VMEM((bb, HALO + tt, cw), F32), pltpu.VMEM((bb, 1, cw), F32)]
    if nblk > 1:
        assert nblk % SUBLANES == 0
        slabs = cw // LANES
        scratch += [pltpu.VMEM((slabs, bb, tt, LANES), F32),
                    pltpu.VMEM((slabs, bb, tt, LANES), F32),
                    pltpu.VMEM((slabs, bb, HALO + nblk, LANES), F32)]
    return dict(
        args=[proj3, proj3, w_conv, b_conv.reshape(1, W_B), w_gates,
              b_rg.reshape(1, W_B), b_ig.reshape(1, W_B), lam.reshape(1, W_B),
              conv_state, h_state],
        in_specs=[
            spec(blk, lambda i, c, j: (i, j, c)),
            spec(blk, lambda i, c, j: (i, j, W_B // cw + c)),
            spec((CONV_B, cw), lambda i, c, j: (0, c)),
            vec,
            spec((1, cw, 2 * cw), lambda i, c, j: (c, 0, 0)),
            vec, vec, vec,
            spec((bb, CONV_B - 1, cw), lambda i, c, j: (i, 0, c)),
            spec((bb, 1, cw), lambda i, c, j: (i, 0, c)),
        ],
        out_specs=[
            spec((bb * tt, cw), lambda i, c, j: (i * nt + j, c)),
            spec((bb, CONV_B - 1, cw), lambda i, c, j: (i, 0, c)),
            spec((bb, 1, cw), lambda i, c, j: (i, 0, c)),
        ],
        out_shape=[
            jax.ShapeDtypeStruct((b * t, W_B), BF16),
            jax.ShapeDtypeStruct((b, CONV_B - 1, W_B), F32),
            jax.ShapeDtypeStruct((b, 1, W_B), F32),
        ],
        scratch=scratch,
        vmem=2 * (2 * _nbytes(blk, F32) + _nbytes(blk, BF16) + _nbytes((cw, 2 * cw), BF16))
        + _nbytes((bb, HALO + tt, cw), F32) + 16 * _nbytes(blk, F32),
        grid=(b // bb, W_B // cw, nt),
    )


def _mixer_b(proj3, *params, bb, tt):
    ops = _mixer_b_operands(proj3, *params, bb, tt, lambda i, c, j: (i, c, j))
    return pl.pallas_call(
        functools.partial(_mixer_b_kernel, bb=bb, tt=tt),
        grid=ops["grid"],
        in_specs=ops["in_specs"],
        out_specs=ops["out_specs"],
        out_shape=ops["out_shape"],
        scratch_shapes=ops["scratch"],
        compiler_params=_params(("arbitrary",) * 3, ops["vmem"]),
        name="mixer_b",
    )(*ops["args"])


def _fused_proj_kernel(*refs, mixers, ni):
    a_ref, w_ref = refs[:2]
    mixers = [dict(mx) for mx in mixers]
    pos = 2
    for mx in mixers:
        mx["ins"] = refs[pos:pos + mx["n_ins"]]
        pos += mx["n_ins"]
    proj_ref = refs[pos]
    pos += 1
    for mx in mixers:
        mx["outs"] = refs[pos:pos + mx["n_outs"]]
        pos += mx["n_outs"]
    wb_ref = refs[pos]
    pos += 1
    for mx in mixers:
        mx["scr"] = refs[pos:pos + mx["n_scr"]]
        pos += mx["n_scr"]
    s = pl.program_id(0)
    _cast_weight_tile(w_ref, wb_ref, lax.rem(s, ni) == 0)
    for mx in mixers:
        mx["init"](mx["ins"], mx["scr"], s)
    k_total = a_ref.shape[1]
    acc = []

    def projection_part(part, n_parts):
        kc = MXU_DIM
        per_part = k_total // (kc * n_parts)
        assert per_part * kc * n_parts == k_total
        for sub in range(part * per_part, (part + 1) * per_part):
            ks = slice(sub * kc, (sub + 1) * kc)
            term = jnp.dot(a_ref[:, ks], wb_ref[ks, :], preferred_element_type=F32)
            acc[:] = [term if not acc else acc[0] + term]
        if part == n_parts - 1:
            proj_ref[...] = acc[0]

    for mx in mixers[:-1]:
        mx["step"](mx["ins"], mx["outs"], mx["scr"], s, None)
    mx = mixers[-1]
    mx["step"](mx["ins"], mx["outs"], mx["scr"], s, projection_part)


def _fused_proj_call(name, h2d, w_in, layer, col0, ncols, mixers):
    steps = None
    for ops, _, _ in mixers:
        n = 1
        for g in ops["grid"]:
            n *= g
        assert steps in (None, n)
        steps = n
    nj = ncols // PROJ_TN
    ni = steps // nj
    tm = h2d.shape[0] // ni
    assert nj * ni == steps and tm * ni == h2d.shape[0] and tm % SUBLANES == 0
    mm = _in_proj_operands(h2d, w_in, layer, col0, ncols, tm, PROJ_TN,
                           lambda s: (s // ni, s % ni))
    specs = [dict(n_ins=len(ops["args"]), n_outs=len(ops["out_shape"]),
                  n_scr=len(ops["scratch"]), init=init, step=step)
             for ops, init, step in mixers]
    cat = lambda key: [x for ops, _, _ in mixers for x in ops[key]]
    return pl.pallas_call(
        functools.partial(_fused_proj_kernel, mixers=specs, ni=ni),
        grid=(steps,),
        in_specs=mm["in_specs"] + cat("in_specs"),
        out_specs=mm["out_specs"] + cat("out_specs"),
        out_shape=mm["out_shape"] + cat("out_shape"),
        scratch_shapes=mm["scratch"] + cat("scratch"),
        compiler_params=_params(("arbitrary",),
                                mm["vmem"] + sum(ops["vmem"] for ops, _, _ in mixers)),
        name=name,
    )(*mm["args"], *cat("args"))


def _flat_index(dims):
    def idx(s):
        out = []
        for k, d in enumerate(dims):
            stride = 1
            for e in dims[k + 1:]:
                stride *= e
            out.append((s // stride) % d if k else s // stride)
        return tuple(out)
    return idx


def _flat_mixer_b(proj3, *params, bb, tt):
    b, t, _ = proj3.shape
    dims = (b // bb, W_B // CH_TILE, t // tt)
    ops = _mixer_b_operands(proj3, *params, bb, tt, _flat_index(dims))
    nt = dims[2]

    def init(ins, scr, s):
        _mixer_b_init(ins[8], ins[9], scr[0], scr[1], lax.rem(s, nt) == 0)

    def step(ins, outs, scr, s, side_matmul):
        _mixer_b_step(*ins[:8], *outs, scr[0], scr[1], scr[2:], bb=bb, tt=tt,
                      side_matmul=side_matmul)

    return ops, init, step


def _s5_disc_kernel(lre_ref, lim_ref, ldt_ref, btre_ref, btim_ref,
                    abre_ref, abim_ref, bbre_ref, bbim_ref):
    lre = lre_ref[0]
    lim = lim_ref[0]
    dt = jnp.exp(ldt_ref[0])
    mag = jnp.exp(lre * dt)
    ab_re = mag * jnp.cos(lim * dt)
    ab_im = mag * jnp.sin(lim * dt)
    nr = ab_re - 1.0
    den = lre * lre + lim * lim
    fr = (nr * lre + ab_im * lim) / den
    fi = (ab_im * lre - nr * lim) / den
    bre = btre_ref[0]
    bim = btim_ref[0]
    bbre_ref[0] = fr * bre - fi * bim
    bbim_ref[0] = fr * bim + fi * bre
    abre_ref[0] = ab_re
    abim_ref[0] = ab_im


def _s5_discretize(lam_re, lam_im, log_dt, b_re, b_im):
    depth = lam_re.shape[0]
    gp = (depth, S5_G, 1, S5_P)
    ldt = jnp.broadcast_to(log_dt[:, :, None, None], gp)
    bt_shape = (depth, S5_G, S5_GROUP, S5_P)
    small = pl.BlockSpec((1, S5_G, 1, S5_P), lambda l: (l, 0, 0, 0))
    big = pl.BlockSpec((1, S5_G, S5_GROUP, S5_P), lambda l: (l, 0, 0, 0))
    ab_re, ab_im, bb_re, bb_im = pl.pallas_call(
        _s5_disc_kernel,
        grid=(depth,),
        in_specs=[small, small, small, big, big],
        out_specs=[small, small, big, big],
        out_shape=[jax.ShapeDtypeStruct(gp, F32), jax.ShapeDtypeStruct(gp, F32),
                   jax.ShapeDtypeStruct(bt_shape, F32), jax.ShapeDtypeStruct(bt_shape, F32)],
        name="s5_discretize",
    )(lam_re.reshape(gp), lam_im.reshape(gp), ldt,
      jnp.swapaxes(b_re, 2, 3), jnp.swapaxes(b_im, 2, 3))
    return (ab_re.reshape(depth, 1, S5_STATES), ab_im.reshape(depth, 1, S5_STATES),
            bb_re, bb_im)


def _s5_pow_kernel(ar_ref, ai_ref, tab_ref):
    def cmul(x, y):
        return x[0] * y[0] - x[1] * y[1], x[0] * y[1] + x[1] * y[0]

    def powers(b1):
        b2 = cmul(b1, b1)
        b3 = cmul(b2, b1)
        b4 = cmul(b2, b2)
        return (b1, b2, b3, b4, cmul(b4, b1), cmul(b4, b2), cmul(b4, b3), cmul(b4, b4))

    shape = (SUBLANES, S5_STATES)
    row = lax.broadcasted_iota(jnp.int32, shape, 0)
    zero = jnp.zeros(shape, F32)
    pw_a = powers((ar_ref[0], ai_ref[0]))
    pw_a8 = powers(pw_a[SUBLANES - 1])
    for base, pw in ((0, pw_a), (N_TABLES // 2, pw_a8)):
        for part in (0, 1):
            tab_ref[0, base + part] = jnp.where(row >= 1, pw[0][part], zero)
            tab_ref[0, base + 2 + part] = jnp.where(row >= 2, pw[1][part], zero)
            tab_ref[0, base + 4 + part] = jnp.where(row >= 4, pw[3][part], zero)
            p = zero
            for r, v in enumerate(pw):
                p = jnp.where(row == r, v[part], p)
            tab_ref[0, base + 6 + part] = p


def _s5_power_tables(ab_re, ab_im):
    depth = ab_re.shape[0]
    vec = pl.BlockSpec((1, 1, S5_STATES), lambda l: (l, 0, 0))
    tables = pl.pallas_call(
        _s5_pow_kernel,
        grid=(depth,),
        in_specs=[vec, vec],
        out_specs=pl.BlockSpec((1, N_TABLES, SUBLANES, S5_STATES), lambda l: (l, 0, 0, 0)),
        out_shape=jax.ShapeDtypeStruct((depth, N_TABLES, SUBLANES, S5_STATES), F32),
        name="s5_power_tables",
    )(ab_re, ab_im)
    kslab = S5_STATES // S5_KT
    tables = tables.reshape(depth, N_TABLES, SUBLANES, S5_KT, kslab)
    return jnp.transpose(tables, (0, 3, 1, 2, 4)).reshape(
        depth * S5_KT, N_TABLES, SUBLANES, kslab)


def _s5_block_matrices(bb_re, bb_im, c_re, c_im):
    depth = bb_re.shape[0]
    gpt = CH_TILE // S5_GROUP
    chan = lax.broadcasted_iota(jnp.int32, (S5_Q, CH_TILE, CH_TILE), 1)
    state = lax.broadcasted_iota(jnp.int32, (S5_Q, CH_TILE, CH_TILE), 2)
    slab = lax.broadcasted_iota(jnp.int32, (S5_Q, CH_TILE, CH_TILE), 0)
    gps = LANES // S5_P
    keep = (chan // S5_GROUP) == gps * slab + (state % LANES) // S5_P

    def dense_b(x):
        return x.reshape(depth * S5_KT, gpt * S5_GROUP, S5_P)

    db = jnp.concatenate([dense_b(bb_re)] * gps + [dense_b(bb_im)] * gps, axis=-1)
    bm = jnp.where(keep[None], db[:, None], 0.0).astype(BF16)

    def dense_c(x):
        x = x.reshape(depth * S5_KT, gpt, S5_GROUP, S5_P)
        return jnp.transpose(x, (0, 3, 1, 2)).reshape(depth * S5_KT, S5_P, gpt * S5_GROUP)

    dc = jnp.concatenate([dense_c(c_re)] * gps + [dense_c(-c_im)] * gps, axis=1)
    keep_c = jnp.swapaxes(keep, 1, 2)
    cm = jnp.where(keep_c[None], dc[:, None], 0.0).astype(BF16)
    return bm, cm


MIXER_C_INS = 10
MIXER_C_OUTS = 3


def _mixer_c_init(sre_ref, sim_ref, car_r, car_i, kt, first_time_tile):
    @pl.when(first_time_tile)
    def _():
        car_r[kt] = sre_ref[...]
        car_i[kt] = sim_ref[...]


def _mixer_c_kernel(*refs, bb, tt):
    ins = refs[:MIXER_C_INS]
    outs = refs[MIXER_C_INS:MIXER_C_INS + MIXER_C_OUTS]
    ystore, car_r, car_i, *scan_scr = refs[MIXER_C_INS + MIXER_C_OUTS:]
    kt = pl.program_id(2)
    _mixer_c_init(ins[8], ins[9], car_r, car_i, kt, pl.program_id(1) == 0)
    _mixer_c_step(*ins[:8], *outs, ystore, car_r, car_i, scan_scr, kt, bb=bb, tt=tt)


def _mixer_c_step(cu_ref, cg_ref, bm_ref, cm_ref, tab_ref, d_ref, wglu_ref, bglu_ref,
                  o_ref, sre_out_ref, sim_out_ref,
                  ystore, car_r, car_i, scan_scr, kt, *, bb, tt, side_matmul=None):
    cw = CH_TILE
    rows = bb * tt
    nblk = tt // SUBLANES
    nv = nblk // SUBLANES
    n = bb * nblk
    base8 = N_TABLES // 2

    u = cu_ref[...].reshape(rows, cw)
    ub = u.astype(BF16)
    y = d_ref[...] * u
    cr_all = car_r[kt]
    ci_all = car_i[kt]
    new_r, new_i = [], []
    for q in range(S5_Q):
        sl = slice(LANES * q, LANES * (q + 1))
        bu = jnp.dot(ub, bm_ref[kt, q], preferred_element_type=F32)
        if side_matmul is not None:
            side_matmul(q, S5_Q)
        br, bi = _block_scan_cplx(bu[:, :LANES].reshape(n, SUBLANES, LANES),
                                  bu[:, LANES:].reshape(n, SUBLANES, LANES), tab_ref, kt, 0, sl)
        p_r = tab_ref[kt, 6, :, sl]
        p_i = tab_ref[kt, 7, :, sl]
        cr = cr_all[:, :, sl]
        ci = ci_all[:, :, sl]
        if nblk == 1:
            kr, ki = cr, ci
        else:
            hsr, hsi, csr, csi = scan_scr
            hsr[...] = br.reshape(bb, tt, LANES)
            hsi[...] = bi.reshape(bb, tt, LANES)
            xr = hsr[:, pl.ds(SUBLANES - 1, nblk, stride=SUBLANES), :]
            xi = hsi[:, pl.ds(SUBLANES - 1, nblk, stride=SUBLANES), :]
            xr, xi = _block_scan_cplx(xr.reshape(bb * nv, SUBLANES, LANES),
                                      xi.reshape(bb * nv, SUBLANES, LANES), tab_ref, kt, base8,
                                      sl)
            xr = xr.reshape(bb, nblk, LANES)
            xi = xi.reshape(bb, nblk, LANES)
            q_r = tab_ref[kt, base8 + 6, :, sl]
            q_i = tab_ref[kt, base8 + 7, :, sl]
            prev_r, prev_i = cr, ci
            ends_r, ends_i = [], []
            for j in range(nv):
                seg = slice(j * SUBLANES, (j + 1) * SUBLANES)
                er = xr[:, seg, :] + (q_r * prev_r - q_i * prev_i)
                ei = xi[:, seg, :] + (q_r * prev_i + q_i * prev_r)
                ends_r.append(er)
                ends_i.append(ei)
                prev_r = er[:, SUBLANES - 1:SUBLANES, :]
                prev_i = ei[:, SUBLANES - 1:SUBLANES, :]
            csr[:, 0:1, :] = cr
            csi[:, 0:1, :] = ci
            csr[:, 1:1 + nblk, :] = jnp.concatenate(ends_r, axis=1)
            csi[:, 1:1 + nblk, :] = jnp.concatenate(ends_i, axis=1)
            kr = _spread_rows(csr, nblk, bb).reshape(n, SUBLANES, LANES)
            ki = _spread_rows(csi, nblk, bb).reshape(n, SUBLANES, LANES)
        hr = br + (p_r * kr - p_i * ki)
        hi = bi + (p_r * ki + p_i * kr)
        hr = hr.reshape(bb, tt, LANES)
        hi = hi.reshape(bb, tt, LANES)
        new_r.append(hr[:, tt - 1:tt, :])
        new_i.append(hi[:, tt - 1:tt, :])
        hcat = jnp.concatenate([hr.reshape(rows, LANES), hi.reshape(rows, LANES)], axis=-1)
        y = y + jnp.dot(hcat.astype(BF16), cm_ref[kt, q], preferred_element_type=F32)

    ncr_all = jnp.concatenate(new_r, axis=-1)
    nci_all = jnp.concatenate(new_i, axis=-1)
    car_r[kt] = ncr_all
    car_i[kt] = nci_all
    sre_out_ref[:, kt] = ncr_all
    sim_out_ref[:, kt] = nci_all
    ystore[kt] = y

    @pl.when(kt == S5_KT - 1)
    def _():
        y_all = jnp.concatenate([ystore[k] for k in range(S5_KT)], axis=-1)
        yg = jax.nn.gelu(y_all)
        z = jnp.dot(yg.astype(BF16), wglu_ref[0].astype(BF16),
                    preferred_element_type=F32) + bglu_ref[...]
        yy = yg * jax.nn.sigmoid(z)
        out = yy * jax.nn.silu(cg_ref[...].reshape(rows, W_C))
        o_ref[...] = out.astype(o_ref.dtype)


def _mixer_c_operands(proj3, bmat, cmat, tables, d_skip, w_glu, b_glu, s_re, s_im, layer,
                      bb, tt, idx):
    b, t, _ = proj3.shape
    u_col = 0
    cw = CH_TILE
    nt = t // tt
    nblk = tt // SUBLANES
    kslab = S5_Q * LANES

    def spec(shape, fn):
        return pl.BlockSpec(shape, lambda *g: fn(*idx(*g)))

    mat = spec((S5_KT, S5_Q, cw, cw), lambda i, j, k: (layer, 0, 0, 0))
    st = spec((bb, 1, kslab), lambda i, j, k: (i, 0, k))
    st_out = spec((bb, S5_KT, 1, kslab), lambda i, j, k: (i, 0, 0, 0))
    rows = bb * tt
    scratch = [pltpu.VMEM((S5_KT, rows, cw), F32),
               pltpu.VMEM((S5_KT, bb, 1, kslab), F32),
               pltpu.VMEM((S5_KT, bb, 1, kslab), F32)]
    if nblk > 1:
        assert nblk % SUBLANES == 0 and nblk // SUBLANES <= SUBLANES
        scratch += [pltpu.VMEM((bb, tt, LANES), F32), pltpu.VMEM((bb, tt, LANES), F32),
                    pltpu.VMEM((bb, HALO + nblk, LANES), F32),
                    pltpu.VMEM((bb, HALO + nblk, LANES), F32)]
    vmem = 2 * (_nbytes((bb, tt, cw), F32) + _nbytes((bb, tt, W_C), F32)
                + 2 * _nbytes((S5_KT, S5_Q, cw, cw), BF16)
                + _nbytes((S5_KT, N_TABLES, SUBLANES, kslab), F32)
                + _nbytes((W_C, W_C), F32) + _nbytes((bb, tt, W_C), BF16)
                + 4 * _nbytes((bb, SUBLANES, kslab), F32)) \
        + _nbytes((S5_KT, rows, cw), F32) + 2 * _nbytes((bb, tt, LANES), F32) \
        + 2 * _nbytes((S5_KT, bb, SUBLANES, kslab), F32) + 10 * _nbytes((rows, W_C), F32)
    return dict(
        args=[proj3, proj3, bmat, cmat, tables, d_skip.reshape(1, W_C), w_glu,
              b_glu.reshape(1, W_C), s_re, s_im],
        in_specs=[
            spec((bb, tt, cw), lambda i, j, k: (i, j, u_col // cw + k)),
            spec((bb, tt, W_C), lambda i, j, k: (i, j, u_col // W_C + 1)),
            mat, mat,
            spec((S5_KT, N_TABLES, SUBLANES, kslab), lambda i, j, k: (layer, 0, 0, 0)),
            spec((1, cw), lambda i, j, k: (0, k)),
            spec((1, W_C, W_C), lambda i, j, k: (layer, 0, 0)),
            spec((1, W_C), lambda i, j, k: (0, 0)),
            st, st,
        ],
        out_specs=[
            spec((rows, W_C), lambda i, j, k: (i * nt + j, 0)),
            st_out, st_out,
        ],
        out_shape=[
            jax.ShapeDtypeStruct((b * t, W_C), BF16),
            jax.ShapeDtypeStruct((b, S5_KT, 1, kslab), F32),
            jax.ShapeDtypeStruct((b, S5_KT, 1, kslab), F32),
        ],
        scratch=scratch,
        vmem=vmem,
        grid=(b // bb, nt, S5_KT),
    )


def _mixer_c(proj3, *params, bb, tt):
    ops = _mixer_c_operands(proj3, *params, bb, tt, lambda i, j, k: (i, j, k))
    return pl.pallas_call(
        functools.partial(_mixer_c_kernel, bb=bb, tt=tt),
        grid=ops["grid"],
        in_specs=ops["in_specs"],
        out_specs=ops["out_specs"],
        out_shape=ops["out_shape"],
        scratch_shapes=ops["scratch"],
        compiler_params=_params(("arbitrary",) * 3, ops["vmem"]),
        name="mixer_c",
    )(*ops["args"])


def _flat_mixer_c(proj3, *params, bb, tt):
    b, t, _ = proj3.shape
    dims = (b // bb, t // tt, S5_KT)
    ops = _mixer_c_operands(proj3, *params, bb, tt, _flat_index(dims))
    nt = dims[1]

    def init(ins, scr, s):
        _mixer_c_init(ins[8], ins[9], scr[1], scr[2], lax.rem(s, S5_KT),
                      lax.rem(s // S5_KT, nt) == 0)

    def step(ins, outs, scr, s, side_matmul):
        _mixer_c_step(*ins[:8], *outs, scr[0], scr[1], scr[2], scr[3:],
                      lax.rem(s, S5_KT), bb=bb, tt=tt, side_matmul=side_matmul)

    return ops, init, step


OUT_PROJ_TN = 512


def _out_proj_kernel(ma_ref, mb_ref, mc_ref, w_ref, x_ref, gate_ref, o_ref, *, bb, tt):
    w = w_ref[0].astype(BF16)
    acc = jnp.dot(ma_ref[...], w[0:W_A, :], preferred_element_type=F32)
    acc = acc + jnp.dot(mb_ref[...], w[W_A:W_A + W_B, :], preferred_element_type=F32)
    acc = acc + jnp.dot(mc_ref[...], w[W_A + W_B:, :], preferred_element_type=F32)
    o_ref[...] = x_ref[...] + gate_ref[...] * acc.reshape(bb, tt, acc.shape[-1])


def _out_projection(out_a, out_b, out_c, w_out, x, gate, layer, bb, tt):
    b, t, d = x.shape
    tm = bb * tt
    tn = OUT_PROJ_TN
    nt = t // tt
    m = b * t

    def rows(width):
        return pl.BlockSpec((tm, width), lambda i, j: (i, 0))

    vmem = 2 * (_nbytes((tm, d), BF16) + _nbytes((d, tn), F32)
                + 2 * _nbytes((tm, tn), F32)) + _nbytes((d, tn), BF16) \
        + 3 * _nbytes((tm, tn), F32)
    return pl.pallas_call(
        functools.partial(_out_proj_kernel, bb=bb, tt=tt),
        grid=(m // tm, d // tn),
        in_specs=[
            rows(W_A), rows(W_B), rows(W_C),
            pl.BlockSpec((1, d, tn), lambda i, j: (layer, 0, j)),
            pl.BlockSpec((bb, tt, tn), lambda i, j: (i // nt, i % nt, j)),
            pl.BlockSpec((bb, 1, tn), lambda i, j: (i // nt, 0, j)),
        ],
        out_specs=pl.BlockSpec((bb, tt, tn), lambda i, j: (i // nt, i % nt, j)),
        out_shape=jax.ShapeDtypeStruct((b, t, d), F32),
        compiler_params=_params(("arbitrary", "arbitrary"), vmem + (4 << 20)),
        name="out_projection",
    )(out_a, out_b, out_c, w_out, x, gate)


def _gate_weights(w_rg, w_ig):
    hpt = CH_TILE // LRU_HD
    pairs = LRU_HEADS // hpt
    w = jnp.stack([w_rg, w_ig], axis=2)
    w = w.reshape(pairs, hpt, LRU_HD, 2, LRU_HD)
    eye = jnp.eye(hpt, dtype=bool)
    out = jnp.where(eye[None, :, None, None, :, None],
                    w[:, :, :, :, None, :], 0.0)
    return out.reshape(pairs, CH_TILE, 2 * CH_TILE).astype(BF16)


def _layer(x, mod, states, wts, layer, tiles):
    b, t, d = x.shape
    conv_a, conv_b, lru_h, s5_re, s5_im = states
    shift, scale, gate = (mod[:, None, k * d:(k + 1) * d] for k in range(3))

    h = _norm_modulate(x, scale, shift, wts["g_norm"], *tiles["norm"])
    w_in = wts["w_in"]
    col_a, col_b, col_c = 0, 4 * W_A, 4 * W_A + 2 * W_B
    b_params = (wts["w_conv_b"], wts["b_conv_b"], wts["w_gates"], wts["b_rg"], wts["b_ig"],
                wts["lru_lambda"], conv_b, lru_h.reshape(b, 1, W_B))
    c_params = (wts["bmat"], wts["cmat"], wts["tables"], wts["s5_d"], wts["w_glu"],
                wts["b_glu"], s5_re.reshape(b, 1, S5_STATES), s5_im.reshape(b, 1, S5_STATES),
                layer)
    bb_b, tt_b = tiles["b"]
    bb_c, tt_c = tiles["c"]

    proj_c = _in_projection(h, w_in, layer, col_c, 2 * W_C).reshape(b, t, 2 * W_C)
    if tiles["fuse"]:
        proj_b, out_c, re_new, im_new = _fused_proj_call(
            "proj_b_mixer_c", h, w_in, layer, col_b, 2 * W_B,
            [_flat_mixer_c(proj_c, *c_params, bb=bb_c, tt=tt_c)])
        proj_b = proj_b.reshape(b, t, 2 * W_B)
        proj_a, out_b, conv_b_new, lru_new = _fused_proj_call(
            "proj_a_mixer_b", h, w_in, layer, col_a, 4 * W_A,
            [_flat_mixer_b(proj_b, *b_params, bb=bb_b, tt=tt_b)])
    else:
        out_c, re_new, im_new = _mixer_c(proj_c, *c_params, bb=bb_c, tt=tt_c)
        proj_b = _in_projection(h, w_in, layer, col_b, 2 * W_B).reshape(b, t, 2 * W_B)
        out_b, conv_b_new, lru_new = _mixer_b(proj_b, *b_params, bb=bb_b, tt=tt_b)
        proj_a = _in_projection(h, w_in, layer, col_a, 4 * W_A)
    proj_a = proj_a.reshape(b, t, 4 * W_A)
    out_a, conv_a_new = _mixer_a(proj_a, wts["w_conv_a"], wts["b_conv_a"], conv_a,
                                 *tiles["a"])

    x_new = _out_projection(out_a, out_b, out_c, wts["w_out"], x, gate, layer, *tiles["out"])
    return (x_new, conv_a_new, conv_b_new, lru_new.reshape(b, W_B),
            re_new.reshape(b, S5_G, S5_P), im_new.reshape(b, S5_G, S5_P))


PROMPT_TILES = {"norm": (1, 512), "a": (1, 2048), "b": (1, 1024), "c": (1, 256),
                "out": (1, 1024), "final": (1, 512), "fuse": True}
SAMPLE_TILES = {"norm": (64, 8), "a": (128, 8), "b": (128, 8), "c": (64, 8),
                "out": (128, 8), "final": (64, 8), "fuse": False}


def kernel(x_prompt, x_sample, c_prompt, c_sample, state_conv_a, state_conv_b, state_lru_h,
           state_s5_re, state_s5_im, g_norm, w_ada, b_ada, w_in, w_conv_a, b_conv_a,
           w_conv_b, b_conv_b, w_rg, b_rg, w_ig, b_ig, lru_lambda, s5_lambda_re,
           s5_lambda_im, s5_log_dt, s5_b_re, s5_b_im, s5_c_re, s5_c_im, s5_d, w_glu,
           b_glu, w_out, g_final):
    depth = w_in.shape[0]
    bp = x_prompt.shape[0]
    bs = x_sample.shape[0]

    c_all = jnp.concatenate([c_prompt, c_sample], axis=0)
    pad = (-c_all.shape[0]) % SUBLANES
    c_all = jnp.pad(c_all, ((0, pad), (0, 0)))
    mod = _modulation(c_all, w_ada, b_ada)

    ab_re, ab_im, bb_re, bb_im = _s5_discretize(s5_lambda_re, s5_lambda_im, s5_log_dt,
                                               s5_b_re, s5_b_im)
    tables = _s5_power_tables(ab_re, ab_im)
    bmat, cmat = _s5_block_matrices(bb_re, bb_im, s5_c_re, s5_c_im)
    w_glu_bf16 = w_glu.astype(BF16)

    xp, xs = x_prompt, x_sample
    zeros_p = (jnp.zeros((bp, CONV_A - 1, W_A), F32), jnp.zeros((bp, CONV_B - 1, W_B), F32),
               jnp.zeros((bp, W_B), F32), jnp.zeros((bp, S5_G, S5_P), F32),
               jnp.zeros((bp, S5_G, S5_P), F32))
    outs_p, outs_s = [], []
    for l in range(depth):
        wts = {
            "g_norm": g_norm[l], "w_in": w_in,
            "w_conv_a": w_conv_a[l], "b_conv_a": b_conv_a[l],
            "w_conv_b": w_conv_b[l], "b_conv_b": b_conv_b[l],
            "w_gates": _gate_weights(w_rg[l], w_ig[l]), "b_rg": b_rg[l], "b_ig": b_ig[l],
            "lru_lambda": lru_lambda[l], "bmat": bmat, "cmat": cmat,
            "tables": tables, "s5_d": s5_d[l],
            "w_glu": w_glu_bf16, "b_glu": b_glu[l], "w_out": w_out,
        }
        res_p = _layer(xp, mod[l, :bp], zeros_p, wts, l, PROMPT_TILES)
        xp = res_p[0]
        outs_p.append(res_p[1:])
        st_s = (state_conv_a[l], state_conv_b[l], state_lru_h[l], state_s5_re[l],
                state_s5_im[l])
        res_s = _layer(xs, mod[l, bp:bp + bs], st_s, wts, l, SAMPLE_TILES)
        xs = res_s[0]
        outs_s.append(res_s[1:])

    y_prompt = _final_norm(xp, g_final, *PROMPT_TILES["final"])
    y_sample = _final_norm(xs, g_final, *SAMPLE_TILES["final"])
    stack = lambda outs, k: jnp.stack([o[k] for o in outs])
    return (y_prompt, y_sample,
            *(stack(outs_p, k) for k in range(5)),
            *(stack(outs_s, k) for k in range(5)))
```

```python
import functools

import jax
import jax.numpy as jnp
from jax import lax
from jax.experimental import pallas as pl
from jax.experimental.pallas import tpu as pltpu

F32 = jnp.float32
BF16 = jnp.bfloat16

D_MODEL = 4096
W_A = D_MODEL // 4
W_B = D_MODEL // 2
W_C = D_MODEL // 4
IN_COLS = 4 * W_A + 2 * W_B + 2 * W_C
CONV_A = 3
CONV_B = 4
LRU_HEADS = 16
LRU_HD = W_B // LRU_HEADS
LRU_C = 8.0
S5_GROUP = 16
S5_G = W_C // S5_GROUP
S5_P = 64
S5_STATES = S5_G * S5_P
EPS = 1e-6

SUBLANES = 8
LANES = 128
MXU_DIM = 256
VMEM_LIMIT_CAP = 60000 * 1024
CH_TILE = MXU_DIM
S5_KT = W_C // CH_TILE
S5_Q = (CH_TILE // S5_GROUP) * S5_P // LANES
HALO = SUBLANES
N_TABLES = 16
GATE_CHUNK_ROWS = 128


def _params(sem, vmem_bytes):
    return pltpu.CompilerParams(
        dimension_semantics=sem,
        vmem_limit_bytes=int(min(VMEM_LIMIT_CAP, vmem_bytes)))


def _nbytes(shape, dtype):
    n = 1
    for s in shape:
        n *= s
    return n * jnp.dtype(dtype).itemsize


def _mod_kernel(c_ref, w_ref, b_ref, o_ref):
    c = c_ref[...].astype(BF16)
    w = w_ref[0].astype(BF16)
    o_ref[0] = jnp.dot(c, w, preferred_element_type=F32) + b_ref[0]


def _modulation(c_all, w_ada, b_ada):
    depth, d, n = w_ada.shape
    rows = c_all.shape[0]
    tn = 512
    vmem = 2 * (_nbytes((rows, d), F32) + _nbytes((d, tn), F32)
                + _nbytes((rows, tn), F32)) + _nbytes((d, tn), F32)
    return pl.pallas_call(
        _mod_kernel,
        grid=(depth, n // tn),
        in_specs=[
            pl.BlockSpec((rows, d), lambda l, j: (0, 0)),
            pl.BlockSpec((1, d, tn), lambda l, j: (l, 0, j)),
            pl.BlockSpec((1, 1, tn), lambda l, j: (l, 0, j)),
        ],
        out_specs=pl.BlockSpec((1, rows, tn), lambda l, j: (l, 0, j)),
        out_shape=jax.ShapeDtypeStruct((depth, rows, n), F32),
        compiler_params=_params(("arbitrary", "arbitrary"), vmem + (8 << 20)),
        name="adaln_modulation",
    )(c_all, w_ada, b_ada.reshape(depth, 1, n))


def _norm_mod_kernel(x_ref, scale_ref, shift_ref, g_ref, o_ref):
    x = x_ref[...]
    ms = jnp.mean(x * x, axis=-1, keepdims=True)
    y = (x * lax.rsqrt(ms + EPS)) * g_ref[...]
    h = y * (1.0 + scale_ref[...]) + shift_ref[...]
    o_ref[...] = h.reshape(o_ref.shape).astype(o_ref.dtype)


def _norm_modulate(x, scale, shift, g, bb, tt):
    b, t, d = x.shape
    nt = t // tt
    vmem = 2 * (_nbytes((bb, tt, d), F32) + _nbytes((bb, tt, d), BF16)) \
        + 4 * _nbytes((bb, tt, d), F32)
    return pl.pallas_call(
        _norm_mod_kernel,
        grid=(b // bb, nt),
        in_specs=[
            pl.BlockSpec((bb, tt, d), lambda i, j: (i, j, 0)),
            pl.BlockSpec((bb, 1, d), lambda i, j: (i, 0, 0)),
            pl.BlockSpec((bb, 1, d), lambda i, j: (i, 0, 0)),
            pl.BlockSpec((1, 1, d), lambda i, j: (0, 0, 0)),
        ],
        out_specs=pl.BlockSpec((bb * tt, d), lambda i, j: (i * nt + j, 0)),
        out_shape=jax.ShapeDtypeStruct((b * t, d), BF16),
        compiler_params=_params(("arbitrary", "arbitrary"), vmem),
        name="norm_modulate",
    )(x, scale, shift, g.reshape(1, 1, d))


def _final_norm_kernel(x_ref, g_ref, o_ref):
    x = x_ref[...]
    ms = jnp.mean(x * x, axis=-1, keepdims=True)
    o_ref[...] = (x * lax.rsqrt(ms + EPS)) * g_ref[...]


def _final_norm(x, g, bb, tt):
    b, t, d = x.shape
    vmem = 8 * _nbytes((bb, tt, d), F32)
    return pl.pallas_call(
        _final_norm_kernel,
        grid=(b // bb, t // tt),
        in_specs=[
            pl.BlockSpec((bb, tt, d), lambda i, j: (i, j, 0)),
            pl.BlockSpec((1, 1, d), lambda i, j: (0, 0, 0)),
        ],
        out_specs=pl.BlockSpec((bb, tt, d), lambda i, j: (i, j, 0)),
        out_shape=jax.ShapeDtypeStruct((b, t, d), F32),
        compiler_params=_params(("arbitrary", "arbitrary"), vmem),
        name="final_norm",
    )(x, g.reshape(1, 1, d))


PROJ_TN = 512
PROJ_ALONE_TILE = (1024, 512)


def _cast_weight_tile(w_ref, wb_ref, first_row_tile):
    @pl.when(first_row_tile)
    def _():
        wb_ref[...] = w_ref[0].astype(BF16)


def _in_proj_kernel(a_ref, w_ref, o_ref, wb_ref):
    _cast_weight_tile(w_ref, wb_ref, pl.program_id(1) == 0)
    o_ref[...] = jnp.dot(a_ref[...], wb_ref[...], preferred_element_type=F32)


def _in_proj_operands(h2d, w_in, layer, col0, ncols, tm, tn, idx):
    m, k = h2d.shape
    assert col0 % tn == 0 and ncols % tn == 0 and m % tm == 0
    return dict(
        args=[h2d, w_in],
        in_specs=[
            pl.BlockSpec((tm, k), lambda *g: (idx(*g)[1], 0)),
            pl.BlockSpec((1, k, tn), lambda *g: (layer, 0, col0 // tn + idx(*g)[0])),
        ],
        out_specs=[pl.BlockSpec((tm, tn), lambda *g: (idx(*g)[1], idx(*g)[0]))],
        out_shape=[jax.ShapeDtypeStruct((m, ncols), F32)],
        scratch=[pltpu.VMEM((k, tn), BF16)],
        vmem=2 * (_nbytes((tm, k), BF16) + _nbytes((k, tn), F32) + _nbytes((tm, tn), F32))
        + _nbytes((k, tn), BF16) + _nbytes((tm, tn), F32),
    )


def _in_projection(h2d, w_in, layer, col0, ncols):
    m = h2d.shape[0]
    tm, tn = PROJ_ALONE_TILE
    ops = _in_proj_operands(h2d, w_in, layer, col0, ncols, tm, tn, lambda j, i: (j, i))
    return pl.pallas_call(
        _in_proj_kernel,
        grid=(ncols // tn, m // tm),
        in_specs=ops["in_specs"],
        out_specs=ops["out_specs"][0],
        out_shape=ops["out_shape"][0],
        scratch_shapes=ops["scratch"],
        compiler_params=_params(("arbitrary", "arbitrary"), ops["vmem"] + (4 << 20)),
        name="in_projection",
    )(*ops["args"])


def _causal_taps(scr, v, w_ref, taps, tt):
    scr[:, HALO:HALO + tt, :] = v
    acc = None
    for k in range(taps):
        src = v if k == taps - 1 else scr[:, HALO - (taps - 1) + k:HALO - (taps - 1) + k + tt, :]
        term = w_ref[k:k + 1, :] * src
        acc = term if acc is None else acc + term
    return acc


def _sigmoid(x):
    return 0.5 * jnp.tanh(0.5 * x) + 0.5


def _silu(x):
    return x * _sigmoid(x)


def _block_scan_real(a, b):
    row = lax.broadcasted_iota(jnp.int32, (1, SUBLANES, a.shape[-1]), 1)
    for d in (1, 2, 4):
        keep = row >= d
        a_sh = pltpu.roll(a, d, 1)
        b_sh = pltpu.roll(b, d, 1)
        b = b + a * jnp.where(keep, b_sh, 0.0)
        a = a * jnp.where(keep, a_sh, 1.0)
    return a, b


def _block_scan_cplx(br, bi, tab_ref, kt, base, sl):
    for lvl, d in enumerate((1, 2, 4)):
        lr = tab_ref[kt, base + 2 * lvl, :, sl]
        li = tab_ref[kt, base + 2 * lvl + 1, :, sl]
        sr = pltpu.roll(br, d, 1)
        si = pltpu.roll(bi, d, 1)
        br, bi = br + (lr * sr - li * si), bi + (lr * si + li * sr)
    return br, bi


def _spread_rows(scr, nblk, bb):
    return jnp.concatenate(
        [jnp.broadcast_to(scr[:, k:k + 1, :], (bb, SUBLANES, LANES)) for k in range(nblk)],
        axis=1)


def _mixer_a_kernel(ab_ref, ac_ref, ax_ref, ag_ref, w_ref, b_ref, st_ref,
                    o_ref, st_out_ref, scr, *, tt):
    keep = CONV_A - 1

    @pl.when(pl.program_id(2) == 0)
    def _():
        scr[:, HALO - keep:HALO, :] = st_ref[...]

    conv_in = ac_ref[...] * ax_ref[...]
    y = b_ref[...] + _causal_taps(scr, conv_in, w_ref, CONV_A, tt)
    out = (ab_ref[...] * y) * _silu(ag_ref[...])
    o_ref[...] = out.reshape(o_ref.shape).astype(o_ref.dtype)
    tail = scr[:, HALO + tt - keep:HALO + tt, :]
    st_out_ref[...] = tail
    scr[:, HALO - keep:HALO, :] = tail


def _mixer_a(proj3, w_conv, b_conv, state, bb, tt):
    b, t, _ = proj3.shape
    cw = CH_TILE
    nc = W_A // cw
    nt = t // tt
    blk = (bb, tt, cw)

    def col(off):
        return pl.BlockSpec(blk, lambda i, c, j, off=off: (i, j, off // cw + c))

    vmem = 2 * (4 * _nbytes(blk, F32) + _nbytes(blk, BF16)) \
        + _nbytes((bb, HALO + tt, cw), F32) + 6 * _nbytes(blk, F32)
    return pl.pallas_call(
        functools.partial(_mixer_a_kernel, tt=tt),
        grid=(b // bb, nc, nt),
        in_specs=[
            col(0), col(W_A), col(2 * W_A), col(3 * W_A),
            pl.BlockSpec((CONV_A, cw), lambda i, c, j: (0, c)),
            pl.BlockSpec((1, cw), lambda i, c, j: (0, c)),
            pl.BlockSpec((bb, CONV_A - 1, cw), lambda i, c, j: (i, 0, c)),
        ],
        out_specs=[
            pl.BlockSpec((bb * tt, cw), lambda i, c, j: (i * nt + j, c)),
            pl.BlockSpec((bb, CONV_A - 1, cw), lambda i, c, j: (i, 0, c)),
        ],
        out_shape=[
            jax.ShapeDtypeStruct((b * t, W_A), BF16),
            jax.ShapeDtypeStruct((b, CONV_A - 1, W_A), F32),
        ],
        scratch_shapes=[pltpu.VMEM((bb, HALO + tt, cw), F32)],
        compiler_params=_params(("arbitrary",) * 3, vmem),
        name="mixer_a",
    )(proj3, proj3, proj3, proj3, w_conv, b_conv.reshape(1, W_A), state)


MIXER_B_INS = 10
MIXER_B_OUTS = 3


def _mixer_b_init(cst_ref, hst_ref, scr, carry, first_time_tile):
    @pl.when(first_time_tile)
    def _():
        scr[:, HALO - (CONV_B - 1):HALO, :] = cst_ref[...]
        carry[...] = hst_ref[...]


def _mixer_b_kernel(*refs, bb, tt):
    ins = refs[:MIXER_B_INS]
    outs = refs[MIXER_B_INS:MIXER_B_INS + MIXER_B_OUTS]
    scr, carry, *scan_scr = refs[MIXER_B_INS + MIXER_B_OUTS:]
    _mixer_b_init(ins[8], ins[9], scr, carry, pl.program_id(2) == 0)
    _mixer_b_step(*ins[:8], *outs, scr, carry, scan_scr, bb=bb, tt=tt)


def _mixer_b_step(bx_ref, bg_ref, w_ref, b_ref, wg_ref, brg_ref, big_ref, lam_ref,
                  o_ref, cst_out_ref, hst_out_ref, scr, carry, scan_scr, *, bb, tt,
                  side_matmul=None):
    keep = CONV_B - 1
    cw = CH_TILE
    rows = bb * tt
    nblk = tt // SUBLANES
    nv = nblk // SUBLANES

    scr[:, HALO:HALO + tt, :] = bx_ref[...]
    cst_out_ref[...] = scr[:, HALO + tt - keep:HALO + tt, :]
    softplus_neg_lam = jax.nn.softplus(-lam_ref[...])
    cin_all = carry[...]
    h_parts, last_parts = [], []
    assert LRU_HD == LANES
    n_heads = cw // LRU_HD
    n_chunks = 1 if side_matmul is None else rows // GATE_CHUNK_ROWS
    rc = rows // n_chunks
    for s in range(n_heads):
        sl = slice(s * LANES, (s + 1) * LANES)
        w_head = jnp.concatenate([wg_ref[0, sl, sl], wg_ref[0, sl, cw + s * LANES:cw + (s + 1) * LANES]],
                                 axis=1)
        a_chunks, b_chunks = [], []
        for k in range(n_chunks):
            acc = None
            for j in range(CONV_B):
                lo = HALO - keep + j
                if tt >= rc:
                    b0, t0 = divmod(k * rc, tt)
                    win = scr[b0:b0 + 1, lo + t0:lo + t0 + rc, sl]
                else:
                    nb = rc // tt
                    win = scr[k * nb:(k + 1) * nb, lo:lo + tt, sl]
                term = w_ref[j:j + 1, sl] * win.reshape(rc, LANES)
                acc = term if acc is None else acc + term
            xb = b_ref[:, sl] + acc
            gates = jnp.dot(xb.astype(BF16), w_head, preferred_element_type=F32)
            if side_matmul is not None:
                side_matmul(s * n_chunks + k, n_heads * n_chunks)
            r = jax.nn.sigmoid(gates[:, :LANES] + brg_ref[:, sl])
            ig = jax.nn.sigmoid(gates[:, LANES:] + big_ref[:, sl])
            log_a = (-LRU_C * r) * softplus_neg_lam[:, sl]
            a = jnp.exp(log_a)
            beta = jnp.sqrt(-jnp.tanh(log_a) * (a * a + 1.0))
            bt = (beta * ig) * xb
            a_k, b_k = _block_scan_real(a.reshape(rc // SUBLANES, SUBLANES, LANES),
                                        bt.reshape(rc // SUBLANES, SUBLANES, LANES))
            a_chunks.append(a_k)
            b_chunks.append(b_k)
        a_blk = jnp.concatenate(a_chunks, axis=0)
        b_blk = jnp.concatenate(b_chunks, axis=0)
        cin = cin_all[:, :, sl]
        if nblk == 1:
            h = b_blk + a_blk * cin
            last = h[:, SUBLANES - 1:SUBLANES, :]
        else:
            acum, bcum, cscr = scan_scr
            acum[s] = a_blk.reshape(bb, tt, LANES)
            bcum[s] = b_blk.reshape(bb, tt, LANES)
            a2 = acum[s, :, pl.ds(SUBLANES - 1, nblk, stride=SUBLANES), :]
            b2 = bcum[s, :, pl.ds(SUBLANES - 1, nblk, stride=SUBLANES), :]
            a2, b2 = _block_scan_real(a2.reshape(bb * nv, SUBLANES, LANES),
                                      b2.reshape(bb * nv, SUBLANES, LANES))
            a2 = a2.reshape(bb, nblk, LANES)
            b2 = b2.reshape(bb, nblk, LANES)
            prev = cin
            ends = []
            for j in range(nv):
                seg = slice(j * SUBLANES, (j + 1) * SUBLANES)
                e = b2[:, seg, :] + a2[:, seg, :] * prev
                ends.append(e)
                prev = e[:, SUBLANES - 1:SUBLANES, :]
            last = prev
            cscr[s, :, 0:1, :] = cin
            cscr[s, :, 1:1 + nblk, :] = jnp.concatenate(ends, axis=1)
            spread = _spread_rows(cscr.at[s], nblk, bb)
            h = b_blk + a_blk * spread.reshape(bb * nblk, SUBLANES, LANES)
        h_parts.append(h.reshape(bb, tt, LANES))
        last_parts.append(last)
    h = jnp.concatenate(h_parts, axis=-1)
    last = jnp.concatenate(last_parts, axis=-1)
    carry[...] = last
    hst_out_ref[...] = last
    out = h * jax.nn.silu(bg_ref[...])
    o_ref[...] = out.reshape(o_ref.shape).astype(o_ref.dtype)
    scr[:, HALO - keep:HALO, :] = scr[:, HALO + tt - keep:HALO + tt, :]


def _mixer_b_operands(proj3, w_conv, b_conv, w_gates, b_rg, b_ig, lam, conv_state, h_state,
                      bb, tt, idx):
    b, t, _ = proj3.shape
    cw = CH_TILE
    nt = t // tt
    nblk = tt // SUBLANES
    blk = (bb, tt, cw)

    def spec(shape, fn):
        return pl.BlockSpec(shape, lambda *g: fn(*idx(*g)))

    vec = spec((1, cw), lambda i, c, j: (0, c))
    scratch = [pltpu.VMEM((bb, HALO + tt, cw), F32), pltpu.VMEM((bb, 1, cw), F32)]
    if nblk > 1:
        assert nblk % SUBLANES == 0
        slabs = cw // LANES
        scratch += [pltpu.VMEM((slabs, bb, tt, LANES), F32),
                    pltpu.VMEM((slabs, bb, tt, LANES), F32),
                    pltpu.VMEM((slabs, bb, HALO + nblk, LANES), F32)]
    return dict(
        args=[proj3, proj3, w_conv, b_conv.reshape(1, W_B), w_gates,
              b_rg.reshape(1, W_B), b_ig.reshape(1, W_B), lam.reshape(1, W_B),
              conv_state, h_state],
        in_specs=[
            spec(blk, lambda i, c, j: (i, j, c)),
            spec(blk, lambda i, c, j: (i, j, W_B // cw + c)),
            spec((CONV_B, cw), lambda i, c, j: (0, c)),
            vec,
            spec((1, cw, 2 * cw), lambda i, c, j: (c, 0, 0)),
            vec, vec, vec,
            spec((bb, CONV_B - 1, cw), lambda i, c, j: (i, 0, c)),
            spec((bb, 1, cw), lambda i, c, j: (i, 0, c)),
        ],
        out_specs=[
            spec((bb * tt, cw), lambda i, c, j: (i * nt + j, c)),
            spec((bb, CONV_B - 1, cw), lambda i, c, j: (i, 0, c)),
            spec((bb, 1, cw), lambda i, c, j: (i, 0, c)),
        ],
        out_shape=[
            jax.ShapeDtypeStruct((b * t, W_B), BF16),
            jax.ShapeDtypeStruct((b, CONV_B - 1, W_B), F32),
            jax.ShapeDtypeStruct((b, 1, W_B), F32),
        ],
        scratch=scratch,
        vmem=2 * (2 * _nbytes(blk, F32) + _nbytes(blk, BF16) + _nbytes((cw, 2 * cw), BF16))
        + _nbytes((bb, HALO + tt, cw), F32) + 16 * _nbytes(blk, F32),
        grid=(b // bb, W_B // cw, nt),
    )


def _mixer_b(proj3, *params, bb, tt):
    ops = _mixer_b_operands(proj3, *params, bb, tt, lambda i, c, j: (i, c, j))
    return pl.pallas_call(
        functools.partial(_mixer_b_kernel, bb=bb, tt=tt),
        grid=ops["grid"],
        in_specs=ops["in_specs"],
        out_specs=ops["out_specs"],
        out_shape=ops["out_shape"],
        scratch_shapes=ops["scratch"],
        compiler_params=_params(("arbitrary",) * 3, ops["vmem"]),
        name="mixer_b",
    )(*ops["args"])


def _fused_proj_kernel(*refs, mixers, ni):
    a_ref, w_ref = refs[:2]
    mixers = [dict(mx) for mx in mixers]
    pos = 2
    for mx in mixers:
        mx["ins"] = refs[pos:pos + mx["n_ins"]]
        pos += mx["n_ins"]
    proj_ref = refs[pos]
    pos += 1
    for mx in mixers:
        mx["outs"] = refs[pos:pos + mx["n_outs"]]
        pos += mx["n_outs"]
    wb_ref = refs[pos]
    pos += 1
    for mx in mixers:
        mx["scr"] = refs[pos:pos + mx["n_scr"]]
        pos += mx["n_scr"]
    s = pl.program_id(0)
    _cast_weight_tile(w_ref, wb_ref, lax.rem(s, ni) == 0)
    for mx in mixers:
        mx["init"](mx["ins"], mx["scr"], s)
    k_total = a_ref.shape[1]
    acc = []

    def projection_part(part, n_parts):
        kc = MXU_DIM
        per_part = k_total // (kc * n_parts)
        assert per_part * kc * n_parts == k_total
        for sub in range(part * per_part, (part + 1) * per_part):
            ks = slice(sub * kc, (sub + 1) * kc)
            term = jnp.dot(a_ref[:, ks], wb_ref[ks, :], preferred_element_type=F32)
            acc[:] = [term if not acc else acc[0] + term]
        if part == n_parts - 1:
            proj_ref[...] = acc[0]

    for mx in mixers[:-1]:
        mx["step"](mx["ins"], mx["outs"], mx["scr"], s, None)
    mx = mixers[-1]
    mx["step"](mx["ins"], mx["outs"], mx["scr"], s, projection_part)


def _fused_proj_call(name, h2d, w_in, layer, col0, ncols, mixers):
    steps = None
    for ops, _, _ in mixers:
        n = 1
        for g in ops["grid"]:
            n *= g
        assert steps in (None, n)
        steps = n
    nj = ncols // PROJ_TN
    ni = steps // nj
    tm = h2d.shape[0] // ni
    assert nj * ni == steps and tm * ni == h2d.shape[0] and tm % SUBLANES == 0
    mm = _in_proj_operands(h2d, w_in, layer, col0, ncols, tm, PROJ_TN,
                           lambda s: (s // ni, s % ni))
    specs = [dict(n_ins=len(ops["args"]), n_outs=len(ops["out_shape"]),
                  n_scr=len(ops["scratch"]), init=init, step=step)
             for ops, init, step in mixers]
    cat = lambda key: [x for ops, _, _ in mixers for x in ops[key]]
    return pl.pallas_call(
        functools.partial(_fused_proj_kernel, mixers=specs, ni=ni),
        grid=(steps,),
        in_specs=mm["in_specs"] + cat("in_specs"),
        out_specs=mm["out_specs"] + cat("out_specs"),
        out_shape=mm["out_shape"] + cat("out_shape"),
        scratch_shapes=mm["scratch"] + cat("scratch"),
        compiler_params=_params(("arbitrary",),
                                mm["vmem"] + sum(ops["vmem"] for ops, _, _ in mixers)),
        name=name,
    )(*mm["args"], *cat("args"))


def _flat_index(dims):
    def idx(s):
        out = []
        for k, d in enumerate(dims):
            stride = 1
            for e in dims[k + 1:]:
                stride *= e
            out.append((s // stride) % d if k else s // stride)
        return tuple(out)
    return idx


def _flat_mixer_b(proj3, *params, bb, tt):
    b, t, _ = proj3.shape
    dims = (b // bb, W_B // CH_TILE, t // tt)
    ops = _mixer_b_operands(proj3, *params, bb, tt, _flat_index(dims))
    nt = dims[2]

    def init(ins, scr, s):
        _mixer_b_init(ins[8], ins[9], scr[0], scr[1], lax.rem(s, nt) == 0)

    def step(ins, outs, scr, s, side_matmul):
        _mixer_b_step(*ins[:8], *outs, scr[0], scr[1], scr[2:], bb=bb, tt=tt,
                      side_matmul=side_matmul)

    return ops, init, step


def _s5_disc_kernel(lre_ref, lim_ref, ldt_ref, btre_ref, btim_ref,
                    abre_ref, abim_ref, bbre_ref, bbim_ref):
    lre = lre_ref[0]
    lim = lim_ref[0]
    dt = jnp.exp(ldt_ref[0])
    mag = jnp.exp(lre * dt)
    ab_re = mag * jnp.cos(lim * dt)
    ab_im = mag * jnp.sin(lim * dt)
    nr = ab_re - 1.0
    den = lre * lre + lim * lim
    fr = (nr * lre + ab_im * lim) / den
    fi = (ab_im * lre - nr * lim) / den
    bre = btre_ref[0]
    bim = btim_ref[0]
    bbre_ref[0] = fr * bre - fi * bim
    bbim_ref[0] = fr * bim + fi * bre
    abre_ref[0] = ab_re
    abim_ref[0] = ab_im


def _s5_discretize(lam_re, lam_im, log_dt, b_re, b_im):
    depth = lam_re.shape[0]
    gp = (depth, S5_G, 1, S5_P)
    ldt = jnp.broadcast_to(log_dt[:, :, None, None], gp)
    bt_shape = (depth, S5_G, S5_GROUP, S5_P)
    small = pl.BlockSpec((1, S5_G, 1, S5_P), lambda l: (l, 0, 0, 0))
    big = pl.BlockSpec((1, S5_G, S5_GROUP, S5_P), lambda l: (l, 0, 0, 0))
    ab_re, ab_im, bb_re, bb_im = pl.pallas_call(
        _s5_disc_kernel,
        grid=(depth,),
        in_specs=[small, small, small, big, big],
        out_specs=[small, small, big, big],
        out_shape=[jax.ShapeDtypeStruct(gp, F32), jax.ShapeDtypeStruct(gp, F32),
                   jax.ShapeDtypeStruct(bt_shape, F32), jax.ShapeDtypeStruct(bt_shape, F32)],
        name="s5_discretize",
    )(lam_re.reshape(gp), lam_im.reshape(gp), ldt,
      jnp.swapaxes(b_re, 2, 3), jnp.swapaxes(b_im, 2, 3))
    return (ab_re.reshape(depth, 1, S5_STATES), ab_im.reshape(depth, 1, S5_STATES),
            bb_re, bb_im)


def _s5_pow_kernel(ar_ref, ai_ref, tab_ref):
    def cmul(x, y):
        return x[0] * y[0] - x[1] * y[1], x[0] * y[1] + x[1] * y[0]

    def powers(b1):
        b2 = cmul(b1, b1)
        b3 = cmul(b2, b1)
        b4 = cmul(b2, b2)
        return (b1, b2, b3, b4, cmul(b4, b1), cmul(b4, b2), cmul(b4, b3), cmul(b4, b4))

    shape = (SUBLANES, S5_STATES)
    row = lax.broadcasted_iota(jnp.int32, shape, 0)
    zero = jnp.zeros(shape, F32)
    pw_a = powers((ar_ref[0], ai_ref[0]))
    pw_a8 = powers(pw_a[SUBLANES - 1])
    for base, pw in ((0, pw_a), (N_TABLES // 2, pw_a8)):
        for part in (0, 1):
            tab_ref[0, base + part] = jnp.where(row >= 1, pw[0][part], zero)
            tab_ref[0, base + 2 + part] = jnp.where(row >= 2, pw[1][part], zero)
            tab_ref[0, base + 4 + part] = jnp.where(row >= 4, pw[3][part], zero)
            p = zero
            for r, v in enumerate(pw):
                p = jnp.where(row == r, v[part], p)
            tab_ref[0, base + 6 + part] = p


def _s5_power_tables(ab_re, ab_im):
    depth = ab_re.shape[0]
    vec = pl.BlockSpec((1, 1, S5_STATES), lambda l: (l, 0, 0))
    tables = pl.pallas_call(
        _s5_pow_kernel,
        grid=(depth,),
        in_specs=[vec, vec],
        out_specs=pl.BlockSpec((1, N_TABLES, SUBLANES, S5_STATES), lambda l: (l, 0, 0, 0)),
        out_shape=jax.ShapeDtypeStruct((depth, N_TABLES, SUBLANES, S5_STATES), F32),
        name="s5_power_tables",
    )(ab_re, ab_im)
    kslab = S5_STATES // S5_KT
    tables = tables.reshape(depth, N_TABLES, SUBLANES, S5_KT, kslab)
    return jnp.transpose(tables, (0, 3, 1, 2, 4)).reshape(
        depth * S5_KT, N_TABLES, SUBLANES, kslab)


def _s5_block_matrices(bb_re, bb_im, c_re, c_im):
    depth = bb_re.shape[0]
    gpt = CH_TILE // S5_GROUP
    chan = lax.broadcasted_iota(jnp.int32, (S5_Q, CH_TILE, CH_TILE), 1)
    state = lax.broadcasted_iota(jnp.int32, (S5_Q, CH_TILE, CH_TILE), 2)
    slab = lax.broadcasted_iota(jnp.int32, (S5_Q, CH_TILE, CH_TILE), 0)
    gps = LANES // S5_P
    keep = (chan // S5_GROUP) == gps * slab + (state % LANES) // S5_P

    def dense_b(x):
        return x.reshape(depth * S5_KT, gpt * S5_GROUP, S5_P)

    db = jnp.concatenate([dense_b(bb_re)] * gps + [dense_b(bb_im)] * gps, axis=-1)
    bm = jnp.where(keep[None], db[:, None], 0.0).astype(BF16)

    def dense_c(x):
        x = x.reshape(depth * S5_KT, gpt, S5_GROUP, S5_P)
        return jnp.transpose(x, (0, 3, 1, 2)).reshape(depth * S5_KT, S5_P, gpt * S5_GROUP)

    dc = jnp.concatenate([dense_c(c_re)] * gps + [dense_c(-c_im)] * gps, axis=1)
    keep_c = jnp.swapaxes(keep, 1, 2)
    cm = jnp.where(keep_c[None], dc[:, None], 0.0).astype(BF16)
    return bm, cm


MIXER_C_INS = 10
MIXER_C_OUTS = 3


def _mixer_c_init(sre_ref, sim_ref, car_r, car_i, kt, first_time_tile):
    @pl.when(first_time_tile)
    def _():
        car_r[kt] = sre_ref[...]
        car_i[kt] = sim_ref[...]


def _mixer_c_kernel(*refs, bb, tt):
    ins = refs[:MIXER_C_INS]
    outs = refs[MIXER_C_INS:MIXER_C_INS + MIXER_C_OUTS]
    ystore, car_r, car_i, *scan_scr = refs[MIXER_C_INS + MIXER_C_OUTS:]
    kt = pl.program_id(2)
    _mixer_c_init(ins[8], ins[9], car_r, car_i, kt, pl.program_id(1) == 0)
    _mixer_c_step(*ins[:8], *outs, ystore, car_r, car_i, scan_scr, kt, bb=bb, tt=tt)


def _mixer_c_step(cu_ref, cg_ref, bm_ref, cm_ref, tab_ref, d_ref, wglu_ref, bglu_ref,
                  o_ref, sre_out_ref, sim_out_ref,
                  ystore, car_r, car_i, scan_scr, kt, *, bb, tt, side_matmul=None):
    cw = CH_TILE
    rows = bb * tt
    nblk = tt // SUBLANES
    nv = nblk // SUBLANES
    n = bb * nblk
    base8 = N_TABLES // 2

    u = cu_ref[...].reshape(rows, cw)
    ub = u.astype(BF16)
    y = d_ref[...] * u
    cr_all = car_r[kt]
    ci_all = car_i[kt]
    new_r, new_i = [], []
    for q in range(S5_Q):
        sl = slice(LANES * q, LANES * (q + 1))
        bu = jnp.dot(ub, bm_ref[kt, q], preferred_element_type=F32)
        if side_matmul is not None:
            side_matmul(q, S5_Q)
        br, bi = _block_scan_cplx(bu[:, :LANES].reshape(n, SUBLANES, LANES),
                                  bu[:, LANES:].reshape(n, SUBLANES, LANES), tab_ref, kt, 0, sl)
        p_r = tab_ref[kt, 6, :, sl]
        p_i = tab_ref[kt, 7, :, sl]
        cr = cr_all[:, :, sl]
        ci = ci_all[:, :, sl]
        if nblk == 1:
            kr, ki = cr, ci
        else:
            hsr, hsi, csr, csi = scan_scr
            hsr[...] = br.reshape(bb, tt, LANES)
            hsi[...] = bi.reshape(bb, tt, LANES)
            xr = hsr[:, pl.ds(SUBLANES - 1, nblk, stride=SUBLANES), :]
            xi = hsi[:, pl.ds(SUBLANES - 1, nblk, stride=SUBLANES), :]
            xr, xi = _block_scan_cplx(xr.reshape(bb * nv, SUBLANES, LANES),
                                      xi.reshape(bb * nv, SUBLANES, LANES), tab_ref, kt, base8,
                                      sl)
            xr = xr.reshape(bb, nblk, LANES)
            xi = xi.reshape(bb, nblk, LANES)
            q_r = tab_ref[kt, base8 + 6, :, sl]
            q_i = tab_ref[kt, base8 + 7, :, sl]
            prev_r, prev_i = cr, ci
            ends_r, ends_i = [], []
            for j in range(nv):
                seg = slice(j * SUBLANES, (j + 1) * SUBLANES)
                er = xr[:, seg, :] + (q_r * prev_r - q_i * prev_i)
                ei = xi[:, seg, :] + (q_r * prev_i + q_i * prev_r)
                ends_r.append(er)
                ends_i.append(ei)
                prev_r = er[:, SUBLANES - 1:SUBLANES, :]
                prev_i = ei[:, SUBLANES - 1:SUBLANES, :]
            csr[:, 0:1, :] = cr
            csi[:, 0:1, :] = ci
            csr[:, 1:1 + nblk, :] = jnp.concatenate(ends_r, axis=1)
            csi[:, 1:1 + nblk, :] = jnp.concatenate(ends_i, axis=1)
            kr = _spread_rows(csr, nblk, bb).reshape(n, SUBLANES, LANES)
            ki = _spread_rows(csi, nblk, bb).reshape(n, SUBLANES, LANES)
        hr = br + (p_r * kr - p_i * ki)
        hi = bi + (p_r * ki + p_i * kr)
        hr = hr.reshape(bb, tt, LANES)
        hi = hi.reshape(bb, tt, LANES)
        new_r.append(hr[:, tt - 1:tt, :])
        new_i.append(hi[:, tt - 1:tt, :])
        hcat = jnp.concatenate([hr.reshape(rows, LANES), hi.reshape(rows, LANES)], axis=-1)
        y = y + jnp.dot(hcat.astype(BF16), cm_ref[kt, q], preferred_element_type=F32)

    ncr_all = jnp.concatenate(new_r, axis=-1)
    nci_all = jnp.concatenate(new_i, axis=-1)
    car_r[kt] = ncr_all
    car_i[kt] = nci_all
    sre_out_ref[:, kt] = ncr_all
    sim_out_ref[:, kt] = nci_all
    ystore[kt] = y

    @pl.when(kt == S5_KT - 1)
    def _():
        y_all = jnp.concatenate([ystore[k] for k in range(S5_KT)], axis=-1)
        yg = jax.nn.gelu(y_all)
        z = jnp.dot(yg.astype(BF16), wglu_ref[0].astype(BF16),
                    preferred_element_type=F32) + bglu_ref[...]
        yy = yg * _sigmoid(z)
        out = yy * _silu(cg_ref[...].reshape(rows, W_C))
        o_ref[...] = out.astype(o_ref.dtype)


def _mixer_c_operands(proj3, bmat, cmat, tables, d_skip, w_glu, b_glu, s_re, s_im, layer,
                      bb, tt, idx):
    b, t, _ = proj3.shape
    u_col = 0
    cw = CH_TILE
    nt = t // tt
    nblk = tt // SUBLANES
    kslab = S5_Q * LANES

    def spec(shape, fn):
        return pl.BlockSpec(shape, lambda *g: fn(*idx(*g)))

    mat = spec((S5_KT, S5_Q, cw, cw), lambda i, j, k: (layer, 0, 0, 0))
    st = spec((bb, 1, kslab), lambda i, j, k: (i, 0, k))
    st_out = spec((bb, S5_KT, 1, kslab), lambda i, j, k: (i, 0, 0, 0))
    rows = bb * tt
    scratch = [pltpu.VMEM((S5_KT, rows, cw), F32),
               pltpu.VMEM((S5_KT, bb, 1, kslab), F32),
               pltpu.VMEM((S5_KT, bb, 1, kslab), F32)]
    if nblk > 1:
        assert nblk % SUBLANES == 0 and nblk // SUBLANES <= SUBLANES
        scratch += [pltpu.VMEM((bb, tt, LANES), F32), pltpu.VMEM((bb, tt, LANES), F32),
                    pltpu.VMEM((bb, HALO + nblk, LANES), F32),
                    pltpu.VMEM((bb, HALO + nblk, LANES), F32)]
    vmem = 2 * (_nbytes((bb, tt, cw), F32) + _nbytes((bb, tt, W_C), F32)
                + 2 * _nbytes((S5_KT, S5_Q, cw, cw), BF16)
                + _nbytes((S5_KT, N_TABLES, SUBLANES, kslab), F32)
                + _nbytes((W_C, W_C), F32) + _nbytes((bb, tt, W_C), BF16)
                + 4 * _nbytes((bb, SUBLANES, kslab), F32)) \
        + _nbytes((S5_KT, rows, cw), F32) + 2 * _nbytes((bb, tt, LANES), F32) \
        + 2 * _nbytes((S5_KT, bb, SUBLANES, kslab), F32) + 10 * _nbytes((rows, W_C), F32)
    return dict(
        args=[proj3, proj3, bmat, cmat, tables, d_skip.reshape(1, W_C), w_glu,
              b_glu.reshape(1, W_C), s_re, s_im],
        in_specs=[
            spec((bb, tt, cw), lambda i, j, k: (i, j, u_col // cw + k)),
            spec((bb, tt, W_C), lambda i, j, k: (i, j, u_col // W_C + 1)),
            mat, mat,
            spec((S5_KT, N_TABLES, SUBLANES, kslab), lambda i, j, k: (layer, 0, 0, 0)),
            spec((1, cw), lambda i, j, k: (0, k)),
            spec((1, W_C, W_C), lambda i, j, k: (layer, 0, 0)),
            spec((1, W_C), lambda i, j, k: (0, 0)),
            st, st,
        ],
        out_specs=[
            spec((rows, W_C), lambda i, j, k: (i * nt + j, 0)),
            st_out, st_out,
        ],
        out_shape=[
            jax.ShapeDtypeStruct((b * t, W_C), BF16),
            jax.ShapeDtypeStruct((b, S5_KT, 1, kslab), F32),
            jax.ShapeDtypeStruct((b, S5_KT, 1, kslab), F32),
        ],
        scratch=scratch,
        vmem=vmem,
        grid=(b // bb, nt, S5_KT),
    )


def _mixer_c(proj3, *params, bb, tt):
    ops = _mixer_c_operands(proj3, *params, bb, tt, lambda i, j, k: (i, j, k))
    return pl.pallas_call(
        functools.partial(_mixer_c_kernel, bb=bb, tt=tt),
        grid=ops["grid"],
        in_specs=ops["in_specs"],
        out_specs=ops["out_specs"],
        out_shape=ops["out_shape"],
        scratch_shapes=ops["scratch"],
        compiler_params=_params(("arbitrary",) * 3, ops["vmem"]),
        name="mixer_c",
    )(*ops["args"])


def _flat_mixer_c(proj3, *params, bb, tt):
    b, t, _ = proj3.shape
    dims = (b // bb, t // tt, S5_KT)
    ops = _mixer_c_operands(proj3, *params, bb, tt, _flat_index(dims))
    nt = dims[1]

    def init(ins, scr, s):
        _mixer_c_init(ins[8], ins[9], scr[1], scr[2], lax.rem(s, S5_KT),
                      lax.rem(s // S5_KT, nt) == 0)

    def step(ins, outs, scr, s, side_matmul):
        _mixer_c_step(*ins[:8], *outs, scr[0], scr[1], scr[2], scr[3:],
                      lax.rem(s, S5_KT), bb=bb, tt=tt, side_matmul=side_matmul)

    return ops, init, step


OUT_PROJ_TN = 512


def _out_proj_kernel(ma_ref, mb_ref, mc_ref, w_ref, x_ref, gate_ref, o_ref, *, bb, tt):
    w = w_ref[0].astype(BF16)
    acc = jnp.dot(ma_ref[...], w[0:W_A, :], preferred_element_type=F32)
    acc = acc + jnp.dot(mb_ref[...], w[W_A:W_A + W_B, :], preferred_element_type=F32)
    acc = acc + jnp.dot(mc_ref[...], w[W_A + W_B:, :], preferred_element_type=F32)
    o_ref[...] = x_ref[...] + gate_ref[...] * acc.reshape(bb, tt, acc.shape[-1])


def _out_projection(out_a, out_b, out_c, w_out, x, gate, layer, bb, tt):
    b, t, d = x.shape
    tm = bb * tt
    tn = OUT_PROJ_TN
    nt = t // tt
    m = b * t

    def rows(width):
        return pl.BlockSpec((tm, width), lambda i, j: (i, 0))

    vmem = 2 * (_nbytes((tm, d), BF16) + _nbytes((d, tn), F32)
                + 2 * _nbytes((tm, tn), F32)) + _nbytes((d, tn), BF16) \
        + 3 * _nbytes((tm, tn), F32)
    return pl.pallas_call(
        functools.partial(_out_proj_kernel, bb=bb, tt=tt),
        grid=(m // tm, d // tn),
        in_specs=[
            rows(W_A), rows(W_B), rows(W_C),
            pl.BlockSpec((1, d, tn), lambda i, j: (layer, 0, j)),
            pl.BlockSpec((bb, tt, tn), lambda i, j: (i // nt, i % nt, j)),
            pl.BlockSpec((bb, 1, tn), lambda i, j: (i // nt, 0, j)),
        ],
        out_specs=pl.BlockSpec((bb, tt, tn), lambda i, j: (i // nt, i % nt, j)),
        out_shape=jax.ShapeDtypeStruct((b, t, d), F32),
        compiler_params=_params(("arbitrary", "arbitrary"), vmem + (4 << 20)),
        name="out_projection",
    )(out_a, out_b, out_c, w_out, x, gate)


def _gate_weights(w_rg, w_ig):
    hpt = CH_TILE // LRU_HD
    pairs = LRU_HEADS // hpt
    w = jnp.stack([w_rg, w_ig], axis=2)
    w = w.reshape(pairs, hpt, LRU_HD, 2, LRU_HD)
    eye = jnp.eye(hpt, dtype=bool)
    out = jnp.where(eye[None, :, None, None, :, None],
                    w[:, :, :, :, None, :], 0.0)
    return out.reshape(pairs, CH_TILE, 2 * CH_TILE).astype(BF16)


def _layer(x, mod, states, wts, layer, tiles):
    b, t, d = x.shape
    conv_a, conv_b, lru_h, s5_re, s5_im = states
    shift, scale, gate = (mod[:, None, k * d:(k + 1) * d] for k in range(3))

    h = _norm_modulate(x, scale, shift, wts["g_norm"], *tiles["norm"])
    w_in = wts["w_in"]
    col_a, col_b, col_c = 0, 4 * W_A, 4 * W_A + 2 * W_B
    b_params = (wts["w_conv_b"], wts["b_conv_b"], wts["w_gates"], wts["b_rg"], wts["b_ig"],
                wts["lru_lambda"], conv_b, lru_h.reshape(b, 1, W_B))
    c_params = (wts["bmat"], wts["cmat"], wts["tables"], wts["s5_d"], wts["w_glu"],
                wts["b_glu"], s5_re.reshape(b, 1, S5_STATES), s5_im.reshape(b, 1, S5_STATES),
                layer)
    bb_b, tt_b = tiles["b"]
    bb_c, tt_c = tiles["c"]

    proj_c = _in_projection(h, w_in, layer, col_c, 2 * W_C).reshape(b, t, 2 * W_C)
    if tiles["fuse"]:
        proj_b, out_c, re_new, im_new = _fused_proj_call(
            "proj_b_mixer_c", h, w_in, layer, col_b, 2 * W_B,
            [_flat_mixer_c(proj_c, *c_params, bb=bb_c, tt=tt_c)])
        proj_b = proj_b.reshape(b, t, 2 * W_B)
        proj_a, out_b, conv_b_new, lru_new = _fused_proj_call(
            "proj_a_mixer_b", h, w_in, layer, col_a, 4 * W_A,
            [_flat_mixer_b(proj_b, *b_params, bb=bb_b, tt=tt_b)])
    else:
        out_c, re_new, im_new = _mixer_c(proj_c, *c_params, bb=bb_c, tt=tt_c)
        proj_b = _in_projection(h, w_in, layer, col_b, 2 * W_B).reshape(b, t, 2 * W_B)
        out_b, conv_b_new, lru_new = _mixer_b(proj_b, *b_params, bb=bb_b, tt=tt_b)
        proj_a = _in_projection(h, w_in, layer, col_a, 4 * W_A)
    proj_a = proj_a.reshape(b, t, 4 * W_A)
    out_a, conv_a_new = _mixer_a(proj_a, wts["w_conv_a"], wts["b_conv_a"], conv_a,
                                 *tiles["a"])

    x_new = _out_projection(out_a, out_b, out_c, wts["w_out"], x, gate, layer, *tiles["out"])
    return (x_new, conv_a_new, conv_b_new, lru_new.reshape(b, W_B),
            re_new.reshape(b, S5_G, S5_P), im_new.reshape(b, S5_G, S5_P))


PROMPT_TILES = {"norm": (1, 512), "a": (1, 2048), "b": (1, 1024), "c": (1, 256),
                "out": (1, 1024), "final": (1, 512), "fuse": True}
SAMPLE_TILES = {"norm": (64, 8), "a": (128, 8), "b": (128, 8), "c": (64, 8),
                "out": (128, 8), "final": (64, 8), "fuse": False}


def kernel(x_prompt, x_sample, c_prompt, c_sample, state_conv_a, state_conv_b, state_lru_h,
           state_s5_re, state_s5_im, g_norm, w_ada, b_ada, w_in, w_conv_a, b_conv_a,
           w_conv_b, b_conv_b, w_rg, b_rg, w_ig, b_ig, lru_lambda, s5_lambda_re,
           s5_lambda_im, s5_log_dt, s5_b_re, s5_b_im, s5_c_re, s5_c_im, s5_d, w_glu,
           b_glu, w_out, g_final):
    depth = w_in.shape[0]
    bp = x_prompt.shape[0]
    bs = x_sample.shape[0]

    c_all = jnp.concatenate([c_prompt, c_sample], axis=0)
    pad = (-c_all.shape[0]) % SUBLANES
    c_all = jnp.pad(c_all, ((0, pad), (0, 0)))
    mod = _modulation(c_all, w_ada, b_ada)

    ab_re, ab_im, bb_re, bb_im = _s5_discretize(s5_lambda_re, s5_lambda_im, s5_log_dt,
                                               s5_b_re, s5_b_im)
    tables = _s5_power_tables(ab_re, ab_im)
    bmat, cmat = _s5_block_matrices(bb_re, bb_im, s5_c_re, s5_c_im)
    w_glu_bf16 = w_glu.astype(BF16)

    xp, xs = x_prompt, x_sample
    zeros_p = (jnp.zeros((bp, CONV_A - 1, W_A), F32), jnp.zeros((bp, CONV_B - 1, W_B), F32),
               jnp.zeros((bp, W_B), F32), jnp.zeros((bp, S5_G, S5_P), F32),
               jnp.zeros((bp, S5_G, S5_P), F32))
    outs_p, outs_s = [], []
    for l in range(depth):
        wts = {
            "g_norm": g_norm[l], "w_in": w_in,
            "w_conv_a": w_conv_a[l], "b_conv_a": b_conv_a[l],
            "w_conv_b": w_conv_b[l], "b_conv_b": b_conv_b[l],
            "w_gates": _gate_weights(w_rg[l], w_ig[l]), "b_rg": b_rg[l], "b_ig": b_ig[l],
            "lru_lambda": lru_lambda[l], "bmat": bmat, "cmat": cmat,
            "tables": tables, "s5_d": s5_d[l],
            "w_glu": w_glu_bf16, "b_glu": b_glu[l], "w_out": w_out,
        }
        res_p = _layer(xp, mod[l, :bp], zeros_p, wts, l, PROMPT_TILES)
        xp = res_p[0]
        outs_p.append(res_p[1:])
        st_s = (state_conv_a[l], state_conv_b[l], state_lru_h[l], state_s5_re[l],
                state_s5_im[l])
        res_s = _layer(xs, mod[l, bp:bp + bs], st_s, wts, l, SAMPLE_TILES)
        xs = res_s[0]
        outs_s.append(res_s[1:])

    y_prompt = _final_norm(xp, g_final, *PROMPT_TILES["final"])
    y_sample = _final_norm(xs, g_final, *SAMPLE_TILES["final"])
    stack = lambda outs, k: jnp.stack([o[k] for o in outs])
    return (y_prompt, y_sample,
            *(stack(outs_p, k) for k in range(5)),
            *(stack(outs_s, k) for k in range(5)))
```
